```python
import jax
import jax.numpy as jnp
from jax import lax
import numpy as np

D_MODEL = 2048
BATCH = 4
SEQ = 4096
DEPTH = 2

GRID_W = 64
CTX_LEN = 256
RMS_EPS = 1e-6

HG_HEADS = 8
HG_DK = 128
HG_DV = 128
HG_WIDTH = HG_HEADS * HG_DV
HG_CHUNK = 64
POOL_WINDOWS = (2, 4, 8, 16)
POOL_WIDTH = 1024
POOL_GROUP = POOL_WIDTH // len(POOL_WINDOWS)
MLA_HEADS = 16
Q_LORA = 512
KV_LORA = 512
QK_NOPE = 128
QK_ROPE = 64
V_HEAD = 128
MLA_WIDTH = MLA_HEADS * V_HEAD
MLA_SCALE = (QK_NOPE + QK_ROPE) ** -0.5
ROPE_THETA = 10000.0
Q_BLOCK = 128
N_BRANCH = 3
PEER_HEADS = 8
PEER_NKEYS = 128
PEER_N = PEER_NKEYS * PEER_NKEYS
PEER_DKEY = 256
PEER_TOPK = 16
TOK_BLOCK = 128

IN_SIZES = (HG_HEADS * HG_DK, HG_HEADS * HG_DK, HG_HEADS * HG_DK, HG_WIDTH, HG_WIDTH,
            POOL_WIDTH, Q_LORA, KV_LORA, QK_ROPE, N_BRANCH * D_MODEL)
IN_COLS = sum(IN_SIZES)

kernel_name = 'hybrid_hgrn2_pool_mla_peer_dit_block'


def rms_norm(x, w, eps=RMS_EPS):
    xf = x.astype(jnp.float32)
    y = xf * lax.rsqrt(jnp.mean(xf * xf, axis=-1, keepdims=True) + eps)
    return (y * w.astype(jnp.float32)).astype(x.dtype)


def modulate(x, gain, shift, scale):
    return rms_norm(x, gain) * (1 + scale) + shift


def split_projection(p):
    parts, start = [], 0
    for size in IN_SIZES:
        parts.append(p[..., start:start + size])
        start += size
    return parts


def axial_rope_tables(n_tokens):
    rows = n_tokens // GRID_W
    r, col = jnp.meshgrid(jnp.arange(rows), jnp.arange(GRID_W), indexing='ij')
    n_freq = QK_ROPE // 4
    freqs = ROPE_THETA ** (-jnp.arange(n_freq, dtype=jnp.float32) / n_freq)
    ang = jnp.stack([r.reshape(-1)[:, None] * freqs, col.reshape(-1)[:, None] * freqs], axis=1)
    return jnp.cos(ang), jnp.sin(ang)


def apply_axial_rope(x, cos, sin):
    xs = x.reshape(x.shape[:-1] + (2, 2, QK_ROPE // 4))
    cs = cos[None, :, None].astype(x.dtype)
    sn = sin[None, :, None].astype(x.dtype)
    x1, x2 = xs[..., 0, :], xs[..., 1, :]
    return jnp.stack([x1 * cs - x2 * sn, x2 * cs + x1 * sn], axis=-2).reshape(x.shape)


def hgrn_lower_bounds(logits):
    cum = jnp.cumsum(jax.nn.softmax(logits.astype(jnp.float32), axis=0), axis=0)
    return cum - cum[0:1]


def gla_chunkwise(q, k, v, log_f, s0):
    B_, H_, L, _ = q.shape
    n = L // HG_CHUNK

    def to_chunks(t):
        return jnp.moveaxis(t.reshape(B_, H_, n, HG_CHUNK, t.shape[-1]), 2, 0)

    causal = jnp.tril(jnp.ones((HG_CHUNK, HG_CHUNK), dtype=bool))

    def step(S, chunk):
        qc, kc, vc, lf = chunk
        b = jnp.cumsum(lf, axis=2)
        b_end = b[:, :, -1:, :]
        rel = jnp.where(causal[:, :, None], b[:, :, :, None, :] - b[:, :, None, :, :], -jnp.inf)
        scores = jnp.einsum('bhtc,bhsc,bhtsc->bhts', qc, kc, jnp.exp(rel))
        o = (jnp.einsum('bhts,bhsv->bhtv', scores, vc)
             + jnp.einsum('bhtc,bhcv->bhtv', qc * jnp.exp(b), S))
        S_new = (jnp.exp(b_end[:, :, 0, :])[..., None] * S
                 + jnp.einsum('bhsc,bhsv->bhcv', kc * jnp.exp(b_end - b), vc))
        return S_new, o

    s_final, o = lax.scan(step, s0, (to_chunks(q), to_chunks(k), to_chunks(v), to_chunks(log_f)))
    return jnp.moveaxis(o, 0, 2).reshape(B_, H_, L, v.shape[-1]), s_final


def hgrn_direction(q, f_raw, v, lb, s0, reverse):
    lbf = lb.astype(jnp.float32)
    log_f = jnp.logaddexp(jnp.log(lbf), jnp.log1p(-lbf) + jax.nn.log_sigmoid(f_raw.astype(jnp.float32)))
    k = -jnp.expm1(log_f)

    def heads(t, d):
        t = t.astype(jnp.float32)
        if reverse:
            t = jnp.flip(t, axis=1)
        return jnp.transpose(t.reshape(t.shape[0], t.shape[1], HG_HEADS, d), (0, 2, 1, 3))

    o, s_final = gla_chunkwise(heads(q, HG_DK), heads(k, HG_DK), heads(v, HG_DV), heads(log_f, HG_DK), s0)
    if reverse:
        o = jnp.flip(o, axis=2)
    return o, s_final


def hgrn_readout(o, g, norm_w):
    B_, _, L, _ = o.shape
    o = rms_norm(jnp.transpose(o, (0, 2, 1, 3)), norm_w).reshape(B_, L, HG_WIDTH)
    return (o * jax.nn.silu(g.astype(jnp.float32))).astype(g.dtype)


def hgrn2_mixer(pc, pl, lb_pair, norm_w, need_ctx_out):
    s0 = jnp.zeros((pl[0].shape[0], HG_HEADS, HG_DK, HG_DV), jnp.float32)
    qc, ql = jax.nn.silu(pc[0]), jax.nn.silu(pl[0])
    oc_f, s_fwd = hgrn_direction(qc, pc[1], pc[3], lb_pair[0], s0, False)
    oc_b, s_bwd = hgrn_direction(qc, pc[2], pc[3], lb_pair[1], s0, True)
    ol_f, _ = hgrn_direction(ql, pl[1], pl[3], lb_pair[0], s_fwd, False)
    ol_b, _ = hgrn_direction(ql, pl[2], pl[3], lb_pair[1], s_bwd, True)
    out_l = hgrn_readout(ol_f + ol_b, pl[4], norm_w)
    out_c = hgrn_readout(oc_f + oc_b, pc[4], norm_w) if need_ctx_out else None
    return out_l, out_c


def multiscale_pool(u, w_pool, pool_scale):
    B_, L, _ = u.shape
    uf = u.astype(jnp.float32)
    csum = jnp.concatenate([jnp.zeros((B_, 1, POOL_WIDTH), jnp.float32), jnp.cumsum(uf, axis=1)], axis=1)
    t = jnp.arange(L)
    groups = []
    for gi, win in enumerate(POOL_WINDOWS):
        lo = jnp.clip(t - win // 2, 0, L)
        hi = jnp.clip(t + win // 2, 0, L)
        sl = slice(gi * POOL_GROUP, (gi + 1) * POOL_GROUP)
        cs = csum[:, :, sl]
        mean = (cs[:, hi] - cs[:, lo]) / (hi - lo).astype(jnp.float32)[None, :, None]
        groups.append(mean - uf[:, :, sl])
    pooled = jnp.stack(groups, axis=2)
    mixed = jnp.einsum('blgi,gio->blgo', pooled, w_pool.astype(jnp.float32))
    return (mixed.reshape(B_, L, POOL_WIDTH) * pool_scale.astype(jnp.float32)).astype(u.dtype)


def mla_queries(c_q, q_norm_w, w_uq, rope):
    B_, L, _ = c_q.shape
    q = (rms_norm(c_q, q_norm_w) @ w_uq).reshape(B_, L, MLA_HEADS, QK_NOPE + QK_ROPE)
    if rope is None:
        return q
    return jnp.concatenate([q[..., :QK_NOPE], apply_axial_rope(q[..., QK_NOPE:], *rope)], axis=-1)


def mla_keys_values(c_kv, k_rope_raw, kv_norm_w, w_ukv, rope):
    B_, L, _ = c_kv.shape
    kv = (rms_norm(c_kv, kv_norm_w) @ w_ukv).reshape(B_, L, MLA_HEADS, QK_NOPE + V_HEAD)
    k_rope = k_rope_raw[:, :, None, :]
    if rope is not None:
        k_rope = apply_axial_rope(k_rope, *rope)
    k = jnp.concatenate([kv[..., :QK_NOPE], jnp.broadcast_to(k_rope, (B_, L, MLA_HEADS, QK_ROPE))], axis=-1)
    return k, kv[..., QK_NOPE:]


def block_attention(q, k, v):
    B_, L, H_, Dqk = q.shape
    nb = L // Q_BLOCK
    q_blocks = jnp.moveaxis(q.reshape(B_, nb, Q_BLOCK, H_, Dqk), 1, 0)

    def attend(qb):
        s = jnp.einsum('bqhd,bkhd->bhqk', qb, k).astype(jnp.float32) * MLA_SCALE
        p = jax.nn.softmax(s, axis=-1).astype(v.dtype)
        return jnp.einsum('bhqk,bkhd->bqhd', p, v)

    o = lax.map(attend, q_blocks)
    return jnp.moveaxis(o, 0, 1).reshape(B_, L, H_ * v.shape[-1])


def mla_mixer(pc, pl, q_norm_w, w_uq, kv_norm_w, w_ukv, rope, need_ctx_out):
    k_c, v_c = mla_keys_values(pc[1], pc[2], kv_norm_w, w_ukv, None)
    k_l, v_l = mla_keys_values(pl[1], pl[2], kv_norm_w, w_ukv, rope)
    q_l = mla_queries(pl[0], q_norm_w, w_uq, rope)
    out_l = block_attention(q_l, jnp.concatenate([k_c, k_l], axis=1), jnp.concatenate([v_c, v_l], axis=1))
    out_c = block_attention(mla_queries(pc[0], q_norm_w, w_uq, None), k_c, v_c) if need_ctx_out else None
    return out_l, out_c


def merge_branches(y_a, y_b, y_c, gate_raw, w_a, w_b, w_c, w_o):
    g = jax.nn.sigmoid(gate_raw)
    m = (g[..., :D_MODEL] * (y_a @ w_a)
         + g[..., D_MODEL:2 * D_MODEL] * (y_b @ w_b)
         + g[..., 2 * D_MODEL:] * (y_c @ w_c))
    return m @ w_o


def peer_ffn(h, wq, keys, u_tab, v_tab):
    B_, L, D = h.shape
    tokens = h.reshape(-1, TOK_BLOCK, D)

    def one_block(hb):
        T = hb.shape[0]
        qh = (hb @ wq).reshape(T, PEER_HEADS, 2, PEER_DKEY // 2)
        s = jnp.einsum('thpd,phkd->thpk', qh, keys).astype(jnp.float32)
        s_top, i_top = lax.top_k(s, PEER_TOPK)
        cand = (s_top[:, :, 0, :, None] + s_top[:, :, 1, None, :]).reshape(T, PEER_HEADS, -1)
        cand_idx = (i_top[:, :, 0, :, None] * PEER_NKEYS + i_top[:, :, 1, None, :]).reshape(T, PEER_HEADS, -1)
        best, pos = lax.top_k(cand, PEER_TOPK)
        idx = jnp.take_along_axis(cand_idx, pos, axis=-1)
        gate = jax.nn.softmax(best, axis=-1)
        u = jnp.take(u_tab, idx, axis=0)
        act = jnp.einsum('thkd,td->thk', u, hb).astype(jnp.float32)
        w = (gate * jax.nn.gelu(act, approximate=False)).astype(v_tab.dtype)
        return jnp.einsum('thk,thkd->td', w, jnp.take(v_tab, idx, axis=0))

    return lax.map(one_block, tokens).reshape(B_, L, D)


def setup_inputs(seed: int = 0) -> dict:
    key = jax.random.key(seed)
    ks = jax.random.split(key, 32)
    D = D_MODEL

    def nrm(i, shape, std):
        return jax.random.normal(ks[i], shape, jnp.float32) * std

    def gain(i, shape):
        return 1.0 + nrm(i, shape, 0.05)

    return {
        'x': nrm(0, (BATCH, SEQ, D), 1.0),
        'c': nrm(1, (BATCH, D), 1.0),
        'ctx': nrm(2, (BATCH, CTX_LEN, D), 1.0),
        'c_ctx': nrm(3, (D,), 1.0),
        'w_mod': nrm(4, (DEPTH, D, 6 * D), 0.5 * D ** -0.5),
        'b_mod': nrm(5, (DEPTH, 6 * D), 0.01),
        'norm_mix': gain(6, (DEPTH, D)),
        'norm_ffn': gain(7, (DEPTH, D)),
        'w_in': nrm(8, (DEPTH, D, IN_COLS), D ** -0.5),
        'hg_lb_logits': nrm(9, (DEPTH, 2, HG_HEADS * HG_DK), 1.0),
        'hg_norm': gain(10, (DEPTH, HG_DV)),
        'pool_w': nrm(11, (DEPTH, len(POOL_WINDOWS), POOL_GROUP, POOL_GROUP), POOL_GROUP ** -0.5),
        'pool_scale': gain(12, (DEPTH, POOL_WIDTH)),
        'mla_q_norm': gain(13, (DEPTH, Q_LORA)),
        'mla_w_uq': nrm(14, (DEPTH, Q_LORA, MLA_HEADS * (QK_NOPE + QK_ROPE)), Q_LORA ** -0.5),
        'mla_kv_norm': gain(15, (DEPTH, KV_LORA)),
        'mla_w_ukv': nrm(16, (DEPTH, KV_LORA, MLA_HEADS * (QK_NOPE + V_HEAD)), KV_LORA ** -0.5),
        'w_branch_a': nrm(17, (DEPTH, HG_WIDTH, D), HG_WIDTH ** -0.5),
        'w_branch_b': nrm(18, (DEPTH, POOL_WIDTH, D), POOL_WIDTH ** -0.5),
        'w_branch_c': nrm(19, (DEPTH, MLA_WIDTH, D), MLA_WIDTH ** -0.5),
        'w_out': nrm(20, (DEPTH, D, D), D ** -0.5),
        'peer_wq': nrm(21, (DEPTH, D, PEER_HEADS * PEER_DKEY), D ** -0.5),
        'peer_keys': nrm(22, (DEPTH, 2, PEER_HEADS, PEER_NKEYS, PEER_DKEY // 2), (PEER_DKEY // 2) ** -0.5),
        'peer_u': nrm(23, (DEPTH, PEER_N, D), D ** -0.5),
        'peer_v': nrm(24, (DEPTH, PEER_N, D), PEER_HEADS ** -0.5),
        'final_norm': gain(25, (D,)),
    }


def reference(x, c, ctx, c_ctx, w_mod, b_mod, norm_mix, norm_ffn, w_in, hg_lb_logits, hg_norm,
              pool_w, pool_scale, mla_q_norm, mla_w_uq, mla_kv_norm, mla_w_ukv,
              w_branch_a, w_branch_b, w_branch_c, w_out, peer_wq, peer_keys, peer_u, peer_v, final_norm):
    rope = axial_rope_tables(x.shape[1])
    lower_bounds = hgrn_lower_bounds(hg_lb_logits)
    xc = ctx
    for l in range(DEPTH):
        last = l == DEPTH - 1
        mod_l = jnp.split((jax.nn.silu(c) @ w_mod[l] + b_mod[l])[:, None, :], 6, axis=-1)
        mod_c = jnp.split((jax.nn.silu(c_ctx) @ w_mod[l] + b_mod[l])[None, None, :], 6, axis=-1)
        h_l = modulate(x, norm_mix[l], mod_l[0], mod_l[1])
        h_c = modulate(xc, norm_mix[l], mod_c[0], mod_c[1])
        p_l = split_projection(h_l @ w_in[l])
        p_c = split_projection(h_c @ w_in[l])
        hg_l, hg_c = hgrn2_mixer(p_c[0:5], p_l[0:5], lower_bounds[l], hg_norm[l], not last)
        att_l, att_c = mla_mixer(p_c[6:9], p_l[6:9], mla_q_norm[l], mla_w_uq[l], mla_kv_norm[l],
                                 mla_w_ukv[l], rope, not last)
        pool_l = multiscale_pool(p_l[5], pool_w[l], pool_scale[l])
        x = x + mod_l[2] * merge_branches(hg_l, pool_l, att_l, p_l[9],
                                          w_branch_a[l], w_branch_b[l], w_branch_c[l], w_out[l])
        h2_l = modulate(x, norm_ffn[l], mod_l[3], mod_l[4])
        x = x + mod_l[5] * peer_ffn(h2_l, peer_wq[l], peer_keys[l], peer_u[l], peer_v[l])
        if not last:
            pool_c = multiscale_pool(p_c[5], pool_w[l], pool_scale[l])
            xc = xc + mod_c[2] * merge_branches(hg_c, pool_c, att_c, p_c[9],
                                                w_branch_a[l], w_branch_b[l], w_branch_c[l], w_out[l])
            h2_c = modulate(xc, norm_ffn[l], mod_c[3], mod_c[4])
            xc = xc + mod_c[5] * peer_ffn(h2_c, peer_wq[l], peer_keys[l], peer_u[l], peer_v[l])
    return rms_norm(x, final_norm)
```

```python
import functools
import math
from typing import NamedTuple

import numpy as np
import jax
import jax.numpy as jnp
from jax import lax
from jax.experimental import pallas as pl
from jax.experimental.pallas import tpu as pltpu

F32 = jnp.float32
BF16 = jnp.bfloat16

RMS_EPS = 1e-6
ROPE_THETA = 10000.0
HEAD_DIM = 128
ROPE_DIM = 64
QK_DIM = 256
POOL_WINDOWS = (2, 4, 8, 16)
POOL_GROUP = 256
POOL_WIDTH = POOL_GROUP * len(POOL_WINDOWS)
POOL_HALO = 8
PEER_NKEYS = 128
PEER_TOPK = 16
HG_CHUNK = 128
HG_LEVELS = 7
UNRANKED = 99.0
MOD_ROWS = 8
VMEM_LIMIT = 56 * 1024 * 1024


class Cfg(NamedTuple):
    d: int
    batch: int
    seq: int
    ctx: int
    grid_w: int
    depth: int
    hg_heads: int
    mla_heads: int
    q_lora: int
    kv_lora: int
    peer_heads: int

    @property
    def nl(self):
        return self.batch * self.seq

    @property
    def nc(self):
        return self.batch * self.ctx

    @property
    def nt(self):
        return self.nl + self.nc

    @property
    def hgw(self):
        return self.hg_heads * HEAD_DIM

    @property
    def off_gate(self):
        return 0

    @property
    def off_hg(self):
        return 3 * self.d

    @property
    def off_pool(self):
        return self.off_hg + 5 * self.hgw

    @property
    def off_cq(self):
        return self.off_pool + POOL_WIDTH

    @property
    def off_ckv(self):
        return self.off_cq + self.q_lora

    @property
    def off_slab(self):
        return self.off_ckv + self.kv_lora

    @property
    def in_cols(self):
        return self.off_slab + HEAD_DIM


def _params(sem):
    return pltpu.CompilerParams(dimension_semantics=sem, vmem_limit_bytes=VMEM_LIMIT)


def _dot(a, b):
    return jnp.dot(a, b, preferred_element_type=F32)


def _dot_nt(a, b):
    return lax.dot_general(a, b, (((1,), (1,)), ((), ())), preferred_element_type=F32)


def _sigmoid(x):
    return 1.0 / (1.0 + jnp.exp(-x))


def _pick(n, prefs):
    for p in prefs:
        if n % p == 0:
            return p
    raise ValueError(f"no tile for {n} in {prefs}")


def _mod_row_map(cfg, tm):
    n_lat = cfg.nl // tm
    per_batch = cfg.seq // tm
    return lambda i: jnp.where(i < n_lat, i // per_batch, cfg.batch)


def _mod_kernel(c_ref, w_ref, b_ref, o_ref):
    c = c_ref[...]
    s = (c * _sigmoid(c)).astype(BF16)
    o_ref[...] = _dot(s, w_ref[...].astype(BF16)) + b_ref[...]


def adaln_tables(cfg, c_all, w_mod, b_mod):
    d6 = 6 * cfg.d
    tn = _pick(d6, (1024, 768, 512, 256))
    out = pl.pallas_call(
        _mod_kernel,
        grid=(cfg.depth, d6 // tn),
        in_specs=[
            pl.BlockSpec((MOD_ROWS, cfg.d), lambda l, j: (0, 0)),
            pl.BlockSpec((None, cfg.d, tn), lambda l, j: (l, 0, j)),
            pl.BlockSpec((None, 1, tn), lambda l, j: (l, 0, j)),
        ],
        out_specs=pl.BlockSpec((None, MOD_ROWS, tn), lambda l, j: (l, 0, j)),
        out_shape=jax.ShapeDtypeStruct((cfg.depth, MOD_ROWS, d6), F32),
        compiler_params=_params(("arbitrary", "arbitrary")),
        name="adaln_tables",
    )(c_all, w_mod, b_mod.reshape(cfg.depth, 1, d6))
    return out.reshape(cfg.depth, MOD_ROWS, 6, cfg.d)


def _norm_matmul_kernel(x_ref, gain_ref, mod_ref, w_ref, *rest, k0, emit_h):
    if emit_h:
        o_ref, hout_ref, h_ref = rest
    else:
        o_ref, h_ref = rest

    @pl.when(pl.program_id(1) == 0)
    def _():
        x = x_ref[...]
        y = x * lax.rsqrt(jnp.mean(x * x, axis=-1, keepdims=True) + RMS_EPS) * gain_ref[...]
        h = (y * (1.0 + mod_ref[k0 + 1:k0 + 2, :]) + mod_ref[k0:k0 + 1, :]).astype(BF16)
        h_ref[...] = h
        if emit_h:
            hout_ref[...] = h

    o_ref[...] = _dot(h_ref[...], w_ref[...])


def norm_matmul(cfg, x, gain, mod_l, w, *, k0, rows, tm, tn, emit_h):
    n = w.shape[1]
    row_of = _mod_row_map(cfg, tm)
    out_shape = [jax.ShapeDtypeStruct((rows, n), F32)]
    out_specs = [pl.BlockSpec((tm, tn), lambda i, j: (i, j))]
    if emit_h:
        out_shape.append(jax.ShapeDtypeStruct((rows, cfg.d), BF16))
        out_specs.append(pl.BlockSpec((tm, cfg.d), lambda i, j: (i, 0)))
    res = pl.pallas_call(
        functools.partial(_norm_matmul_kernel, k0=k0, emit_h=emit_h),
        grid=(rows // tm, n // tn),
        in_specs=[
            pl.BlockSpec((tm, cfg.d), lambda i, j: (i, 0)),
            pl.BlockSpec((1, cfg.d), lambda i, j: (0, 0)),
            pl.BlockSpec((None, 6, cfg.d), lambda i, j: (row_of(i), 0, 0)),
            pl.BlockSpec((cfg.d, tn), lambda i, j: (0, j)),
        ],
        out_specs=out_specs,
        out_shape=out_shape,
        scratch_shapes=[pltpu.VMEM((tm, cfg.d), BF16)],
        compiler_params=_params(("arbitrary", "arbitrary")),
        name="norm_matmul_h" if emit_h else "norm_matmul",
    )(x, gain.reshape(1, cfg.d), mod_l, w)
    return res if emit_h else res[0]


def _hgrn_consts():
    c = HG_CHUNK
    out = []
    for rev in (False, True):
        p = np.arange(c) if not rev else c - 1 - np.arange(c)
        pt, pu = p[:, None], p[None, :]
        g = np.zeros((HG_LEVELS + 1, c, c), np.float32)
        up = np.zeros((HG_LEVELS, c, c), np.float32)
        g[0] = pu <= pt
        for l in range(HG_LEVELS):
            m = 1 << l
            blk = p >> (l + 1)
            upper = ((p >> l) & 1) == 1
            mid = (blk * 2 * m + m)[:, None]
            same = blk[:, None] == blk[None, :]
            g_up = same & (pu >= mid) & (pu <= pt)
            g_lo = same & (pu > pt) & (pu < mid)
            g[1 + l] = np.where(upper[:, None], g_up, g_lo)
            up[l] = np.broadcast_to(upper[:, None], (c, c))
        x = pt ^ pu
        lv = np.where(pu < pt, np.floor(np.log2(np.maximum(x, 1))), -1.0).astype(np.float32)
        out.append((jnp.asarray(g.reshape(-1, c), BF16), jnp.asarray(up, F32), jnp.asarray(lv, F32)))
    return out


def _hgrn_chunk(q_raw, v, f_raw, log_lb, log_1mlb, one_m_lb, g_ref, up_ref, lv_ref, st_ref, end_row):
    c = HG_CHUNK
    q = q_raw * _sigmoid(q_raw)
    log_sig = jnp.minimum(f_raw, 0.0) - jnp.log1p(jnp.exp(-jnp.abs(f_raw)))
    t = log_1mlb + log_sig
    log_f = jnp.maximum(log_lb, t) + jnp.log1p(jnp.exp(-jnp.abs(log_lb - t)))
    k = one_m_lb / (1.0 + jnp.exp(f_raw))
    hi = log_f.astype(BF16)
    lo = (log_f - hi.astype(F32)).astype(BF16)
    a2 = _dot(g_ref[...], jnp.concatenate([hi, lo], axis=1))
    a = a2[:, :HEAD_DIM] + a2[:, HEAD_DIM:]
    b = a[0:c]
    lv = lv_ref[...]
    scores = jnp.zeros((c, c), F32)
    for l in range(HG_LEVELS):
        e = jnp.exp(a[(1 + l) * c:(2 + l) * c])
        x = (jnp.where(up_ref[l] > 0.5, q, k) * e).astype(BF16)
        scores = jnp.where(lv == float(l), _dot_nt(x, x), scores)
    b_end = b[end_row:end_row + 1, :]
    qb = (q * jnp.exp(b)).astype(BF16)
    kd = (k * jnp.exp(b_end - b)).astype(BF16)
    st = st_ref[...]
    vb = v.astype(BF16)
    o = (_dot(scores.astype(BF16), vb) + _dot_nt(qb, st.astype(BF16))
         + jnp.sum(q * k, axis=-1, keepdims=True) * v)
    st_ref[...] = st * jnp.exp(b_end) + _dot(v.T.astype(BF16), kd)
    return o


def _hgrn_kernel(ql, ffl, fbl, il, gl, qc, ffc, fbc, ic, gc, logit_ref, nw_ref,
                 gf_ref, upf_ref, lvf_ref, gb_ref, upb_ref, lvb_ref,
                 yl_ref, yc_ref, ofl, obl, ofc, obc, stf, stb, *, layer, seq, ctx):
    c = HG_CHUNK
    depth = logit_ref.shape[0]
    lg = [logit_ref[dd] for dd in range(depth)]
    mx = functools.reduce(jnp.maximum, lg)
    ex = [jnp.exp(v - mx) for v in lg]
    tot = functools.reduce(jnp.add, ex)
    cum = [ex[0] / tot]
    for dd in range(1, layer + 1):
        cum.append(cum[-1] + ex[dd] / tot)
    lb = cum[layer] - cum[0]
    log_lb = jnp.log(lb)
    log_1mlb = jnp.log1p(-lb)
    one_m_lb = 1.0 - lb

    stf[...] = jnp.zeros_like(stf)
    stb[...] = jnp.zeros_like(stb)

    def segment(q_ref, ff_ref, fb_ref, i_ref, of_ref, ob_ref, n):
        def body(j, carry):
            rf = pl.multiple_of(j * c, c)
            rb = pl.multiple_of((n - 1 - j) * c, c)
            of_ref[pl.ds(rf, c), :] = _hgrn_chunk(
                q_ref[pl.ds(rf, c), :], i_ref[pl.ds(rf, c), :], ff_ref[pl.ds(rf, c), :],
                log_lb[0:1], log_1mlb[0:1], one_m_lb[0:1], gf_ref, upf_ref, lvf_ref, stf, c - 1)
            ob_ref[pl.ds(rb, c), :] = _hgrn_chunk(
                q_ref[pl.ds(rb, c), :], i_ref[pl.ds(rb, c), :], fb_ref[pl.ds(rb, c), :],
                log_lb[1:2], log_1mlb[1:2], one_m_lb[1:2], gb_ref, upb_ref, lvb_ref, stb, 0)
            return carry
        lax.fori_loop(0, n, body, 0)

    segment(qc, ffc, fbc, ic, ofc, obc, ctx // c)
    segment(ql, ffl, fbl, il, ofl, obl, seq // c)

    nw = nw_ref[...]

    def readout(of_ref, ob_ref, g_ref, y_ref, n):
        def body(j, carry):
            r = pl.multiple_of(j * c, c)
            o = of_ref[pl.ds(r, c), :] + ob_ref[pl.ds(r, c), :]
            y = o * lax.rsqrt(jnp.mean(o * o, axis=-1, keepdims=True) + RMS_EPS) * nw
            g = g_ref[pl.ds(r, c), :]
            y_ref[pl.ds(r, c), :] = (y * (g * _sigmoid(g))).astype(y_ref.dtype)
            return carry
        lax.fori_loop(0, n, body, 0)

    readout(ofc, obc, gc, yc_ref, ctx // c)
    readout(ofl, obl, gl, yl_ref, seq // c)


def hgrn_mixer(cfg, p, logits, norm_w, layer):
    hd = HEAD_DIM
    nh = cfg.hg_heads
    col0 = cfg.off_hg // hd
    ctx_blk0 = cfg.nl // cfg.ctx
    (gf, upf, lvf), (gb, upb, lvb) = _hgrn_consts()

    def lat_spec(part):
        return pl.BlockSpec((cfg.seq, hd), lambda b, h, part=part: (b, col0 + part * nh + h))

    def ctx_spec(part):
        return pl.BlockSpec((cfg.ctx, hd), lambda b, h, part=part: (ctx_blk0 + b, col0 + part * nh + h))

    parts = (0, 1, 2, 3, 4)

    def const(arr):
        return pl.BlockSpec(arr.shape, lambda b, h, nd=arr.ndim: (0,) * nd)

    y_lat, y_ctx = pl.pallas_call(
        functools.partial(_hgrn_kernel, layer=layer, seq=cfg.seq, ctx=cfg.ctx),
        grid=(cfg.batch, nh),
        in_specs=[lat_spec(k) for k in parts] + [ctx_spec(k) for k in parts] + [
            pl.BlockSpec((cfg.depth, 2, hd), lambda b, h: (0, 0, h)),
            pl.BlockSpec((1, hd), lambda b, h: (0, 0)),
            const(gf), const(upf), const(lvf), const(gb), const(upb), const(lvb),
        ],
        out_specs=[
            pl.BlockSpec((cfg.seq, hd), lambda b, h: (b, h)),
            pl.BlockSpec((cfg.ctx, hd), lambda b, h: (b, h)),
        ],
        out_shape=[
            jax.ShapeDtypeStruct((cfg.nl, cfg.hgw), BF16),
            jax.ShapeDtypeStruct((cfg.nc, cfg.hgw), BF16),
        ],
        scratch_shapes=[
            pltpu.VMEM((cfg.seq, hd), F32), pltpu.VMEM((cfg.seq, hd), F32),
            pltpu.VMEM((cfg.ctx, hd), F32), pltpu.VMEM((cfg.ctx, hd), F32),
            pltpu.VMEM((hd, hd), F32), pltpu.VMEM((hd, hd), F32),
        ],
        compiler_params=_params(("arbitrary", "arbitrary")),
        name="hgrn_mixer",
    )(*([p] * 10), logits, norm_w.reshape(1, hd), gf, upf, lvf, gb, upb, lvb)
    return y_lat, y_ctx


def _pool_kernel(prev_ref, cur_ref, next_ref, w_ref, scale_ref, o_ref, buf_ref, *, seq_len, tile, n_tiles):
    i = pl.program_id(1)
    h = POOL_HALO
    cur = cur_ref[...]
    buf_ref[0:h, :] = jnp.where(i > 0, prev_ref[...], 0.0)
    buf_ref[h:h + tile, :] = cur
    buf_ref[h + tile:2 * h + tile, :] = jnp.where(i < n_tiles - 1, next_ref[...], 0.0)
    pos = i * tile + lax.broadcasted_iota(jnp.int32, (tile, POOL_GROUP), 0)
    for gi, win in enumerate(POOL_WINDOWS):
        half = win // 2
        cols = slice(gi * POOL_GROUP, (gi + 1) * POOL_GROUP)
        acc = buf_ref[h - half:h - half + tile, cols]
        for dlt in range(-half + 1, half):
            acc = acc + buf_ref[h + dlt:h + dlt + tile, cols]
        cnt = (jnp.minimum(pos + half, seq_len) - jnp.maximum(pos - half, 0)).astype(F32)
        pooled = acc / cnt - cur[:, cols]
        mixed = _dot(pooled.astype(BF16), w_ref[gi])
        o_ref[:, cols] = (mixed * scale_ref[:, cols]).astype(o_ref.dtype)


def pool_mixer(cfg, p, w_pool, scale, *, n_seq, seq_len, row0):
    tile = _pick(seq_len, (512, 256, 128))
    n_tiles = seq_len // tile
    blk0 = row0 // tile
    col = cfg.off_pool // POOL_WIDTH
    per8 = tile // POOL_HALO
    last8 = cfg.nt // POOL_HALO - 1

    def cur_map(s, i):
        return (blk0 + s * n_tiles + i, col)

    def prev_map(s, i):
        return (jnp.maximum((blk0 + s * n_tiles + i) * per8 - 1, 0), col)

    def next_map(s, i):
        return (jnp.minimum((blk0 + s * n_tiles + i + 1) * per8, last8), col)

    return pl.pallas_call(
        functools.partial(_pool_kernel, seq_len=seq_len, tile=tile, n_tiles=n_tiles),
        grid=(n_seq, n_tiles),
        in_specs=[
            pl.BlockSpec((POOL_HALO, POOL_WIDTH), prev_map),
            pl.BlockSpec((tile, POOL_WIDTH), cur_map),
            pl.BlockSpec((POOL_HALO, POOL_WIDTH), next_map),
            pl.BlockSpec((len(POOL_WINDOWS), POOL_GROUP, POOL_GROUP), lambda s, i: (0, 0, 0)),
            pl.BlockSpec((1, POOL_WIDTH), lambda s, i: (0, 0)),
        ],
        out_specs=pl.BlockSpec((tile, POOL_WIDTH), lambda s, i: (s * n_tiles + i, 0)),
        out_shape=jax.ShapeDtypeStruct((n_seq * seq_len, POOL_WIDTH), BF16),
        scratch_shapes=[pltpu.VMEM((tile + 2 * POOL_HALO, POOL_WIDTH), F32)],
        compiler_params=_params(("arbitrary", "arbitrary")),
        name="pool_mixer",
    )(p, p, p, w_pool, scale.reshape(1, POOL_WIDTH))


def _rope_slab(slab, cs):
    t = slab * cs
    r = t + pltpu.roll(t, ROPE_DIM, axis=1)
    lane = lax.broadcasted_iota(jnp.int32, r.shape, 1)
    return jnp.where(lane < ROPE_DIM, r, 0.0)


def _rms_bf16(x, w):
    return (x * lax.rsqrt(jnp.mean(x * x, axis=-1, keepdims=True) + RMS_EPS) * w).astype(BF16)


def _mla_q_kernel(cq_ref, nw_ref, cs_ref, w_ref, o_ref, n_ref):
    @pl.when(pl.program_id(1) == 0)
    def _():
        n_ref[...] = _rms_bf16(cq_ref[...], nw_ref[...])

    y = _dot(n_ref[...], w_ref[...])
    rope = _rope_slab(y[:, HEAD_DIM:], cs_ref[...])
    o_ref[...] = jnp.concatenate([y[:, :HEAD_DIM], rope], axis=1).astype(o_ref.dtype)


def mla_queries(cfg, p, norm_w, w_q, cs):
    tm = _pick(cfg.nc, (1024, 512, 256))
    nh = cfg.mla_heads
    col = cfg.off_cq // cfg.q_lora
    return pl.pallas_call(
        _mla_q_kernel,
        grid=(cfg.nt // tm, nh),
        in_specs=[
            pl.BlockSpec((tm, cfg.q_lora), lambda i, h: (i, col)),
            pl.BlockSpec((1, cfg.q_lora), lambda i, h: (0, 0)),
            pl.BlockSpec((tm, HEAD_DIM), lambda i, h: (i, 0)),
            pl.BlockSpec((None, cfg.q_lora, QK_DIM), lambda i, h: (h, 0, 0)),
        ],
        out_specs=pl.BlockSpec((None, tm, QK_DIM), lambda i, h: (h, i, 0)),
        out_shape=jax.ShapeDtypeStruct((nh, cfg.nt, QK_DIM), BF16),
        scratch_shapes=[pltpu.VMEM((tm, cfg.q_lora), BF16)],
        compiler_params=_params(("arbitrary", "arbitrary")),
        name="mla_queries",
    )(p, norm_w.reshape(1, cfg.q_lora), cs, w_q)


def _mla_kv_kernel(ckv_ref, slab_ref, nw_ref, cs_ref, w_ref, k_ref, v_ref, n_ref, r_ref):
    @pl.when(pl.program_id(1) == 0)
    def _():
        n_ref[...] = _rms_bf16(ckv_ref[...], nw_ref[...])
        r_ref[...] = _rope_slab(slab_ref[...], cs_ref[...])

    y = _dot(n_ref[...], w_ref[...])
    k_ref[...] = jnp.concatenate([y[:, :HEAD_DIM], r_ref[...]], axis=1).astype(k_ref.dtype)
    v_ref[...] = y[:, HEAD_DIM:].astype(v_ref.dtype)


def mla_keys_values(cfg, p, norm_w, w_kv, cs):
    tm = _pick(cfg.nc, (1024, 512, 256))
    nh = cfg.mla_heads
    col = cfg.off_ckv // cfg.kv_lora
    slab_col = cfg.off_slab // HEAD_DIM
    return pl.pallas_call(
        _mla_kv_kernel,
        grid=(cfg.nt // tm, nh),
        in_specs=[
            pl.BlockSpec((tm, cfg.kv_lora), lambda i, h: (i, col)),
            pl.BlockSpec((tm, HEAD_DIM), lambda i, h: (i, slab_col)),
            pl.BlockSpec((1, cfg.kv_lora), lambda i, h: (0, 0)),
            pl.BlockSpec((tm, HEAD_DIM), lambda i, h: (i, 0)),
            pl.BlockSpec((None, cfg.kv_lora, 2 * HEAD_DIM), lambda i, h: (h, 0, 0)),
        ],
        out_specs=[
            pl.BlockSpec((None, tm, QK_DIM), lambda i, h: (h, i, 0)),
            pl.BlockSpec((None, tm, HEAD_DIM), lambda i, h: (h, i, 0)),
        ],
        out_shape=[
            jax.ShapeDtypeStruct((nh, cfg.nt, QK_DIM), BF16),
            jax.ShapeDtypeStruct((nh, cfg.nt, HEAD_DIM), BF16),
        ],
        scratch_shapes=[pltpu.VMEM((tm, cfg.kv_lora), BF16), pltpu.VMEM((tm, HEAD_DIM), F32)],
        compiler_params=_params(("arbitrary", "arbitrary")),
        name="mla_keys_values",
    )(p, p, norm_w.reshape(1, cfg.kv_lora), cs, w_kv)


def _attn_lat_kernel(q_ref, kl_ref, kc_ref, vl_ref, vc_ref, o_ref, *, scale):
    q = q_ref[...]
    sl = _dot_nt(q, kl_ref[...])
    sc = _dot_nt(q, kc_ref[...])
    m = jnp.maximum(jnp.max(sl, axis=-1, keepdims=True), jnp.max(sc, axis=-1, keepdims=True))
    pl_ = jnp.exp((sl - m) * scale)
    pc = jnp.exp((sc - m) * scale)
    denom = jnp.sum(pl_, axis=-1, keepdims=True) + jnp.sum(pc, axis=-1, keepdims=True)
    o = _dot(pl_.astype(BF16), vl_ref[...]) + _dot(pc.astype(BF16), vc_ref[...])
    o_ref[...] = (o / denom).astype(o_ref.dtype)


def _attn_ctx_kernel(q_ref, kc_ref, vc_ref, o_ref, *, scale):
    s = _dot_nt(q_ref[...], kc_ref[...])
    m = jnp.max(s, axis=-1, keepdims=True)
    p = jnp.exp((s - m) * scale)
    o = _dot(p.astype(BF16), vc_ref[...])
    o_ref[...] = (o / jnp.sum(p, axis=-1, keepdims=True)).astype(o_ref.dtype)


def mla_attention(cfg, q, k, v, *, with_ctx):
    nh = cfg.mla_heads
    scale = float((HEAD_DIM + ROPE_DIM) ** -0.5)
    tq = _pick(cfg.seq, (512, 256, 128))
    nq = cfg.seq // tq
    cb0 = cfg.nl // cfg.ctx
    y_lat = pl.pallas_call(
        functools.partial(_attn_lat_kernel, scale=scale),
        grid=(cfg.batch, nh, nq),
        in_specs=[
            pl.BlockSpec((None, tq, QK_DIM), lambda b, h, i: (h, b * nq + i, 0)),
            pl.BlockSpec((None, cfg.seq, QK_DIM), lambda b, h, i: (h, b, 0)),
            pl.BlockSpec((None, cfg.ctx, QK_DIM), lambda b, h, i: (h, cb0 + b, 0)),
            pl.BlockSpec((None, cfg.seq, HEAD_DIM), lambda b, h, i: (h, b, 0)),
            pl.BlockSpec((None, cfg.ctx, HEAD_DIM), lambda b, h, i: (h, cb0 + b, 0)),
        ],
        out_specs=pl.BlockSpec((tq, HEAD_DIM), lambda b, h, i: (b * nq + i, h)),
        out_shape=jax.ShapeDtypeStruct((cfg.nl, nh * HEAD_DIM), BF16),
        compiler_params=_params(("arbitrary", "arbitrary", "arbitrary")),
        name="mla_attention",
    )(q, k, k, v, v)
    if not with_ctx:
        return y_lat, None
    y_ctx = pl.pallas_call(
        functools.partial(_attn_ctx_kernel, scale=scale),
        grid=(cfg.batch, nh),
        in_specs=[
            pl.BlockSpec((None, cfg.ctx, QK_DIM), lambda b, h: (h, cb0 + b, 0)),
            pl.BlockSpec((None, cfg.ctx, QK_DIM), lambda b, h: (h, cb0 + b, 0)),
            pl.BlockSpec((None, cfg.ctx, HEAD_DIM), lambda b, h: (h, cb0 + b, 0)),
        ],
        out_specs=pl.BlockSpec((cfg.ctx, HEAD_DIM), lambda b, h: (b, h)),
        out_shape=jax.ShapeDtypeStruct((cfg.nc, nh * HEAD_DIM), BF16),
        compiler_params=_params(("arbitrary", "arbitrary")),
        name="mla_attention_ctx",
    )(q, k, v)
    return y_lat, y_ctx


def _merge_kernel(ya_ref, yb_ref, yc_ref, ga_ref, gb_ref, gc_ref, wa_ref, wb_ref, wc_ref, o_ref):
    m = (_sigmoid(ga_ref[...]) * _dot(ya_ref[...], wa_ref[...])
         + _sigmoid(gb_ref[...]) * _dot(yb_ref[...], wb_ref[...])
         + _sigmoid(gc_ref[...]) * _dot(yc_ref[...], wc_ref[...]))
    o_ref[...] = m.astype(o_ref.dtype)


def merge_branches(cfg, ya, yb, yc, p, wa, wb, wc, *, rows):
    tm = _pick(cfg.nc, (512, 256))
    tn = _pick(cfg.d, (512, 256))
    gcols = cfg.d // tn

    def gate_spec(k):
        return pl.BlockSpec((tm, tn), lambda i, j, k=k: (i, k * gcols + j))

    return pl.pallas_call(
        _merge_kernel,
        grid=(rows // tm, cfg.d // tn),
        in_specs=[
            pl.BlockSpec((tm, ya.shape[1]), lambda i, j: (i, 0)),
            pl.BlockSpec((tm, yb.shape[1]), lambda i, j: (i, 0)),
            pl.BlockSpec((tm, yc.shape[1]), lambda i, j: (i, 0)),
            gate_spec(0), gate_spec(1), gate_spec(2),
            pl.BlockSpec((wa.shape[0], tn), lambda i, j: (0, j)),
            pl.BlockSpec((wb.shape[0], tn), lambda i, j: (0, j)),
            pl.BlockSpec((wc.shape[0], tn), lambda i, j: (0, j)),
        ],
        out_specs=pl.BlockSpec((tm, tn), lambda i, j: (i, j)),
        out_shape=jax.ShapeDtypeStruct((rows, cfg.d), BF16),
        compiler_params=_params(("arbitrary", "arbitrary")),
        name="merge_branches",
    )(ya, yb, yc, p, p, p, wa, wb, wc)


def _matmul_resid_kernel(a_ref, w_ref, x_ref, mod_ref, o_ref, *, k):
    o_ref[...] = x_ref[...] + mod_ref[k:k + 1, :] * _dot(a_ref[...], w_ref[...])


def matmul_residual(cfg, a, w, x, mod_l, *, k, rows):
    tm = _pick(cfg.nc, (1024, 512, 256))
    tn = _pick(cfg.d, (512, 256))
    row_of = _mod_row_map(cfg, tm)
    return pl.pallas_call(
        functools.partial(_matmul_resid_kernel, k=k),
        grid=(rows // tm, cfg.d // tn),
        in_specs=[
            pl.BlockSpec((tm, a.shape[1]), lambda i, j: (i, 0)),
            pl.BlockSpec((a.shape[1], tn), lambda i, j: (0, j)),
            pl.BlockSpec((tm, tn), lambda i, j: (i, j)),
            pl.BlockSpec((None, 6, tn), lambda i, j: (row_of(i), 0, j)),
        ],
        out_specs=pl.BlockSpec((tm, tn), lambda i, j: (i, j)),
        out_shape=jax.ShapeDtypeStruct((rows, cfg.d), F32),
        compiler_params=_params(("arbitrary", "arbitrary")),
        name="matmul_residual",
    )(a, w, x, mod_l)


def _top16_rows(s, top_ref):
    n = s.shape[0]
    row = lax.broadcasted_iota(jnp.int32, s.shape, 0).astype(F32)
    rank = jnp.full(s.shape, UNRANKED, F32)
    work = s
    for r in range(PEER_TOPK):
        m = jnp.max(work, axis=0, keepdims=True)
        first = jnp.min(jnp.where(work == m, row, float(n)), axis=0, keepdims=True)
        sel = row == first
        rank = jnp.where(sel, float(r), rank)
        work = jnp.where(sel, -jnp.inf, work)
        top_ref[r:r + 1, :] = m
    return rank


def _peer_tables_kernel(qh_ref, keys_ref, b_ref, e2_ref, c_ref, e1_ref, top1_ref, top2_ref, cnt_ref, *, tile):
    lanes = HEAD_DIM
    k1 = keys_ref[0]
    k2 = keys_ref[1]
    for part in range(tile // lanes):
        rows = slice(part * lanes, (part + 1) * lanes)
        cols = slice(part * lanes, (part + 1) * lanes)
        qh = qh_ref[rows, :].astype(BF16)
        s1 = _dot_nt(k1, qh[:, :lanes])
        s2 = _dot_nt(k2, qh[:, lanes:])
        rank1 = _top16_rows(s1, top1_ref)
        rank2 = _top16_rows(s2, top2_ref)
        t1 = top1_ref[...]
        t2 = top2_ref[...]
        pieces = [t1[0:1] + t2]
        pos = [lax.broadcasted_iota(jnp.int32, (PEER_TOPK, lanes), 0).astype(F32)]
        for a in range(1, 8):
            pieces.append(t1[a:a + 1] + t2[0:8])
            pos.append(lax.broadcasted_iota(jnp.int32, (8, lanes), 0).astype(F32) + float(a * PEER_TOPK))
        pieces.append(t1[8:16] + t2[0:1])
        pos.append((lax.broadcasted_iota(jnp.int32, (8, lanes), 0).astype(F32) + 8.0) * float(PEER_TOPK))
        cand = jnp.concatenate(pieces, axis=0)
        cpos = jnp.concatenate(pos, axis=0)
        a_row = lax.broadcasted_iota(jnp.int32, (PEER_TOPK, lanes), 0).astype(F32)
        cnt = jnp.zeros((PEER_TOPK, lanes), F32)
        z = jnp.zeros((1, lanes), F32)
        best0 = None
        for r in range(PEER_TOPK):
            m = jnp.max(cand, axis=0, keepdims=True)
            first = jnp.min(jnp.where(cand == m, cpos, 1e9), axis=0, keepdims=True)
            cand = jnp.where(cpos == first, -jnp.inf, cand)
            cnt = cnt + jnp.where(a_row == jnp.floor(first * (1.0 / PEER_TOPK)), 1.0, 0.0)
            if r == 0:
                best0 = m
            z = z + jnp.exp(m - best0)
        cnt_ref[...] = cnt
        c_tab = jnp.zeros((PEER_NKEYS, lanes), F32)
        for a in range(PEER_TOPK):
            c_tab = jnp.where(rank1 == float(a), cnt_ref[a:a + 1, :], c_tab)
        b_ref[:, cols] = rank2
        c_ref[:, cols] = c_tab
        e1_ref[:, cols] = jnp.exp(s1 - t1[0:1]) / z
        e2_ref[:, cols] = jnp.exp(s2 - t2[0:1])


def peer_tables(cfg, qh, keys, *, rows):
    tile = 256
    nh = cfg.peer_heads
    shp = jax.ShapeDtypeStruct((nh, PEER_NKEYS, rows), F32)
    out_spec = pl.BlockSpec((None, PEER_NKEYS, tile), lambda i, h: (h, 0, i))
    return pl.pallas_call(
        functools.partial(_peer_tables_kernel, tile=tile),
        grid=(rows // tile, nh),
        in_specs=[
            pl.BlockSpec((tile, 2 * HEAD_DIM), lambda i, h: (i, h)),
            pl.BlockSpec((2, None, PEER_NKEYS, HEAD_DIM), lambda i, h: (0, h, 0, 0)),
        ],
        out_specs=[out_spec] * 4,
        out_shape=[shp] * 4,
        scratch_shapes=[pltpu.VMEM((PEER_TOPK, HEAD_DIM), F32)] * 3,
        compiler_params=_params(("arbitrary", "arbitrary")),
        name="peer_tables",
    )(qh, keys)


def _peer_dense_kernel(ht_ref, u_ref, vt_ref, b_ref, e2_ref, c_ref, e1_ref, x_ref, mod_ref, o_ref,
                       acc_ref, w_ref, *, heads, eb):
    e = pl.program_id(1)

    @pl.when(e == 0)
    def _():
        acc_ref[...] = jnp.zeros_like(acc_ref)

    act = _dot(u_ref[...], ht_ref[...])
    gelu = 0.5 * act * (1.0 + lax.erf(act * float(math.sqrt(0.5))))
    per = eb // PEER_NKEYS
    for r in range(per):
        i = e * per + r
        g = jnp.zeros((PEER_NKEYS, act.shape[1]), F32)
        for h in range(heads):
            crow = c_ref[h, pl.ds(i, 1), :]
            erow = e1_ref[h, pl.ds(i, 1), :]
            g = g + jnp.where(b_ref[h] < crow, e2_ref[h] * erow, 0.0)
        rows = slice(r * PEER_NKEYS, (r + 1) * PEER_NKEYS)
        w_ref[rows, :] = (g * gelu[rows]).astype(BF16)
    acc_ref[...] += _dot(vt_ref[...], w_ref[...])

    @pl.when(e == pl.num_programs(1) - 1)
    def _():
        o_ref[...] = x_ref[...] + mod_ref[5:6, :] * acc_ref[...].T


def peer_dense(cfg, ht, u, vt, tabs, x, mod_l, *, rows):
    tile = _pick(cfg.nc, (512, 256))
    eb = 512
    n_exp = u.shape[0]
    nh = cfg.peer_heads
    row_of = _mod_row_map(cfg, tile)
    tab_spec = pl.BlockSpec((nh, PEER_NKEYS, tile), lambda i, e: (0, 0, i))
    return pl.pallas_call(
        functools.partial(_peer_dense_kernel, heads=nh, eb=eb),
        grid=(rows // tile, n_exp // eb),
        in_specs=[
            pl.BlockSpec((cfg.d, tile), lambda i, e: (0, i)),
            pl.BlockSpec((eb, cfg.d), lambda i, e: (e, 0)),
            pl.BlockSpec((cfg.d, eb), lambda i, e: (0, e)),
            tab_spec, tab_spec, tab_spec, tab_spec,
            pl.BlockSpec((tile, cfg.d), lambda i, e: (i, 0)),
            pl.BlockSpec((None, 6, cfg.d), lambda i, e: (row_of(i), 0, 0)),
        ],
        out_specs=pl.BlockSpec((tile, cfg.d), lambda i, e: (i, 0)),
        out_shape=jax.ShapeDtypeStruct((rows, cfg.d), F32),
        scratch_shapes=[pltpu.VMEM((cfg.d, tile), F32), pltpu.VMEM((eb, tile), BF16)],
        compiler_params=_params(("arbitrary", "arbitrary")),
        name="peer_dense",
    )(ht, u, vt, *tabs, x, mod_l)


def _final_norm_kernel(x_ref, w_ref, o_ref):
    x = x_ref[...]
    o_ref[...] = x * lax.rsqrt(jnp.mean(x * x, axis=-1, keepdims=True) + RMS_EPS) * w_ref[...]


def final_norm(cfg, x, w):
    tm = _pick(cfg.nl, (1024, 512, 256))
    return pl.pallas_call(
        _final_norm_kernel,
        grid=(cfg.nl // tm,),
        in_specs=[pl.BlockSpec((tm, cfg.d), lambda i: (i, 0)), pl.BlockSpec((1, cfg.d), lambda i: (0, 0))],
        out_specs=pl.BlockSpec((tm, cfg.d), lambda i: (i, 0)),
        out_shape=jax.ShapeDtypeStruct((cfg.nl, cfg.d), F32),
        compiler_params=_params(("arbitrary",)),
        name="final_norm",
    )(x, w.reshape(1, cfg.d))


def _rot_cols(w):
    q = ROPE_DIM // 4
    return jnp.concatenate([-w[..., q:2 * q], w[..., 0:q], -w[..., 3 * q:4 * q], w[..., 2 * q:3 * q]], axis=-1)


def _in_proj_weight(cfg, w_in, n_cols):
    d = cfg.d
    hg_end = 5 * cfg.hgw
    pool_end = hg_end + POOL_WIDTH
    cq_end = pool_end + cfg.q_lora
    ckv_end = cq_end + cfg.kv_lora
    rope_end = ckv_end + ROPE_DIM
    k_rope = w_in[:, ckv_end:rope_end]
    parts = [w_in[:, rope_end:rope_end + 3 * d], w_in[:, :ckv_end], k_rope, _rot_cols(k_rope)]
    w = jnp.concatenate(parts, axis=1)
    return jnp.pad(w, ((0, 0), (0, n_cols - w.shape[1]))).astype(BF16)


def _mla_q_weight(cfg, w_uq):
    w = w_uq.reshape(cfg.q_lora, cfg.mla_heads, HEAD_DIM + ROPE_DIM)
    rope = w[..., HEAD_DIM:]
    w = jnp.concatenate([w[..., :HEAD_DIM], rope, _rot_cols(rope)], axis=-1)
    return jnp.transpose(w, (1, 0, 2)).astype(BF16)


def _mla_kv_weight(cfg, w_ukv):
    w = w_ukv.reshape(cfg.kv_lora, cfg.mla_heads, 2 * HEAD_DIM)
    return jnp.transpose(w, (1, 0, 2)).astype(BF16)


def _rope_table(cfg):
    rows = cfg.seq // cfg.grid_w
    r, col = jnp.meshgrid(jnp.arange(rows), jnp.arange(cfg.grid_w), indexing="ij")
    n_freq = ROPE_DIM // 4
    freqs = ROPE_THETA ** (-jnp.arange(n_freq, dtype=F32) / n_freq)
    ang_r = r.reshape(-1)[:, None] * freqs
    ang_c = col.reshape(-1)[:, None] * freqs
    cos = jnp.concatenate([jnp.cos(ang_r)] * 2 + [jnp.cos(ang_c)] * 2, axis=1)
    sin = jnp.concatenate([jnp.sin(ang_r)] * 2 + [jnp.sin(ang_c)] * 2, axis=1)
    lat = jnp.tile(jnp.concatenate([cos, sin], axis=1).astype(F32), (cfg.batch, 1))
    ctx = jnp.concatenate([jnp.ones((cfg.nc, ROPE_DIM), F32), jnp.zeros((cfg.nc, ROPE_DIM), F32)], axis=1)
    return jnp.concatenate([lat, ctx], axis=0)


def _forward(cfg, x, c, ctx, c_ctx, w_mod, b_mod, norm_mix, norm_ffn, w_in, hg_lb_logits, hg_norm,
             pool_w, pool_scale, mla_q_norm, mla_w_uq, mla_kv_norm, mla_w_ukv,
             w_branch_a, w_branch_b, w_branch_c, w_out, peer_wq, peer_keys, peer_u, peer_v, final_w):
    d = cfg.d
    assert cfg.seq % HG_CHUNK == 0 and cfg.ctx % HG_CHUNK == 0 and cfg.nl % cfg.ctx == 0
    assert cfg.off_pool % POOL_WIDTH == 0 and cfg.off_cq % cfg.q_lora == 0
    assert cfg.off_ckv % cfg.kv_lora == 0 and cfg.batch < MOD_ROWS
    tn_in = 768 if d % 256 == 0 and cfg.in_cols > 8192 else 256
    n_cols = -(-cfg.in_cols // tn_in) * tn_in
    tm = _pick(cfg.nc, (1024, 512, 256))

    xs = jnp.concatenate([x.reshape(cfg.nl, d), ctx.reshape(cfg.nc, d)], axis=0)
    c_all = jnp.concatenate([c, c_ctx[None], jnp.zeros((MOD_ROWS - cfg.batch - 1, d), F32)], axis=0)
    mod = adaln_tables(cfg, c_all, w_mod, b_mod)
    cs = _rope_table(cfg)

    for l in range(cfg.depth):
        last = l == cfg.depth - 1
        rows = cfg.nl if last else cfg.nt
        mod_l = mod[l]
        p = norm_matmul(cfg, xs, norm_mix[l], mod_l, _in_proj_weight(cfg, w_in[l], n_cols),
                        k0=0, rows=cfg.nt, tm=tm, tn=tn_in, emit_h=False)
        hg_lat, hg_ctx = hgrn_mixer(cfg, p, hg_lb_logits, hg_norm[l], l)
        pw = pool_w[l].astype(BF16)
        pool_lat = pool_mixer(cfg, p, pw, pool_scale[l], n_seq=cfg.batch, seq_len=cfg.seq, row0=0)
        q = mla_queries(cfg, p, mla_q_norm[l], _mla_q_weight(cfg, mla_w_uq[l]), cs)
        k, v = mla_keys_values(cfg, p, mla_kv_norm[l], _mla_kv_weight(cfg, mla_w_ukv[l]), cs)
        att_lat, att_ctx = mla_attention(cfg, q, k, v, with_ctx=not last)
        if last:
            ya, yb, yc = hg_lat, pool_lat, att_lat
        else:
            pool_ctx = pool_mixer(cfg, p, pw, pool_scale[l], n_seq=cfg.batch, seq_len=cfg.ctx, row0=cfg.nl)
            ya = jnp.concatenate([hg_lat, hg_ctx], axis=0)
            yb = jnp.concatenate([pool_lat, pool_ctx], axis=0)
            yc = jnp.concatenate([att_lat, att_ctx], axis=0)
        m = merge_branches(cfg, ya, yb, yc, p, w_branch_a[l].astype(BF16), w_branch_b[l].astype(BF16),
                           w_branch_c[l].astype(BF16), rows=rows)
        xs = matmul_residual(cfg, m, w_out[l].astype(BF16), xs, mod_l, k=2, rows=rows)
        qh, h2 = norm_matmul(cfg, xs, norm_ffn[l], mod_l, peer_wq[l].astype(BF16),
                             k0=3, rows=rows, tm=tm, tn=_pick(peer_wq.shape[2], (512, 256)), emit_h=True)
        tabs = peer_tables(cfg, qh, peer_keys[l].astype(BF16), rows=rows)
        xs = peer_dense(cfg, h2.T, peer_u[l].astype(BF16), peer_v[l].T.astype(BF16), tabs, xs, mod_l, rows=rows)

    return final_norm(cfg, xs, final_w).reshape(cfg.batch, cfg.seq, d)


def kernel(x, c, ctx, c_ctx, w_mod, b_mod, norm_mix, norm_ffn, w_in, hg_lb_logits, hg_norm, pool_w, pool_scale,
           mla_q_norm, mla_w_uq, mla_kv_norm, mla_w_ukv, w_branch_a, w_branch_b, w_branch_c, w_out,
           peer_wq, peer_keys, peer_u, peer_v, final_norm):
    batch, seq, d = x.shape
    cfg = Cfg(d=d, batch=batch, seq=seq, ctx=ctx.shape[1], grid_w=64, depth=w_mod.shape[0],
              hg_heads=hg_lb_logits.shape[2] // HEAD_DIM,
              mla_heads=mla_w_ukv.shape[2] // (2 * HEAD_DIM), q_lora=mla_q_norm.shape[1],
              kv_lora=mla_kv_norm.shape[1], peer_heads=peer_keys.shape[2])
    return _forward(cfg, x, c, ctx, c_ctx, w_mod, b_mod, norm_mix, norm_ffn, w_in, hg_lb_logits, hg_norm,
                    pool_w, pool_scale, mla_q_norm, mla_w_uq, mla_kv_norm, mla_w_ukv,
                    w_branch_a, w_branch_b, w_branch_c, w_out, peer_wq, peer_keys, peer_u, peer_v, final_norm)
```

```python
import functools
import math
from typing import NamedTuple

import numpy as np
import jax
import jax.numpy as jnp
from jax import lax
from jax.experimental import pallas as pl
from jax.experimental.pallas import tpu as pltpu

F32 = jnp.float32
BF16 = jnp.bfloat16

RMS_EPS = 1e-6
ROPE_THETA = 10000.0
HEAD_DIM = 128
ROPE_DIM = 64
QK_DIM = 256
QK_LOG2_SCALE = float((HEAD_DIM + ROPE_DIM) ** -0.5 * math.log2(math.e))
POOL_WINDOWS = (2, 4, 8, 16)
POOL_GROUP = 256
POOL_WIDTH = POOL_GROUP * len(POOL_WINDOWS)
POOL_HALO = 8
PEER_NKEYS = 128
PEER_TOPK = 16
HG_CHUNK = 128
HG_LEVELS = 7
UNRANKED = 99.0
MOD_ROWS = 8
VMEM_LIMIT = 56 * 1024 * 1024


class Cfg(NamedTuple):
    d: int
    batch: int
    seq: int
    ctx: int
    grid_w: int
    depth: int
    hg_heads: int
    mla_heads: int
    q_lora: int
    kv_lora: int
    peer_heads: int

    @property
    def nl(self):
        return self.batch * self.seq

    @property
    def nc(self):
        return self.batch * self.ctx

    @property
    def nt(self):
        return self.nl + self.nc

    @property
    def hgw(self):
        return self.hg_heads * HEAD_DIM

    @property
    def off_gate(self):
        return 0

    @property
    def off_hg(self):
        return 3 * self.d

    @property
    def off_pool(self):
        return self.off_hg + 5 * self.hgw

    @property
    def off_cq(self):
        return self.off_pool + POOL_WIDTH

    @property
    def off_ckv(self):
        return self.off_cq + self.q_lora

    @property
    def off_slab(self):
        return self.off_ckv + self.kv_lora

    @property
    def in_cols(self):
        return self.off_slab + HEAD_DIM


def _params(sem):
    return pltpu.CompilerParams(dimension_semantics=sem, vmem_limit_bytes=VMEM_LIMIT)


def _dot(a, b):
    return jnp.dot(a, b, preferred_element_type=F32)


def _dot_nt(a, b):
    return lax.dot_general(a, b, (((1,), (1,)), ((), ())), preferred_element_type=F32)


def _sigmoid(x):
    return 1.0 / (1.0 + jnp.exp(-x))


def _pick(n, prefs):
    for p in prefs:
        if n % p == 0:
            return p
    raise ValueError(f"no tile for {n} in {prefs}")


def _mod_row_map(cfg, tm):
    n_lat = cfg.nl // tm
    per_batch = cfg.seq // tm
    return lambda i: jnp.where(i < n_lat, i // per_batch, cfg.batch)


def _mod_kernel(c_ref, w_ref, b_ref, o_ref):
    c = c_ref[...]
    s = (c * _sigmoid(c)).astype(BF16)
    o_ref[...] = _dot(s, w_ref[...].astype(BF16)) + b_ref[...]


def adaln_tables(cfg, c_all, w_mod, b_mod):
    d6 = 6 * cfg.d
    tn = _pick(d6, (1024, 768, 512, 256))
    out = pl.pallas_call(
        _mod_kernel,
        grid=(cfg.depth, d6 // tn),
        in_specs=[
            pl.BlockSpec((MOD_ROWS, cfg.d), lambda l, j: (0, 0)),
            pl.BlockSpec((None, cfg.d, tn), lambda l, j: (l, 0, j)),
            pl.BlockSpec((None, 1, tn), lambda l, j: (l, 0, j)),
        ],
        out_specs=pl.BlockSpec((None, MOD_ROWS, tn), lambda l, j: (l, 0, j)),
        out_shape=jax.ShapeDtypeStruct((cfg.depth, MOD_ROWS, d6), F32),
        compiler_params=_params(("arbitrary", "arbitrary")),
        name="adaln_tables",
    )(c_all, w_mod, b_mod.reshape(cfg.depth, 1, d6))
    return out.reshape(cfg.depth, MOD_ROWS, 6, cfg.d)


def _norm_matmul_kernel(x_ref, gain_ref, mod_ref, w_ref, *rest, k0, emit_h):
    if emit_h:
        o_ref, hout_ref, h_ref = rest
    else:
        o_ref, h_ref = rest

    @pl.when(pl.program_id(1) == 0)
    def _():
        x = x_ref[...]
        y = x * lax.rsqrt(jnp.mean(x * x, axis=-1, keepdims=True) + RMS_EPS) * gain_ref[...]
        h = (y * (1.0 + mod_ref[k0 + 1:k0 + 2, :]) + mod_ref[k0:k0 + 1, :]).astype(BF16)
        h_ref[...] = h
        if emit_h:
            hout_ref[...] = h

    o_ref[...] = _dot(h_ref[...], w_ref[...])


def norm_matmul(cfg, x, gain, mod_l, w, *, k0, rows, tm, tn, emit_h):
    n = w.shape[1]
    row_of = _mod_row_map(cfg, tm)
    out_shape = [jax.ShapeDtypeStruct((rows, n), F32)]
    out_specs = [pl.BlockSpec((tm, tn), lambda i, j: (i, j))]
    if emit_h:
        out_shape.append(jax.ShapeDtypeStruct((rows, cfg.d), BF16))
        out_specs.append(pl.BlockSpec((tm, cfg.d), lambda i, j: (i, 0)))
    res = pl.pallas_call(
        functools.partial(_norm_matmul_kernel, k0=k0, emit_h=emit_h),
        grid=(rows // tm, n // tn),
        in_specs=[
            pl.BlockSpec((tm, cfg.d), lambda i, j: (i, 0)),
            pl.BlockSpec((1, cfg.d), lambda i, j: (0, 0)),
            pl.BlockSpec((None, 6, cfg.d), lambda i, j: (row_of(i), 0, 0)),
            pl.BlockSpec((cfg.d, tn), lambda i, j: (0, j)),
        ],
        out_specs=out_specs,
        out_shape=out_shape,
        scratch_shapes=[pltpu.VMEM((tm, cfg.d), BF16)],
        compiler_params=_params(("arbitrary", "arbitrary")),
        name="norm_matmul_h" if emit_h else "norm_matmul",
    )(x, gain.reshape(1, cfg.d), mod_l, w)
    return res if emit_h else res[0]


def _hgrn_consts():
    c = HG_CHUNK
    out = []
    for rev in (False, True):
        p = np.arange(c) if not rev else c - 1 - np.arange(c)
        pt, pu = p[:, None], p[None, :]
        g = np.zeros((HG_LEVELS + 1, c, c), np.float32)
        up = np.zeros((HG_LEVELS, c, c), np.float32)
        g[0] = pu <= pt
        for l in range(HG_LEVELS):
            m = 1 << l
            blk = p >> (l + 1)
            upper = ((p >> l) & 1) == 1
            mid = (blk * 2 * m + m)[:, None]
            same = blk[:, None] == blk[None, :]
            g_up = same & (pu >= mid) & (pu <= pt)
            g_lo = same & (pu > pt) & (pu < mid)
            g[1 + l] = np.where(upper[:, None], g_up, g_lo)
            up[l] = np.broadcast_to(upper[:, None], (c, c))
        x = pt ^ pu
        lv = np.where(pu < pt, np.floor(np.log2(np.maximum(x, 1))), -1.0).astype(np.float32)
        out.append((jnp.asarray(g.reshape(-1, c), BF16), jnp.asarray(up, F32), jnp.asarray(lv, F32)))
    return out


def _hgrn_chunk(q_raw, v, f_raw, log_lb, log_1mlb, one_m_lb, g_ref, up_ref, lv_ref, st_ref, end_row):
    c = HG_CHUNK
    q = q_raw * _sigmoid(q_raw)
    log_sig = jnp.minimum(f_raw, 0.0) - jnp.log1p(jnp.exp(-jnp.abs(f_raw)))
    t = log_1mlb + log_sig
    log_f = jnp.maximum(log_lb, t) + jnp.log1p(jnp.exp(-jnp.abs(log_lb - t)))
    k = one_m_lb / (1.0 + jnp.exp(f_raw))
    hi = log_f.astype(BF16)
    lo = (log_f - hi.astype(F32)).astype(BF16)
    a2 = _dot(g_ref[...], jnp.concatenate([hi, lo], axis=1))
    a = a2[:, :HEAD_DIM] + a2[:, HEAD_DIM:]
    b = a[0:c]
    lv = lv_ref[...]
    scores = jnp.zeros((c, c), F32)
    for l in range(HG_LEVELS):
        e = jnp.exp(a[(1 + l) * c:(2 + l) * c])
        x = (jnp.where(up_ref[l] > 0.5, q, k) * e).astype(BF16)
        scores = jnp.where(lv == float(l), _dot_nt(x, x), scores)
    b_end = b[end_row:end_row + 1, :]
    qb = (q * jnp.exp(b)).astype(BF16)
    kd = (k * jnp.exp(b_end - b)).astype(BF16)
    st = st_ref[...]
    vb = v.astype(BF16)
    o = (_dot(scores.astype(BF16), vb) + _dot_nt(qb, st.astype(BF16))
         + jnp.sum(q * k, axis=-1, keepdims=True) * v)
    st_ref[...] = st * jnp.exp(b_end) + _dot(v.T.astype(BF16), kd)
    return o


def _hgrn_kernel(ql, ffl, fbl, il, gl, qc, ffc, fbc, ic, gc, logit_ref, nw_ref,
                 gf_ref, upf_ref, lvf_ref, gb_ref, upb_ref, lvb_ref,
                 yl_ref, yc_ref, ofl, obl, ofc, obc, stf, stb, *, layer, seq, ctx):
    c = HG_CHUNK
    depth = logit_ref.shape[0]
    lg = [logit_ref[dd] for dd in range(depth)]
    mx = functools.reduce(jnp.maximum, lg)
    ex = [jnp.exp(v - mx) for v in lg]
    tot = functools.reduce(jnp.add, ex)
    cum = [ex[0] / tot]
    for dd in range(1, layer + 1):
        cum.append(cum[-1] + ex[dd] / tot)
    lb = cum[layer] - cum[0]
    log_lb = jnp.log(lb)
    log_1mlb = jnp.log1p(-lb)
    one_m_lb = 1.0 - lb

    stf[...] = jnp.zeros_like(stf)
    stb[...] = jnp.zeros_like(stb)

    def segment(q_ref, ff_ref, fb_ref, i_ref, of_ref, ob_ref, n):
        def body(j, carry):
            rf = pl.multiple_of(j * c, c)
            rb = pl.multiple_of((n - 1 - j) * c, c)
            of_ref[pl.ds(rf, c), :] = _hgrn_chunk(
                q_ref[pl.ds(rf, c), :], i_ref[pl.ds(rf, c), :], ff_ref[pl.ds(rf, c), :],
                log_lb[0:1], log_1mlb[0:1], one_m_lb[0:1], gf_ref, upf_ref, lvf_ref, stf, c - 1)
            ob_ref[pl.ds(rb, c), :] = _hgrn_chunk(
                q_ref[pl.ds(rb, c), :], i_ref[pl.ds(rb, c), :], fb_ref[pl.ds(rb, c), :],
                log_lb[1:2], log_1mlb[1:2], one_m_lb[1:2], gb_ref, upb_ref, lvb_ref, stb, 0)
            return carry
        lax.fori_loop(0, n, body, 0)

    segment(qc, ffc, fbc, ic, ofc, obc, ctx // c)
    segment(ql, ffl, fbl, il, ofl, obl, seq // c)

    nw = nw_ref[...]

    def readout(of_ref, ob_ref, g_ref, y_ref, n):
        def body(j, carry):
            r = pl.multiple_of(j * c, c)
            o = of_ref[pl.ds(r, c), :] + ob_ref[pl.ds(r, c), :]
            y = o * lax.rsqrt(jnp.mean(o * o, axis=-1, keepdims=True) + RMS_EPS) * nw
            g = g_ref[pl.ds(r, c), :]
            y_ref[pl.ds(r, c), :] = (y * (g * _sigmoid(g))).astype(y_ref.dtype)
            return carry
        lax.fori_loop(0, n, body, 0)

    readout(ofc, obc, gc, yc_ref, ctx // c)
    readout(ofl, obl, gl, yl_ref, seq // c)


def hgrn_mixer(cfg, p, logits, norm_w, layer):
    hd = HEAD_DIM
    nh = cfg.hg_heads
    col0 = cfg.off_hg // hd
    ctx_blk0 = cfg.nl // cfg.ctx
    (gf, upf, lvf), (gb, upb, lvb) = _hgrn_consts()

    def lat_spec(part):
        return pl.BlockSpec((cfg.seq, hd), lambda b, h, part=part: (b, col0 + part * nh + h))

    def ctx_spec(part):
        return pl.BlockSpec((cfg.ctx, hd), lambda b, h, part=part: (ctx_blk0 + b, col0 + part * nh + h))

    parts = (0, 1, 2, 3, 4)

    def const(arr):
        return pl.BlockSpec(arr.shape, lambda b, h, nd=arr.ndim: (0,) * nd)

    y_lat, y_ctx = pl.pallas_call(
        functools.partial(_hgrn_kernel, layer=layer, seq=cfg.seq, ctx=cfg.ctx),
        grid=(cfg.batch, nh),
        in_specs=[lat_spec(k) for k in parts] + [ctx_spec(k) for k in parts] + [
            pl.BlockSpec((cfg.depth, 2, hd), lambda b, h: (0, 0, h)),
            pl.BlockSpec((1, hd), lambda b, h: (0, 0)),
            const(gf), const(upf), const(lvf), const(gb), const(upb), const(lvb),
        ],
        out_specs=[
            pl.BlockSpec((cfg.seq, hd), lambda b, h: (b, h)),
            pl.BlockSpec((cfg.ctx, hd), lambda b, h: (b, h)),
        ],
        out_shape=[
            jax.ShapeDtypeStruct((cfg.nl, cfg.hgw), BF16),
            jax.ShapeDtypeStruct((cfg.nc, cfg.hgw), BF16),
        ],
        scratch_shapes=[
            pltpu.VMEM((cfg.seq, hd), F32), pltpu.VMEM((cfg.seq, hd), F32),
            pltpu.VMEM((cfg.ctx, hd), F32), pltpu.VMEM((cfg.ctx, hd), F32),
            pltpu.VMEM((hd, hd), F32), pltpu.VMEM((hd, hd), F32),
        ],
        compiler_params=_params(("arbitrary", "arbitrary")),
        name="hgrn_mixer",
    )(*([p] * 10), logits, norm_w.reshape(1, hd), gf, upf, lvf, gb, upb, lvb)
    return y_lat, y_ctx


def _pool_kernel(prev_ref, cur_ref, next_ref, w_ref, scale_ref, o_ref, buf_ref, *, seq_len, tile, n_tiles):
    i = pl.program_id(1)
    h = POOL_HALO
    cur = cur_ref[...]
    buf_ref[0:h, :] = jnp.where(i > 0, prev_ref[...], 0.0)
    buf_ref[h:h + tile, :] = cur
    buf_ref[h + tile:2 * h + tile, :] = jnp.where(i < n_tiles - 1, next_ref[...], 0.0)
    pos = i * tile + lax.broadcasted_iota(jnp.int32, (tile, POOL_GROUP), 0)
    for gi, win in enumerate(POOL_WINDOWS):
        half = win // 2
        cols = slice(gi * POOL_GROUP, (gi + 1) * POOL_GROUP)
        acc = buf_ref[h - half:h - half + tile, cols]
        for dlt in range(-half + 1, half):
            acc = acc + buf_ref[h + dlt:h + dlt + tile, cols]
        cnt = (jnp.minimum(pos + half, seq_len) - jnp.maximum(pos - half, 0)).astype(F32)
        pooled = acc / cnt - cur[:, cols]
        mixed = _dot(pooled.astype(BF16), w_ref[gi])
        o_ref[:, cols] = (mixed * scale_ref[:, cols]).astype(o_ref.dtype)


def pool_mixer(cfg, p, w_pool, scale, *, n_seq, seq_len, row0):
    tile = _pick(seq_len, (512, 256, 128))
    n_tiles = seq_len // tile
    blk0 = row0 // tile
    col = cfg.off_pool // POOL_WIDTH
    per8 = tile // POOL_HALO
    last8 = cfg.nt // POOL_HALO - 1

    def cur_map(s, i):
        return (blk0 + s * n_tiles + i, col)

    def prev_map(s, i):
        return (jnp.maximum((blk0 + s * n_tiles + i) * per8 - 1, 0), col)

    def next_map(s, i):
        return (jnp.minimum((blk0 + s * n_tiles + i + 1) * per8, last8), col)

    return pl.pallas_call(
        functools.partial(_pool_kernel, seq_len=seq_len, tile=tile, n_tiles=n_tiles),
        grid=(n_seq, n_tiles),
        in_specs=[
            pl.BlockSpec((POOL_HALO, POOL_WIDTH), prev_map),
            pl.BlockSpec((tile, POOL_WIDTH), cur_map),
            pl.BlockSpec((POOL_HALO, POOL_WIDTH), next_map),
            pl.BlockSpec((len(POOL_WINDOWS), POOL_GROUP, POOL_GROUP), lambda s, i: (0, 0, 0)),
            pl.BlockSpec((1, POOL_WIDTH), lambda s, i: (0, 0)),
        ],
        out_specs=pl.BlockSpec((tile, POOL_WIDTH), lambda s, i: (s * n_tiles + i, 0)),
        out_shape=jax.ShapeDtypeStruct((n_seq * seq_len, POOL_WIDTH), BF16),
        scratch_shapes=[pltpu.VMEM((tile + 2 * POOL_HALO, POOL_WIDTH), F32)],
        compiler_params=_params(("arbitrary", "arbitrary")),
        name="pool_mixer",
    )(p, p, p, w_pool, scale.reshape(1, POOL_WIDTH))


def _rope_slab(slab, cs):
    t = slab * cs
    r = t + pltpu.roll(t, ROPE_DIM, axis=1)
    lane = lax.broadcasted_iota(jnp.int32, r.shape, 1)
    return jnp.where(lane < ROPE_DIM, r, 0.0)


def _rms_bf16(x, w):
    return (x * lax.rsqrt(jnp.mean(x * x, axis=-1, keepdims=True) + RMS_EPS) * w).astype(BF16)


def _mla_q_kernel(cq_ref, nw_ref, cs_ref, w_ref, o_ref, n_ref):
    @pl.when(pl.program_id(1) == 0)
    def _():
        n_ref[...] = _rms_bf16(cq_ref[...], nw_ref[...])

    y = _dot(n_ref[...], w_ref[...])
    rope = _rope_slab(y[:, HEAD_DIM:], cs_ref[...])
    o_ref[...] = (jnp.concatenate([y[:, :HEAD_DIM], rope], axis=1) * QK_LOG2_SCALE).astype(o_ref.dtype)


def mla_queries(cfg, p, norm_w, w_q, cs):
    tm = _pick(cfg.nc, (1024, 512, 256))
    nh = cfg.mla_heads
    col = cfg.off_cq // cfg.q_lora
    return pl.pallas_call(
        _mla_q_kernel,
        grid=(cfg.nt // tm, nh),
        in_specs=[
            pl.BlockSpec((tm, cfg.q_lora), lambda i, h: (i, col)),
            pl.BlockSpec((1, cfg.q_lora), lambda i, h: (0, 0)),
            pl.BlockSpec((tm, HEAD_DIM), lambda i, h: (i, 0)),
            pl.BlockSpec((None, cfg.q_lora, QK_DIM), lambda i, h: (h, 0, 0)),
        ],
        out_specs=pl.BlockSpec((None, tm, QK_DIM), lambda i, h: (h, i, 0)),
        out_shape=jax.ShapeDtypeStruct((nh, cfg.nt, QK_DIM), BF16),
        scratch_shapes=[pltpu.VMEM((tm, cfg.q_lora), BF16)],
        compiler_params=_params(("arbitrary", "arbitrary")),
        name="mla_queries",
    )(p, norm_w.reshape(1, cfg.q_lora), cs, w_q)


def _mla_kv_kernel(ckv_ref, slab_ref, nw_ref, cs_ref, w_ref, k_ref, v_ref, n_ref, r_ref):
    @pl.when(pl.program_id(1) == 0)
    def _():
        n_ref[...] = _rms_bf16(ckv_ref[...], nw_ref[...])
        r_ref[...] = _rope_slab(slab_ref[...], cs_ref[...])

    y = _dot(n_ref[...], w_ref[...])
    k_ref[...] = jnp.concatenate([y[:, :HEAD_DIM], r_ref[...]], axis=1).astype(k_ref.dtype)
    lane = lax.broadcasted_iota(jnp.int32, (y.shape[0], HEAD_DIM), 1)
    ones_col = jnp.where(lane == 0, 1.0, 0.0)
    v_ref[...] = jnp.concatenate([y[:, HEAD_DIM:], ones_col], axis=1).astype(v_ref.dtype)


def mla_keys_values(cfg, p, norm_w, w_kv, cs):
    tm = _pick(cfg.nc, (1024, 512, 256))
    nh = cfg.mla_heads
    col = cfg.off_ckv // cfg.kv_lora
    slab_col = cfg.off_slab // HEAD_DIM
    return pl.pallas_call(
        _mla_kv_kernel,
        grid=(cfg.nt // tm, nh),
        in_specs=[
            pl.BlockSpec((tm, cfg.kv_lora), lambda i, h: (i, col)),
            pl.BlockSpec((tm, HEAD_DIM), lambda i, h: (i, slab_col)),
            pl.BlockSpec((1, cfg.kv_lora), lambda i, h: (0, 0)),
            pl.BlockSpec((tm, HEAD_DIM), lambda i, h: (i, 0)),
            pl.BlockSpec((None, cfg.kv_lora, 2 * HEAD_DIM), lambda i, h: (h, 0, 0)),
        ],
        out_specs=[
            pl.BlockSpec((None, tm, QK_DIM), lambda i, h: (h, i, 0)),
            pl.BlockSpec((None, tm, 2 * HEAD_DIM), lambda i, h: (h, i, 0)),
        ],
        out_shape=[
            jax.ShapeDtypeStruct((nh, cfg.nt, QK_DIM), BF16),
            jax.ShapeDtypeStruct((nh, cfg.nt, 2 * HEAD_DIM), BF16),
        ],
        scratch_shapes=[pltpu.VMEM((tm, cfg.kv_lora), BF16), pltpu.VMEM((tm, HEAD_DIM), F32)],
        compiler_params=_params(("arbitrary", "arbitrary")),
        name="mla_keys_values",
    )(p, p, norm_w.reshape(1, cfg.kv_lora), cs, w_kv)


ATTN_SUB = 256
ATTN_KT = 512


def _softmax_pv(s_list, v_list):
    m = functools.reduce(jnp.maximum, [jnp.max(s, axis=-1, keepdims=True) for s in s_list])
    o = None
    for s, v in zip(s_list, v_list):
        part = _dot(jnp.exp2(s - m).astype(BF16), v)
        o = part if o is None else o + part
    return o[:, :HEAD_DIM] / o[:, HEAD_DIM:HEAD_DIM + 1]


def _attn_lat_kernel(q_ref, kl_ref, kc_ref, vl_ref, vc_ref, o_ref, s_ref):
    n_keys = kl_ref.shape[0]
    for r in range(0, q_ref.shape[0], ATTN_SUB):
        rows = slice(r, r + ATTN_SUB)
        q = q_ref[rows, :]
        s_ref[rows, :] = _dot_nt(q, kl_ref[...])
        sc = _dot_nt(q, kc_ref[...])
        m = jnp.maximum(jnp.max(s_ref[rows, :], axis=-1, keepdims=True), jnp.max(sc, axis=-1, keepdims=True))
        o = _dot(jnp.exp2(sc - m).astype(BF16), vc_ref[...])
        for c in range(0, n_keys, ATTN_KT):
            p = jnp.exp2(s_ref[rows, c:c + ATTN_KT] - m).astype(BF16)
            o = o + _dot(p, vl_ref[c:c + ATTN_KT, :])
        o_ref[rows, :] = (o[:, :HEAD_DIM] / o[:, HEAD_DIM:HEAD_DIM + 1]).astype(o_ref.dtype)


def _attn_ctx_kernel(q_ref, kc_ref, vc_ref, o_ref):
    o_ref[...] = _softmax_pv([_dot_nt(q_ref[...], kc_ref[...])], [vc_ref[...]]).astype(o_ref.dtype)


def mla_attention(cfg, q, k, v, *, with_ctx):
    nh = cfg.mla_heads
    vw = 2 * HEAD_DIM
    tq = _pick(cfg.seq, (1024, 512, 256))
    nq = cfg.seq // tq
    cb0 = cfg.nl // cfg.ctx
    y_lat = pl.pallas_call(
        _attn_lat_kernel,
        grid=(cfg.batch, nh, nq),
        in_specs=[
            pl.BlockSpec((None, tq, QK_DIM), lambda b, h, i: (h, b * nq + i, 0)),
            pl.BlockSpec((None, cfg.seq, QK_DIM), lambda b, h, i: (h, b, 0)),
            pl.BlockSpec((None, cfg.ctx, QK_DIM), lambda b, h, i: (h, cb0 + b, 0)),
            pl.BlockSpec((None, cfg.seq, vw), lambda b, h, i: (h, b, 0)),
            pl.BlockSpec((None, cfg.ctx, vw), lambda b, h, i: (h, cb0 + b, 0)),
        ],
        out_specs=pl.BlockSpec((tq, HEAD_DIM), lambda b, h, i: (b * nq + i, h)),
        out_shape=jax.ShapeDtypeStruct((cfg.nl, nh * HEAD_DIM), BF16),
        scratch_shapes=[pltpu.VMEM((tq, cfg.seq), F32)],
        compiler_params=_params(("arbitrary", "arbitrary", "arbitrary")),
        name="mla_attention",
    )(q, k, k, v, v)
    if not with_ctx:
        return y_lat, None
    y_ctx = pl.pallas_call(
        _attn_ctx_kernel,
        grid=(cfg.batch, nh),
        in_specs=[
            pl.BlockSpec((None, cfg.ctx, QK_DIM), lambda b, h: (h, cb0 + b, 0)),
            pl.BlockSpec((None, cfg.ctx, QK_DIM), lambda b, h: (h, cb0 + b, 0)),
            pl.BlockSpec((None, cfg.ctx, vw), lambda b, h: (h, cb0 + b, 0)),
        ],
        out_specs=pl.BlockSpec((cfg.ctx, HEAD_DIM), lambda b, h: (b, h)),
        out_shape=jax.ShapeDtypeStruct((cfg.nc, nh * HEAD_DIM), BF16),
        compiler_params=_params(("arbitrary", "arbitrary")),
        name="mla_attention_ctx",
    )(q, k, v)
    return y_lat, y_ctx


def _merge_kernel(ya_ref, yb_ref, yc_ref, ga_ref, gb_ref, gc_ref, wa_ref, wb_ref, wc_ref, o_ref):
    m = (_sigmoid(ga_ref[...]) * _dot(ya_ref[...], wa_ref[...])
         + _sigmoid(gb_ref[...]) * _dot(yb_ref[...], wb_ref[...])
         + _sigmoid(gc_ref[...]) * _dot(yc_ref[...], wc_ref[...]))
    o_ref[...] = m.astype(o_ref.dtype)


def merge_branches(cfg, ya, yb, yc, p, wa, wb, wc, *, rows):
    tm = _pick(cfg.nc, (512, 256))
    tn = _pick(cfg.d, (512, 256))
    gcols = cfg.d // tn

    def gate_spec(k):
        return pl.BlockSpec((tm, tn), lambda i, j, k=k: (i, k * gcols + j))

    return pl.pallas_call(
        _merge_kernel,
        grid=(rows // tm, cfg.d // tn),
        in_specs=[
            pl.BlockSpec((tm, ya.shape[1]), lambda i, j: (i, 0)),
            pl.BlockSpec((tm, yb.shape[1]), lambda i, j: (i, 0)),
            pl.BlockSpec((tm, yc.shape[1]), lambda i, j: (i, 0)),
            gate_spec(0), gate_spec(1), gate_spec(2),
            pl.BlockSpec((wa.shape[0], tn), lambda i, j: (0, j)),
            pl.BlockSpec((wb.shape[0], tn), lambda i, j: (0, j)),
            pl.BlockSpec((wc.shape[0], tn), lambda i, j: (0, j)),
        ],
        out_specs=pl.BlockSpec((tm, tn), lambda i, j: (i, j)),
        out_shape=jax.ShapeDtypeStruct((rows, cfg.d), BF16),
        compiler_params=_params(("arbitrary", "arbitrary")),
        name="merge_branches",
    )(ya, yb, yc, p, p, p, wa, wb, wc)


def _matmul_resid_kernel(a_ref, w_ref, x_ref, mod_ref, o_ref, *, k):
    o_ref[...] = x_ref[...] + mod_ref[k:k + 1, :] * _dot(a_ref[...], w_ref[...])


def matmul_residual(cfg, a, w, x, mod_l, *, k, rows):
    tm = _pick(cfg.nc, (1024, 512, 256))
    tn = _pick(cfg.d, (512, 256))
    row_of = _mod_row_map(cfg, tm)
    return pl.pallas_call(
        functools.partial(_matmul_resid_kernel, k=k),
        grid=(rows // tm, cfg.d // tn),
        in_specs=[
            pl.BlockSpec((tm, a.shape[1]), lambda i, j: (i, 0)),
            pl.BlockSpec((a.shape[1], tn), lambda i, j: (0, j)),
            pl.BlockSpec((tm, tn), lambda i, j: (i, j)),
            pl.BlockSpec((None, 6, tn), lambda i, j: (row_of(i), 0, j)),
        ],
        out_specs=pl.BlockSpec((tm, tn), lambda i, j: (i, j)),
        out_shape=jax.ShapeDtypeStruct((rows, cfg.d), F32),
        compiler_params=_params(("arbitrary", "arbitrary")),
        name="matmul_residual",
    )(a, w, x, mod_l)


def _top16_rows(s, top_ref):
    n = s.shape[0]
    row = lax.broadcasted_iota(jnp.int32, s.shape, 0).astype(F32)
    rank = jnp.full(s.shape, UNRANKED, F32)
    work = s
    for r in range(PEER_TOPK):
        m = jnp.max(work, axis=0, keepdims=True)
        first = jnp.min(jnp.where(work == m, row, float(n)), axis=0, keepdims=True)
        sel = row == first
        rank = jnp.where(sel, float(r), rank)
        work = jnp.where(sel, -jnp.inf, work)
        top_ref[r:r + 1, :] = m
    return rank


def _peer_tables_kernel(qh_ref, keys_ref, b_ref, e2_ref, c_ref, e1_ref, top1_ref, top2_ref, cnt_ref, *, tile):
    lanes = HEAD_DIM
    k1 = keys_ref[0]
    k2 = keys_ref[1]
    for part in range(tile // lanes):
        rows = slice(part * lanes, (part + 1) * lanes)
        cols = slice(part * lanes, (part + 1) * lanes)
        qh = qh_ref[rows, :].astype(BF16)
        s1 = _dot_nt(k1, qh[:, :lanes])
        s2 = _dot_nt(k2, qh[:, lanes:])
        rank1 = _top16_rows(s1, top1_ref)
        rank2 = _top16_rows(s2, top2_ref)
        t1 = top1_ref[...]
        t2 = top2_ref[...]
        pieces = [t1[0:1] + t2]
        pos = [lax.broadcasted_iota(jnp.int32, (PEER_TOPK, lanes), 0).astype(F32)]
        for a in range(1, 8):
            pieces.append(t1[a:a + 1] + t2[0:8])
            pos.append(lax.broadcasted_iota(jnp.int32, (8, lanes), 0).astype(F32) + float(a * PEER_TOPK))
        pieces.append(t1[8:16] + t2[0:1])
        pos.append((lax.broadcasted_iota(jnp.int32, (8, lanes), 0).astype(F32) + 8.0) * float(PEER_TOPK))
        cand = jnp.concatenate(pieces, axis=0)
        cpos = jnp.concatenate(pos, axis=0)
        a_row = lax.broadcasted_iota(jnp.int32, (PEER_TOPK, lanes), 0).astype(F32)
        cnt = jnp.zeros((PEER_TOPK, lanes), F32)
        z = jnp.zeros((1, lanes), F32)
        best0 = None
        for r in range(PEER_TOPK):
            m = jnp.max(cand, axis=0, keepdims=True)
            first = jnp.min(jnp.where(cand == m, cpos, 1e9), axis=0, keepdims=True)
            cand = jnp.where(cpos == first, -jnp.inf, cand)
            cnt = cnt + jnp.where(a_row == jnp.floor(first * (1.0 / PEER_TOPK)), 1.0, 0.0)
            if r == 0:
                best0 = m
            z = z + jnp.exp(m - best0)
        cnt_ref[...] = cnt
        c_tab = jnp.zeros((PEER_NKEYS, lanes), F32)
        for a in range(PEER_TOPK):
            c_tab = jnp.where(rank1 == float(a), cnt_ref[a:a + 1, :], c_tab)
        b_ref[:, cols] = rank2.astype(b_ref.dtype)
        c_ref[:, cols] = c_tab
        e1_ref[:, cols] = jnp.exp(s1 - t1[0:1]) / z
        e2_ref[:, cols] = jnp.exp(s2 - t2[0:1]).astype(e2_ref.dtype)


def peer_tables(cfg, qh, keys, *, rows):
    tile = 256
    nh = cfg.peer_heads
    shapes = [jax.ShapeDtypeStruct((nh, PEER_NKEYS, rows), dt) for dt in (BF16, F32, F32, F32)]
    out_spec = pl.BlockSpec((None, PEER_NKEYS, tile), lambda i, h: (h, 0, i))
    return pl.pallas_call(
        functools.partial(_peer_tables_kernel, tile=tile),
        grid=(rows // tile, nh),
        in_specs=[
            pl.BlockSpec((tile, 2 * HEAD_DIM), lambda i, h: (i, h)),
            pl.BlockSpec((2, None, PEER_NKEYS, HEAD_DIM), lambda i, h: (0, h, 0, 0)),
        ],
        out_specs=[out_spec] * 4,
        out_shape=shapes,
        scratch_shapes=[pltpu.VMEM((PEER_TOPK, HEAD_DIM), F32)] * 3,
        compiler_params=_params(("arbitrary", "arbitrary")),
        name="peer_tables",
    )(qh, keys)


PEER_EB = 512
PEER_ROWS = PEER_EB // PEER_NKEYS


def _peer_gate_block(a_ref, w_ref, b_ref, e2_ref, c_ref, e1_ref, row0, heads):
    for r in range(PEER_ROWS):
        rows = slice(r * PEER_NKEYS, (r + 1) * PEER_NKEYS)
        act = a_ref[rows, :]
        gelu = 0.5 * act * (1.0 + lax.erf(act * float(math.sqrt(0.5))))
        g = jnp.zeros(act.shape, F32)
        for h in range(heads):
            crow = c_ref[h, row0 + r:row0 + r + 1, :]
            erow = e1_ref[h, row0 + r:row0 + r + 1, :]
            g = g + jnp.where(b_ref[h].astype(F32) < crow, e2_ref[h].astype(F32) * erow, 0.0)
        w_ref[rows, :] = (g * gelu).astype(BF16)


def _peer_dense_kernel(ht_ref, u_ref, vt_ref, b_ref, e2_ref, cp_ref, e1p_ref, cc_ref, e1c_ref, x_ref, mod_ref,
                       o_ref, acc_ref, a0, a1, w0, w1, *, heads):
    k = pl.program_id(1)
    eb = PEER_EB

    @pl.when(k == 0)
    def _():
        acc_ref[...] = jnp.zeros_like(acc_ref)
        a1[...] = jnp.zeros_like(a1)
        w0[...] = jnp.zeros_like(w0)

    ht = ht_ref[...]
    _peer_gate_block(a1, w1, b_ref, e2_ref, cp_ref, e1p_ref, PEER_ROWS, heads)
    a0[...] = _dot(u_ref[0:eb, :], ht)
    acc_ref[...] += _dot(vt_ref[:, 0:eb], w0[...])
    a1[...] = _dot(u_ref[eb:2 * eb, :], ht)
    _peer_gate_block(a0, w0, b_ref, e2_ref, cc_ref, e1c_ref, 0, heads)
    acc_ref[...] += _dot(vt_ref[:, eb:2 * eb], w1[...])

    @pl.when(k == pl.num_programs(1) - 1)
    def _():
        o_ref[...] = x_ref[...] + mod_ref[5:6, :] * acc_ref[...].T


def peer_dense(cfg, ht, u, vt, tabs, x, mod_l, *, rows):
    tile = _pick(cfg.nc, (512, 256))
    eb2 = 2 * PEER_EB
    n_pairs = u.shape[0] // eb2
    nh = cfg.peer_heads
    assert 2 * PEER_ROWS == 8 and u.shape[0] == PEER_NKEYS * PEER_NKEYS
    row_of = _mod_row_map(cfg, tile)
    b_tab, e2_tab, c_tab, e1_tab = tabs
    full_spec = pl.BlockSpec((nh, PEER_NKEYS, tile), lambda i, k: (0, 0, i))
    prev_spec = pl.BlockSpec((nh, 2 * PEER_ROWS, tile), lambda i, k: (0, jnp.maximum(k - 1, 0), i))
    cur_spec = pl.BlockSpec((nh, 2 * PEER_ROWS, tile), lambda i, k: (0, jnp.minimum(k, n_pairs - 1), i))
    return pl.pallas_call(
        functools.partial(_peer_dense_kernel, heads=nh),
        grid=(rows // tile, n_pairs + 1),
        in_specs=[
            pl.BlockSpec((cfg.d, tile), lambda i, k: (0, i)),
            pl.BlockSpec((eb2, cfg.d), lambda i, k: (jnp.minimum(k, n_pairs - 1), 0)),
            pl.BlockSpec((cfg.d, eb2), lambda i, k: (0, jnp.maximum(k - 1, 0))),
            full_spec, full_spec, prev_spec, prev_spec, cur_spec, cur_spec,
            pl.BlockSpec((tile, cfg.d), lambda i, k: (i, 0)),
            pl.BlockSpec((None, 6, cfg.d), lambda i, k: (row_of(i), 0, 0)),
        ],
        out_specs=pl.BlockSpec((tile, cfg.d), lambda i, k: (i, 0)),
        out_shape=jax.ShapeDtypeStruct((rows, cfg.d), F32),
        scratch_shapes=[
            pltpu.VMEM((cfg.d, tile), F32),
            pltpu.VMEM((PEER_EB, tile), F32), pltpu.VMEM((PEER_EB, tile), F32),
            pltpu.VMEM((PEER_EB, tile), BF16), pltpu.VMEM((PEER_EB, tile), BF16),
        ],
        compiler_params=_params(("arbitrary", "arbitrary")),
        name="peer_dense",
    )(ht, u, vt, b_tab, e2_tab, c_tab, e1_tab, c_tab, e1_tab, x, mod_l)


def _peer_dense_simple_kernel(ht_ref, u_ref, vt_ref, b_ref, e2_ref, c_ref, e1_ref, x_ref, mod_ref, o_ref,
                              acc_ref, w_ref, *, heads, eb):
    e = pl.program_id(1)

    @pl.when(e == 0)
    def _():
        acc_ref[...] = jnp.zeros_like(acc_ref)

    act = _dot(u_ref[...], ht_ref[...])
    gelu = 0.5 * act * (1.0 + lax.erf(act * float(math.sqrt(0.5))))
    per = eb // PEER_NKEYS
    for r in range(per):
        i = e * per + r
        g = jnp.zeros((PEER_NKEYS, act.shape[1]), F32)
        for h in range(heads):
            crow = c_ref[h, pl.ds(i, 1), :]
            erow = e1_ref[h, pl.ds(i, 1), :]
            g = g + jnp.where(b_ref[h].astype(F32) < crow, e2_ref[h] * erow, 0.0)
        rows = slice(r * PEER_NKEYS, (r + 1) * PEER_NKEYS)
        w_ref[rows, :] = (g * gelu[rows]).astype(BF16)
    acc_ref[...] += _dot(vt_ref[...], w_ref[...])

    @pl.when(e == pl.num_programs(1) - 1)
    def _():
        o_ref[...] = x_ref[...] + mod_ref[5:6, :] * acc_ref[...].T


def peer_dense_simple(cfg, ht, u, vt, tabs, x, mod_l, *, rows):
    tile = _pick(cfg.nc, (512, 256))
    eb = 512
    n_exp = u.shape[0]
    nh = cfg.peer_heads
    row_of = _mod_row_map(cfg, tile)
    tab_spec = pl.BlockSpec((nh, PEER_NKEYS, tile), lambda i, e: (0, 0, i))
    return pl.pallas_call(
        functools.partial(_peer_dense_simple_kernel, heads=nh, eb=eb),
        grid=(rows // tile, n_exp // eb),
        in_specs=[
            pl.BlockSpec((cfg.d, tile), lambda i, e: (0, i)),
            pl.BlockSpec((eb, cfg.d), lambda i, e: (e, 0)),
            pl.BlockSpec((cfg.d, eb), lambda i, e: (0, e)),
            tab_spec, tab_spec, tab_spec, tab_spec,
            pl.BlockSpec((tile, cfg.d), lambda i, e: (i, 0)),
            pl.BlockSpec((None, 6, cfg.d), lambda i, e: (row_of(i), 0, 0)),
        ],
        out_specs=pl.BlockSpec((tile, cfg.d), lambda i, e: (i, 0)),
        out_shape=jax.ShapeDtypeStruct((rows, cfg.d), F32),
        scratch_shapes=[pltpu.VMEM((cfg.d, tile), F32), pltpu.VMEM((eb, tile), BF16)],
        compiler_params=_params(("arbitrary", "arbitrary")),
        name="peer_dense_simple",
    )(ht, u, vt, *tabs, x, mod_l)


def _final_norm_kernel(x_ref, w_ref, o_ref):
    x = x_ref[...]
    o_ref[...] = x * lax.rsqrt(jnp.mean(x * x, axis=-1, keepdims=True) + RMS_EPS) * w_ref[...]


def final_norm(cfg, x, w):
    tm = _pick(cfg.nl, (1024, 512, 256))
    return pl.pallas_call(
        _final_norm_kernel,
        grid=(cfg.nl // tm,),
        in_specs=[pl.BlockSpec((tm, cfg.d), lambda i: (i, 0)), pl.BlockSpec((1, cfg.d), lambda i: (0, 0))],
        out_specs=pl.BlockSpec((tm, cfg.d), lambda i: (i, 0)),
        out_shape=jax.ShapeDtypeStruct((cfg.nl, cfg.d), F32),
        compiler_params=_params(("arbitrary",)),
        name="final_norm",
    )(x, w.reshape(1, cfg.d))


def _rot_cols(w):
    q = ROPE_DIM // 4
    return jnp.concatenate([-w[..., q:2 * q], w[..., 0:q], -w[..., 3 * q:4 * q], w[..., 2 * q:3 * q]], axis=-1)


def _in_proj_weight(cfg, w_in, n_cols):
    d = cfg.d
    hg_end = 5 * cfg.hgw
    pool_end = hg_end + POOL_WIDTH
    cq_end = pool_end + cfg.q_lora
    ckv_end = cq_end + cfg.kv_lora
    rope_end = ckv_end + ROPE_DIM
    k_rope = w_in[:, ckv_end:rope_end]
    parts = [w_in[:, rope_end:rope_end + 3 * d], w_in[:, :ckv_end], k_rope, _rot_cols(k_rope)]
    w = jnp.concatenate(parts, axis=1)
    return jnp.pad(w, ((0, 0), (0, n_cols - w.shape[1]))).astype(BF16)


def _mla_q_weight(cfg, w_uq):
    w = w_uq.reshape(cfg.q_lora, cfg.mla_heads, HEAD_DIM + ROPE_DIM)
    rope = w[..., HEAD_DIM:]
    w = jnp.concatenate([w[..., :HEAD_DIM], rope, _rot_cols(rope)], axis=-1)
    return jnp.transpose(w, (1, 0, 2)).astype(BF16)


def _mla_kv_weight(cfg, w_ukv):
    w = w_ukv.reshape(cfg.kv_lora, cfg.mla_heads, 2 * HEAD_DIM)
    return jnp.transpose(w, (1, 0, 2)).astype(BF16)


def _rope_table(cfg):
    rows = cfg.seq // cfg.grid_w
    r, col = jnp.meshgrid(jnp.arange(rows), jnp.arange(cfg.grid_w), indexing="ij")
    n_freq = ROPE_DIM // 4
    freqs = ROPE_THETA ** (-jnp.arange(n_freq, dtype=F32) / n_freq)
    ang_r = r.reshape(-1)[:, None] * freqs
    ang_c = col.reshape(-1)[:, None] * freqs
    cos = jnp.concatenate([jnp.cos(ang_r)] * 2 + [jnp.cos(ang_c)] * 2, axis=1)
    sin = jnp.concatenate([jnp.sin(ang_r)] * 2 + [jnp.sin(ang_c)] * 2, axis=1)
    lat = jnp.tile(jnp.concatenate([cos, sin], axis=1).astype(F32), (cfg.batch, 1))
    ctx = jnp.concatenate([jnp.ones((cfg.nc, ROPE_DIM), F32), jnp.zeros((cfg.nc, ROPE_DIM), F32)], axis=1)
    return jnp.concatenate([lat, ctx], axis=0)


def _forward(cfg, x, c, ctx, c_ctx, w_mod, b_mod, norm_mix, norm_ffn, w_in, hg_lb_logits, hg_norm,
             pool_w, pool_scale, mla_q_norm, mla_w_uq, mla_kv_norm, mla_w_ukv,
             w_branch_a, w_branch_b, w_branch_c, w_out, peer_wq, peer_keys, peer_u, peer_v, final_w):
    d = cfg.d
    assert cfg.seq % HG_CHUNK == 0 and cfg.ctx % HG_CHUNK == 0 and cfg.nl % cfg.ctx == 0
    assert cfg.off_pool % POOL_WIDTH == 0 and cfg.off_cq % cfg.q_lora == 0
    assert cfg.off_ckv % cfg.kv_lora == 0 and cfg.batch < MOD_ROWS
    tn_in = 768 if d % 256 == 0 and cfg.in_cols > 8192 else 256
    n_cols = -(-cfg.in_cols // tn_in) * tn_in
    tm = _pick(cfg.nc, (1024, 512, 256))

    xs = jnp.concatenate([x.reshape(cfg.nl, d), ctx.reshape(cfg.nc, d)], axis=0)
    c_all = jnp.concatenate([c, c_ctx[None], jnp.zeros((MOD_ROWS - cfg.batch - 1, d), F32)], axis=0)
    mod = adaln_tables(cfg, c_all, w_mod, b_mod)
    cs = _rope_table(cfg)

    for l in range(cfg.depth):
        last = l == cfg.depth - 1
        rows = cfg.nl if last else cfg.nt
        mod_l = mod[l]
        p = norm_matmul(cfg, xs, norm_mix[l], mod_l, _in_proj_weight(cfg, w_in[l], n_cols),
                        k0=0, rows=cfg.nt, tm=tm, tn=tn_in, emit_h=False)
        hg_lat, hg_ctx = hgrn_mixer(cfg, p, hg_lb_logits, hg_norm[l], l)
        pw = pool_w[l].astype(BF16)
        pool_lat = pool_mixer(cfg, p, pw, pool_scale[l], n_seq=cfg.batch, seq_len=cfg.seq, row0=0)
        q = mla_queries(cfg, p, mla_q_norm[l], _mla_q_weight(cfg, mla_w_uq[l]), cs)
        k, v = mla_keys_values(cfg, p, mla_kv_norm[l], _mla_kv_weight(cfg, mla_w_ukv[l]), cs)
        att_lat, att_ctx = mla_attention(cfg, q, k, v, with_ctx=not last)
        if last:
            ya, yb, yc = hg_lat, pool_lat, att_lat
        else:
            pool_ctx = pool_mixer(cfg, p, pw, pool_scale[l], n_seq=cfg.batch, seq_len=cfg.ctx, row0=cfg.nl)
            ya = jnp.concatenate([hg_lat, hg_ctx], axis=0)
            yb = jnp.concatenate([pool_lat, pool_ctx], axis=0)
            yc = jnp.concatenate([att_lat, att_ctx], axis=0)
        m = merge_branches(cfg, ya, yb, yc, p, w_branch_a[l].astype(BF16), w_branch_b[l].astype(BF16),
                           w_branch_c[l].astype(BF16), rows=rows)
        xs = matmul_residual(cfg, m, w_out[l].astype(BF16), xs, mod_l, k=2, rows=rows)
        qh, h2 = norm_matmul(cfg, xs, norm_ffn[l], mod_l, peer_wq[l].astype(BF16),
                             k0=3, rows=rows, tm=tm, tn=_pick(peer_wq.shape[2], (512, 256)), emit_h=True)
        tabs = peer_tables(cfg, qh, peer_keys[l].astype(BF16), rows=rows)
        xs = peer_dense_simple(cfg, h2.T, peer_u[l].astype(BF16), peer_v[l].T.astype(BF16), tabs, xs, mod_l, rows=rows)

    return final_norm(cfg, xs, final_w).reshape(cfg.batch, cfg.seq, d)


def kernel(x, c, ctx, c_ctx, w_mod, b_mod, norm_mix, norm_ffn, w_in, hg_lb_logits, hg_norm, pool_w, pool_scale,
           mla_q_norm, mla_w_uq, mla_kv_norm, mla_w_ukv, w_branch_a, w_branch_b, w_branch_c, w_out,
           peer_wq, peer_keys, peer_u, peer_v, final_norm):
    batch, seq, d = x.shape
    cfg = Cfg(d=d, batch=batch, seq=seq, ctx=ctx.shape[1], grid_w=64, depth=w_mod.shape[0],
              hg_heads=hg_lb_logits.shape[2] // HEAD_DIM,
              mla_heads=mla_w_ukv.shape[2] // (2 * HEAD_DIM), q_lora=mla_q_norm.shape[1],
              kv_lora=mla_kv_norm.shape[1], peer_heads=peer_keys.shape[2])
    return _forward(cfg, x, c, ctx, c_ctx, w_mod, b_mod, norm_mix, norm_ffn, w_in, hg_lb_logits, hg_norm,
                    pool_w, pool_scale, mla_q_norm, mla_w_uq, mla_kv_norm, mla_w_ukv,
                    w_branch_a, w_branch_b, w_branch_c, w_out, peer_wq, peer_keys, peer_u, peer_v, final_norm)
```

```python
import functools
import math
from typing import NamedTuple

import numpy as np
import jax
import jax.numpy as jnp
from jax import lax
from jax.experimental import pallas as pl
from jax.experimental.pallas import tpu as pltpu

F32 = jnp.float32
BF16 = jnp.bfloat16

RMS_EPS = 1e-6
ROPE_THETA = 10000.0
HEAD_DIM = 128
ROPE_DIM = 64
QK_DIM = 256
QK_LOG2_SCALE = float((HEAD_DIM + ROPE_DIM) ** -0.5 * math.log2(math.e))
POOL_WINDOWS = (2, 4, 8, 16)
POOL_GROUP = 256
POOL_WIDTH = POOL_GROUP * len(POOL_WINDOWS)
POOL_HALO = 8
PEER_NKEYS = 128
PEER_TOPK = 16
HG_CHUNK = 128
HG_LEVELS = 7
UNRANKED = 99.0
MOD_ROWS = 8
VMEM_LIMIT = 56 * 1024 * 1024


class Cfg(NamedTuple):
    d: int
    batch: int
    seq: int
    ctx: int
    grid_w: int
    depth: int
    hg_heads: int
    mla_heads: int
    q_lora: int
    kv_lora: int
    peer_heads: int

    @property
    def nl(self):
        return self.batch * self.seq

    @property
    def nc(self):
        return self.batch * self.ctx

    @property
    def nt(self):
        return self.nl + self.nc

    @property
    def hgw(self):
        return self.hg_heads * HEAD_DIM

    @property
    def off_gate(self):
        return 0

    @property
    def off_hg(self):
        return 3 * self.d

    @property
    def off_pool(self):
        return self.off_hg + 5 * self.hgw

    @property
    def off_cq(self):
        return self.off_pool + POOL_WIDTH

    @property
    def off_ckv(self):
        return self.off_cq + self.q_lora

    @property
    def off_slab(self):
        return self.off_ckv + self.kv_lora

    @property
    def in_cols(self):
        return self.off_slab + HEAD_DIM


def _params(sem):
    return pltpu.CompilerParams(dimension_semantics=sem, vmem_limit_bytes=VMEM_LIMIT)


def _dot(a, b):
    return jnp.dot(a, b, preferred_element_type=F32)


def _dot_nt(a, b):
    return lax.dot_general(a, b, (((1,), (1,)), ((), ())), preferred_element_type=F32)


def _sigmoid(x):
    return 1.0 / (1.0 + jnp.exp(-x))


def _pick(n, prefs):
    for p in prefs:
        if n % p == 0:
            return p
    raise ValueError(f"no tile for {n} in {prefs}")


def _mod_row_map(cfg, tm):
    n_lat = cfg.nl // tm
    per_batch = cfg.seq // tm
    return lambda i: jnp.where(i < n_lat, i // per_batch, cfg.batch)


def _mod_kernel(c_ref, w_ref, b_ref, o_ref):
    c = c_ref[...]
    s = (c * _sigmoid(c)).astype(BF16)
    o_ref[...] = _dot(s, w_ref[...].astype(BF16)) + b_ref[...]


def adaln_tables(cfg, c_all, w_mod, b_mod):
    d6 = 6 * cfg.d
    tn = _pick(d6, (1024, 768, 512, 256))
    out = pl.pallas_call(
        _mod_kernel,
        grid=(cfg.depth, d6 // tn),
        in_specs=[
            pl.BlockSpec((MOD_ROWS, cfg.d), lambda l, j: (0, 0)),
            pl.BlockSpec((None, cfg.d, tn), lambda l, j: (l, 0, j)),
            pl.BlockSpec((None, 1, tn), lambda l, j: (l, 0, j)),
        ],
        out_specs=pl.BlockSpec((None, MOD_ROWS, tn), lambda l, j: (l, 0, j)),
        out_shape=jax.ShapeDtypeStruct((cfg.depth, MOD_ROWS, d6), F32),
        compiler_params=_params(("arbitrary", "arbitrary")),
        name="adaln_tables",
    )(c_all, w_mod, b_mod.reshape(cfg.depth, 1, d6))
    return out.reshape(cfg.depth, MOD_ROWS, 6, cfg.d)


def _norm_matmul_kernel(x_ref, gain_ref, mod_ref, w_ref, *rest, k0, emit_h):
    if emit_h:
        o_ref, hout_ref, h_ref = rest
    else:
        o_ref, h_ref = rest

    @pl.when(pl.program_id(1) == 0)
    def _():
        x = x_ref[...]
        y = x * lax.rsqrt(jnp.mean(x * x, axis=-1, keepdims=True) + RMS_EPS) * gain_ref[...]
        h = (y * (1.0 + mod_ref[k0 + 1:k0 + 2, :]) + mod_ref[k0:k0 + 1, :]).astype(BF16)
        h_ref[...] = h
        if emit_h:
            hout_ref[...] = h

    o_ref[...] = _dot(h_ref[...], w_ref[...])


def norm_matmul(cfg, x, gain, mod_l, w, *, k0, rows, tm, tn, emit_h):
    n = w.shape[1]
    row_of = _mod_row_map(cfg, tm)
    out_shape = [jax.ShapeDtypeStruct((rows, n), F32)]
    out_specs = [pl.BlockSpec((tm, tn), lambda i, j: (i, j))]
    if emit_h:
        out_shape.append(jax.ShapeDtypeStruct((rows, cfg.d), BF16))
        out_specs.append(pl.BlockSpec((tm, cfg.d), lambda i, j: (i, 0)))
    res = pl.pallas_call(
        functools.partial(_norm_matmul_kernel, k0=k0, emit_h=emit_h),
        grid=(rows // tm, n // tn),
        in_specs=[
            pl.BlockSpec((tm, cfg.d), lambda i, j: (i, 0)),
            pl.BlockSpec((1, cfg.d), lambda i, j: (0, 0)),
            pl.BlockSpec((None, 6, cfg.d), lambda i, j: (row_of(i), 0, 0)),
            pl.BlockSpec((cfg.d, tn), lambda i, j: (0, j)),
        ],
        out_specs=out_specs,
        out_shape=out_shape,
        scratch_shapes=[pltpu.VMEM((tm, cfg.d), BF16)],
        compiler_params=_params(("arbitrary", "arbitrary")),
        name="norm_matmul_h" if emit_h else "norm_matmul",
    )(x, gain.reshape(1, cfg.d), mod_l, w)
    return res if emit_h else res[0]


def _hgrn_consts():
    c = HG_CHUNK
    out = []
    for rev in (False, True):
        p = np.arange(c) if not rev else c - 1 - np.arange(c)
        pt, pu = p[:, None], p[None, :]
        g = np.zeros((HG_LEVELS + 1, c, c), np.float32)
        up = np.zeros((HG_LEVELS, c, c), np.float32)
        g[0] = pu <= pt
        for l in range(HG_LEVELS):
            m = 1 << l
            blk = p >> (l + 1)
            upper = ((p >> l) & 1) == 1
            mid = (blk * 2 * m + m)[:, None]
            same = blk[:, None] == blk[None, :]
            g_up = same & (pu >= mid) & (pu <= pt)
            g_lo = same & (pu > pt) & (pu < mid)
            g[1 + l] = np.where(upper[:, None], g_up, g_lo)
            up[l] = np.broadcast_to(upper[:, None], (c, c))
        x = pt ^ pu
        lv = np.where(pu < pt, np.floor(np.log2(np.maximum(x, 1))), -1.0).astype(np.float32)
        out.append((jnp.asarray(g.reshape(-1, c), BF16), jnp.asarray(up, F32), jnp.asarray(lv, F32)))
    return out


def _hgrn_chunk(q_raw, v, f_raw, log_lb, log_1mlb, one_m_lb, g_ref, up_ref, lv_ref, st_ref, end_row):
    c = HG_CHUNK
    q = q_raw * _sigmoid(q_raw)
    log_sig = jnp.minimum(f_raw, 0.0) - jnp.log1p(jnp.exp(-jnp.abs(f_raw)))
    t = log_1mlb + log_sig
    log_f = jnp.maximum(log_lb, t) + jnp.log1p(jnp.exp(-jnp.abs(log_lb - t)))
    k = one_m_lb / (1.0 + jnp.exp(f_raw))
    hi = log_f.astype(BF16)
    lo = (log_f - hi.astype(F32)).astype(BF16)
    a2 = _dot(g_ref[...], jnp.concatenate([hi, lo], axis=1))
    a = a2[:, :HEAD_DIM] + a2[:, HEAD_DIM:]
    b = a[0:c]
    lv = lv_ref[...]
    scores = jnp.zeros((c, c), F32)
    for l in range(HG_LEVELS):
        e = jnp.exp(a[(1 + l) * c:(2 + l) * c])
        x = (jnp.where(up_ref[l] > 0.5, q, k) * e).astype(BF16)
        scores = jnp.where(lv == float(l), _dot_nt(x, x), scores)
    b_end = b[end_row:end_row + 1, :]
    qb = (q * jnp.exp(b)).astype(BF16)
    kd = (k * jnp.exp(b_end - b)).astype(BF16)
    st = st_ref[...]
    vb = v.astype(BF16)
    o = (_dot(scores.astype(BF16), vb) + _dot_nt(qb, st.astype(BF16))
         + jnp.sum(q * k, axis=-1, keepdims=True) * v)
    st_ref[...] = st * jnp.exp(b_end) + _dot(v.T.astype(BF16), kd)
    return o


def _hgrn_kernel(ql, ffl, fbl, il, gl, qc, ffc, fbc, ic, gc, logit_ref, nw_ref,
                 gf_ref, upf_ref, lvf_ref, gb_ref, upb_ref, lvb_ref,
                 yl_ref, yc_ref, ofl, obl, ofc, obc, stf, stb, *, layer, seq, ctx):
    c = HG_CHUNK
    depth = logit_ref.shape[0]
    lg = [logit_ref[dd] for dd in range(depth)]
    mx = functools.reduce(jnp.maximum, lg)
    ex = [jnp.exp(v - mx) for v in lg]
    tot = functools.reduce(jnp.add, ex)
    cum = [ex[0] / tot]
    for dd in range(1, layer + 1):
        cum.append(cum[-1] + ex[dd] / tot)
    lb = cum[layer] - cum[0]
    log_lb = jnp.log(lb)
    log_1mlb = jnp.log1p(-lb)
    one_m_lb = 1.0 - lb

    stf[...] = jnp.zeros_like(stf)
    stb[...] = jnp.zeros_like(stb)

    def segment(q_ref, ff_ref, fb_ref, i_ref, of_ref, ob_ref, n):
        def body(j, carry):
            rf = pl.multiple_of(j * c, c)
            rb = pl.multiple_of((n - 1 - j) * c, c)
            of_ref[pl.ds(rf, c), :] = _hgrn_chunk(
                q_ref[pl.ds(rf, c), :], i_ref[pl.ds(rf, c), :], ff_ref[pl.ds(rf, c), :],
                log_lb[0:1], log_1mlb[0:1], one_m_lb[0:1], gf_ref, upf_ref, lvf_ref, stf, c - 1)
            ob_ref[pl.ds(rb, c), :] = _hgrn_chunk(
                q_ref[pl.ds(rb, c), :], i_ref[pl.ds(rb, c), :], fb_ref[pl.ds(rb, c), :],
                log_lb[1:2], log_1mlb[1:2], one_m_lb[1:2], gb_ref, upb_ref, lvb_ref, stb, 0)
            return carry
        lax.fori_loop(0, n, body, 0, unroll=4 if n % 4 == 0 else 2)

    segment(qc, ffc, fbc, ic, ofc, obc, ctx // c)
    segment(ql, ffl, fbl, il, ofl, obl, seq // c)

    nw = nw_ref[...]

    def readout(of_ref, ob_ref, g_ref, y_ref, n):
        def body(j, carry):
            r = pl.multiple_of(j * c, c)
            o = of_ref[pl.ds(r, c), :] + ob_ref[pl.ds(r, c), :]
            y = o * lax.rsqrt(jnp.mean(o * o, axis=-1, keepdims=True) + RMS_EPS) * nw
            g = g_ref[pl.ds(r, c), :]
            y_ref[pl.ds(r, c), :] = (y * (g * _sigmoid(g))).astype(y_ref.dtype)
            return carry
        lax.fori_loop(0, n, body, 0)

    readout(ofc, obc, gc, yc_ref, ctx // c)
    readout(ofl, obl, gl, yl_ref, seq // c)


def hgrn_mixer(cfg, p, logits, norm_w, layer):
    hd = HEAD_DIM
    nh = cfg.hg_heads
    col0 = cfg.off_hg // hd
    ctx_blk0 = cfg.nl // cfg.ctx
    (gf, upf, lvf), (gb, upb, lvb) = _hgrn_consts()

    def lat_spec(part):
        return pl.BlockSpec((cfg.seq, hd), lambda b, h, part=part: (b, col0 + part * nh + h))

    def ctx_spec(part):
        return pl.BlockSpec((cfg.ctx, hd), lambda b, h, part=part: (ctx_blk0 + b, col0 + part * nh + h))

    parts = (0, 1, 2, 3, 4)

    def const(arr):
        return pl.BlockSpec(arr.shape, lambda b, h, nd=arr.ndim: (0,) * nd)

    y_lat, y_ctx = pl.pallas_call(
        functools.partial(_hgrn_kernel, layer=layer, seq=cfg.seq, ctx=cfg.ctx),
        grid=(cfg.batch, nh),
        in_specs=[lat_spec(k) for k in parts] + [ctx_spec(k) for k in parts] + [
            pl.BlockSpec((cfg.depth, 2, hd), lambda b, h: (0, 0, h)),
            pl.BlockSpec((1, hd), lambda b, h: (0, 0)),
            const(gf), const(upf), const(lvf), const(gb), const(upb), const(lvb),
        ],
        out_specs=[
            pl.BlockSpec((cfg.seq, hd), lambda b, h: (b, h)),
            pl.BlockSpec((cfg.ctx, hd), lambda b, h: (b, h)),
        ],
        out_shape=[
            jax.ShapeDtypeStruct((cfg.nl, cfg.hgw), BF16),
            jax.ShapeDtypeStruct((cfg.nc, cfg.hgw), BF16),
        ],
        scratch_shapes=[
            pltpu.VMEM((cfg.seq, hd), F32), pltpu.VMEM((cfg.seq, hd), F32),
            pltpu.VMEM((cfg.ctx, hd), F32), pltpu.VMEM((cfg.ctx, hd), F32),
            pltpu.VMEM((hd, hd), F32), pltpu.VMEM((hd, hd), F32),
        ],
        compiler_params=_params(("arbitrary", "arbitrary")),
        name="hgrn_mixer",
    )(*([p] * 10), logits, norm_w.reshape(1, hd), gf, upf, lvf, gb, upb, lvb)
    return y_lat, y_ctx


def _pool_kernel(prev_ref, cur_ref, next_ref, w_ref, scale_ref, o_ref, buf_ref, *, seq_len, tile, n_tiles):
    i = pl.program_id(1)
    h = POOL_HALO
    cur = cur_ref[...]
    buf_ref[0:h, :] = jnp.where(i > 0, prev_ref[...], 0.0)
    buf_ref[h:h + tile, :] = cur
    buf_ref[h + tile:2 * h + tile, :] = jnp.where(i < n_tiles - 1, next_ref[...], 0.0)
    pos = i * tile + lax.broadcasted_iota(jnp.int32, (tile, POOL_GROUP), 0)
    for gi, win in enumerate(POOL_WINDOWS):
        half = win // 2
        cols = slice(gi * POOL_GROUP, (gi + 1) * POOL_GROUP)
        acc = buf_ref[h - half:h - half + tile, cols]
        for dlt in range(-half + 1, half):
            acc = acc + buf_ref[h + dlt:h + dlt + tile, cols]
        cnt = (jnp.minimum(pos + half, seq_len) - jnp.maximum(pos - half, 0)).astype(F32)
        pooled = acc / cnt - cur[:, cols]
        mixed = _dot(pooled.astype(BF16), w_ref[gi])
        o_ref[:, cols] = (mixed * scale_ref[:, cols]).astype(o_ref.dtype)


def pool_mixer(cfg, p, w_pool, scale, *, n_seq, seq_len, row0):
    tile = _pick(seq_len, (512, 256, 128))
    n_tiles = seq_len // tile
    blk0 = row0 // tile
    col = cfg.off_pool // POOL_WIDTH
    per8 = tile // POOL_HALO
    last8 = cfg.nt // POOL_HALO - 1

    def cur_map(s, i):
        return (blk0 + s * n_tiles + i, col)

    def prev_map(s, i):
        return (jnp.maximum((blk0 + s * n_tiles + i) * per8 - 1, 0), col)

    def next_map(s, i):
        return (jnp.minimum((blk0 + s * n_tiles + i + 1) * per8, last8), col)

    return pl.pallas_call(
        functools.partial(_pool_kernel, seq_len=seq_len, tile=tile, n_tiles=n_tiles),
        grid=(n_seq, n_tiles),
        in_specs=[
            pl.BlockSpec((POOL_HALO, POOL_WIDTH), prev_map),
            pl.BlockSpec((tile, POOL_WIDTH), cur_map),
            pl.BlockSpec((POOL_HALO, POOL_WIDTH), next_map),
            pl.BlockSpec((len(POOL_WINDOWS), POOL_GROUP, POOL_GROUP), lambda s, i: (0, 0, 0)),
            pl.BlockSpec((1, POOL_WIDTH), lambda s, i: (0, 0)),
        ],
        out_specs=pl.BlockSpec((tile, POOL_WIDTH), lambda s, i: (s * n_tiles + i, 0)),
        out_shape=jax.ShapeDtypeStruct((n_seq * seq_len, POOL_WIDTH), BF16),
        scratch_shapes=[pltpu.VMEM((tile + 2 * POOL_HALO, POOL_WIDTH), F32)],
        compiler_params=_params(("arbitrary", "arbitrary")),
        name="pool_mixer",
    )(p, p, p, w_pool, scale.reshape(1, POOL_WIDTH))


def _rope_slab(slab, cs):
    t = slab * cs
    r = t + pltpu.roll(t, ROPE_DIM, axis=1)
    lane = lax.broadcasted_iota(jnp.int32, r.shape, 1)
    return jnp.where(lane < ROPE_DIM, r, 0.0)


def _rms_bf16(x, w):
    return (x * lax.rsqrt(jnp.mean(x * x, axis=-1, keepdims=True) + RMS_EPS) * w).astype(BF16)


def _mla_q_kernel(cq_ref, nw_ref, cs_ref, w_ref, o_ref, n_ref):
    @pl.when(pl.program_id(1) == 0)
    def _():
        n_ref[...] = _rms_bf16(cq_ref[...], nw_ref[...])

    y = _dot(n_ref[...], w_ref[...])
    rope = _rope_slab(y[:, HEAD_DIM:], cs_ref[...])
    o_ref[...] = (jnp.concatenate([y[:, :HEAD_DIM], rope], axis=1) * QK_LOG2_SCALE).astype(o_ref.dtype)


def mla_queries(cfg, p, norm_w, w_q, cs):
    tm = _pick(cfg.nc, (1024, 512, 256))
    nh = cfg.mla_heads
    col = cfg.off_cq // cfg.q_lora
    return pl.pallas_call(
        _mla_q_kernel,
        grid=(cfg.nt // tm, nh),
        in_specs=[
            pl.BlockSpec((tm, cfg.q_lora), lambda i, h: (i, col)),
            pl.BlockSpec((1, cfg.q_lora), lambda i, h: (0, 0)),
            pl.BlockSpec((tm, HEAD_DIM), lambda i, h: (i, 0)),
            pl.BlockSpec((None, cfg.q_lora, QK_DIM), lambda i, h: (h, 0, 0)),
        ],
        out_specs=pl.BlockSpec((None, tm, QK_DIM), lambda i, h: (h, i, 0)),
        out_shape=jax.ShapeDtypeStruct((nh, cfg.nt, QK_DIM), BF16),
        scratch_shapes=[pltpu.VMEM((tm, cfg.q_lora), BF16)],
        compiler_params=_params(("arbitrary", "arbitrary")),
        name="mla_queries",
    )(p, norm_w.reshape(1, cfg.q_lora), cs, w_q)


def _mla_kv_kernel(ckv_ref, slab_ref, nw_ref, cs_ref, w_ref, k_ref, v_ref, n_ref, r_ref):
    @pl.when(pl.program_id(1) == 0)
    def _():
        n_ref[...] = _rms_bf16(ckv_ref[...], nw_ref[...])
        r_ref[...] = _rope_slab(slab_ref[...], cs_ref[...])

    y = _dot(n_ref[...], w_ref[...])
    k_ref[...] = jnp.concatenate([y[:, :HEAD_DIM], r_ref[...]], axis=1).astype(k_ref.dtype)
    lane = lax.broadcasted_iota(jnp.int32, (y.shape[0], HEAD_DIM), 1)
    ones_col = jnp.where(lane == 0, 1.0, 0.0)
    v_ref[...] = jnp.concatenate([y[:, HEAD_DIM:], ones_col], axis=1).astype(v_ref.dtype)


def mla_keys_values(cfg, p, norm_w, w_kv, cs):
    tm = _pick(cfg.nc, (1024, 512, 256))
    nh = cfg.mla_heads
    col = cfg.off_ckv // cfg.kv_lora
    slab_col = cfg.off_slab // HEAD_DIM
    return pl.pallas_call(
        _mla_kv_kernel,
        grid=(cfg.nt // tm, nh),
        in_specs=[
            pl.BlockSpec((tm, cfg.kv_lora), lambda i, h: (i, col)),
            pl.BlockSpec((tm, HEAD_DIM), lambda i, h: (i, slab_col)),
            pl.BlockSpec((1, cfg.kv_lora), lambda i, h: (0, 0)),
            pl.BlockSpec((tm, HEAD_DIM), lambda i, h: (i, 0)),
            pl.BlockSpec((None, cfg.kv_lora, 2 * HEAD_DIM), lambda i, h: (h, 0, 0)),
        ],
        out_specs=[
            pl.BlockSpec((None, tm, QK_DIM), lambda i, h: (h, i, 0)),
            pl.BlockSpec((None, tm, 2 * HEAD_DIM), lambda i, h: (h, i, 0)),
        ],
        out_shape=[
            jax.ShapeDtypeStruct((nh, cfg.nt, QK_DIM), BF16),
            jax.ShapeDtypeStruct((nh, cfg.nt, 2 * HEAD_DIM), BF16),
        ],
        scratch_shapes=[pltpu.VMEM((tm, cfg.kv_lora), BF16), pltpu.VMEM((tm, HEAD_DIM), F32)],
        compiler_params=_params(("arbitrary", "arbitrary")),
        name="mla_keys_values",
    )(p, p, norm_w.reshape(1, cfg.kv_lora), cs, w_kv)


ATTN_SUB = 256
ATTN_KT = 512


def _softmax_pv(s_list, v_list):
    m = functools.reduce(jnp.maximum, [jnp.max(s, axis=-1, keepdims=True) for s in s_list])
    o = None
    for s, v in zip(s_list, v_list):
        part = _dot(jnp.exp2(s - m).astype(BF16), v)
        o = part if o is None else o + part
    return o[:, :HEAD_DIM] / o[:, HEAD_DIM:HEAD_DIM + 1]


def _attn_lat_kernel(q_ref, kl_ref, kc_ref, vl_ref, vc_ref, o_ref, s_ref):
    n_keys = kl_ref.shape[0]
    for r in range(0, q_ref.shape[0], ATTN_SUB):
        rows = slice(r, r + ATTN_SUB)
        q = q_ref[rows, :]
        s_ref[rows, :] = _dot_nt(q, kl_ref[...])
        sc = _dot_nt(q, kc_ref[...])
        m = jnp.maximum(jnp.max(s_ref[rows, :], axis=-1, keepdims=True), jnp.max(sc, axis=-1, keepdims=True))
        o = _dot(jnp.exp2(sc - m).astype(BF16), vc_ref[...])
        for c in range(0, n_keys, ATTN_KT):
            p = jnp.exp2(s_ref[rows, c:c + ATTN_KT] - m).astype(BF16)
            o = o + _dot(p, vl_ref[c:c + ATTN_KT, :])
        o_ref[rows, :] = (o[:, :HEAD_DIM] / o[:, HEAD_DIM:HEAD_DIM + 1]).astype(o_ref.dtype)


def _attn_ctx_kernel(q_ref, kc_ref, vc_ref, o_ref):
    o_ref[...] = _softmax_pv([_dot_nt(q_ref[...], kc_ref[...])], [vc_ref[...]]).astype(o_ref.dtype)


def mla_attention(cfg, q, k, v, *, with_ctx):
    nh = cfg.mla_heads
    vw = 2 * HEAD_DIM
    tq = _pick(cfg.seq, (1024, 512, 256))
    nq = cfg.seq // tq
    cb0 = cfg.nl // cfg.ctx
    y_lat = pl.pallas_call(
        _attn_lat_kernel,
        grid=(cfg.batch, nh, nq),
        in_specs=[
            pl.BlockSpec((None, tq, QK_DIM), lambda b, h, i: (h, b * nq + i, 0)),
            pl.BlockSpec((None, cfg.seq, QK_DIM), lambda b, h, i: (h, b, 0)),
            pl.BlockSpec((None, cfg.ctx, QK_DIM), lambda b, h, i: (h, cb0 + b, 0)),
            pl.BlockSpec((None, cfg.seq, vw), lambda b, h, i: (h, b, 0)),
            pl.BlockSpec((None, cfg.ctx, vw), lambda b, h, i: (h, cb0 + b, 0)),
        ],
        out_specs=pl.BlockSpec((tq, HEAD_DIM), lambda b, h, i: (b * nq + i, h)),
        out_shape=jax.ShapeDtypeStruct((cfg.nl, nh * HEAD_DIM), BF16),
        scratch_shapes=[pltpu.VMEM((tq, cfg.seq), F32)],
        compiler_params=_params(("arbitrary", "arbitrary", "arbitrary")),
        name="mla_attention",
    )(q, k, k, v, v)
    if not with_ctx:
        return y_lat, None
    y_ctx = pl.pallas_call(
        _attn_ctx_kernel,
        grid=(cfg.batch, nh),
        in_specs=[
            pl.BlockSpec((None, cfg.ctx, QK_DIM), lambda b, h: (h, cb0 + b, 0)),
            pl.BlockSpec((None, cfg.ctx, QK_DIM), lambda b, h: (h, cb0 + b, 0)),
            pl.BlockSpec((None, cfg.ctx, vw), lambda b, h: (h, cb0 + b, 0)),
        ],
        out_specs=pl.BlockSpec((cfg.ctx, HEAD_DIM), lambda b, h: (b, h)),
        out_shape=jax.ShapeDtypeStruct((cfg.nc, nh * HEAD_DIM), BF16),
        compiler_params=_params(("arbitrary", "arbitrary")),
        name="mla_attention_ctx",
    )(q, k, v)
    return y_lat, y_ctx


def _merge_kernel(ya_ref, yb_ref, yc_ref, ga_ref, gb_ref, gc_ref, wa_ref, wb_ref, wc_ref, o_ref):
    m = (_sigmoid(ga_ref[...]) * _dot(ya_ref[...], wa_ref[...])
         + _sigmoid(gb_ref[...]) * _dot(yb_ref[...], wb_ref[...])
         + _sigmoid(gc_ref[...]) * _dot(yc_ref[...], wc_ref[...]))
    o_ref[...] = m.astype(o_ref.dtype)


def merge_branches(cfg, ya, yb, yc, p, wa, wb, wc, *, rows):
    tm = _pick(cfg.nc, (512, 256))
    tn = _pick(cfg.d, (512, 256))
    gcols = cfg.d // tn

    def gate_spec(k):
        return pl.BlockSpec((tm, tn), lambda i, j, k=k: (i, k * gcols + j))

    return pl.pallas_call(
        _merge_kernel,
        grid=(rows // tm, cfg.d // tn),
        in_specs=[
            pl.BlockSpec((tm, ya.shape[1]), lambda i, j: (i, 0)),
            pl.BlockSpec((tm, yb.shape[1]), lambda i, j: (i, 0)),
            pl.BlockSpec((tm, yc.shape[1]), lambda i, j: (i, 0)),
            gate_spec(0), gate_spec(1), gate_spec(2),
            pl.BlockSpec((wa.shape[0], tn), lambda i, j: (0, j)),
            pl.BlockSpec((wb.shape[0], tn), lambda i, j: (0, j)),
            pl.BlockSpec((wc.shape[0], tn), lambda i, j: (0, j)),
        ],
        out_specs=pl.BlockSpec((tm, tn), lambda i, j: (i, j)),
        out_shape=jax.ShapeDtypeStruct((rows, cfg.d), BF16),
        compiler_params=_params(("arbitrary", "arbitrary")),
        name="merge_branches",
    )(ya, yb, yc, p, p, p, wa, wb, wc)


def _matmul_resid_kernel(a_ref, w_ref, x_ref, mod_ref, o_ref, *, k):
    o_ref[...] = x_ref[...] + mod_ref[k:k + 1, :] * _dot(a_ref[...], w_ref[...])


def matmul_residual(cfg, a, w, x, mod_l, *, k, rows):
    tm = _pick(cfg.nc, (1024, 512, 256))
    tn = _pick(cfg.d, (512, 256))
    row_of = _mod_row_map(cfg, tm)
    return pl.pallas_call(
        functools.partial(_matmul_resid_kernel, k=k),
        grid=(rows // tm, cfg.d // tn),
        in_specs=[
            pl.BlockSpec((tm, a.shape[1]), lambda i, j: (i, 0)),
            pl.BlockSpec((a.shape[1], tn), lambda i, j: (0, j)),
            pl.BlockSpec((tm, tn), lambda i, j: (i, j)),
            pl.BlockSpec((None, 6, tn), lambda i, j: (row_of(i), 0, j)),
        ],
        out_specs=pl.BlockSpec((tm, tn), lambda i, j: (i, j)),
        out_shape=jax.ShapeDtypeStruct((rows, cfg.d), F32),
        compiler_params=_params(("arbitrary", "arbitrary")),
        name="matmul_residual",
    )(a, w, x, mod_l)


def _top16_rows(s, top_ref):
    n = s.shape[0]
    row = lax.broadcasted_iota(jnp.int32, s.shape, 0).astype(F32)
    rank = jnp.full(s.shape, UNRANKED, F32)
    work = s
    for r in range(PEER_TOPK):
        m = jnp.max(work, axis=0, keepdims=True)
        first = jnp.min(jnp.where(work == m, row, float(n)), axis=0, keepdims=True)
        sel = row == first
        rank = jnp.where(sel, float(r), rank)
        work = jnp.where(sel, -jnp.inf, work)
        top_ref[r:r + 1, :] = m
    return rank


def _peer_tables_kernel(qh_ref, keys_ref, b_ref, e2_ref, c_ref, e1_ref, top1_ref, top2_ref, cnt_ref, *, tile):
    lanes = HEAD_DIM
    k1 = keys_ref[0]
    k2 = keys_ref[1]
    for part in range(tile // lanes):
        rows = slice(part * lanes, (part + 1) * lanes)
        cols = slice(part * lanes, (part + 1) * lanes)
        qh = qh_ref[rows, :].astype(BF16)
        s1 = _dot_nt(k1, qh[:, :lanes])
        s2 = _dot_nt(k2, qh[:, lanes:])
        rank1 = _top16_rows(s1, top1_ref)
        rank2 = _top16_rows(s2, top2_ref)
        t1 = top1_ref[...]
        t2 = top2_ref[...]
        pieces = [t1[0:1] + t2]
        pos = [lax.broadcasted_iota(jnp.int32, (PEER_TOPK, lanes), 0).astype(F32)]
        for a in range(1, 8):
            pieces.append(t1[a:a + 1] + t2[0:8])
            pos.append(lax.broadcasted_iota(jnp.int32, (8, lanes), 0).astype(F32) + float(a * PEER_TOPK))
        pieces.append(t1[8:16] + t2[0:1])
        pos.append((lax.broadcasted_iota(jnp.int32, (8, lanes), 0).astype(F32) + 8.0) * float(PEER_TOPK))
        cand = jnp.concatenate(pieces, axis=0)
        cpos = jnp.concatenate(pos, axis=0)
        a_row = lax.broadcasted_iota(jnp.int32, (PEER_TOPK, lanes), 0).astype(F32)
        cnt = jnp.zeros((PEER_TOPK, lanes), F32)
        z = jnp.zeros((1, lanes), F32)
        best0 = None
        for r in range(PEER_TOPK):
            m = jnp.max(cand, axis=0, keepdims=True)
            first = jnp.min(jnp.where(cand == m, cpos, 1e9), axis=0, keepdims=True)
            cand = jnp.where(cpos == first, -jnp.inf, cand)
            cnt = cnt + jnp.where(a_row == jnp.floor(first * (1.0 / PEER_TOPK)), 1.0, 0.0)
            if r == 0:
                best0 = m
            z = z + jnp.exp(m - best0)
        cnt_ref[...] = cnt
        c_tab = jnp.zeros((PEER_NKEYS, lanes), F32)
        for a in range(PEER_TOPK):
            c_tab = jnp.where(rank1 == float(a), cnt_ref[a:a + 1, :], c_tab)
        b_ref[:, cols] = rank2.astype(b_ref.dtype)
        c_ref[:, cols] = c_tab
        e1_ref[:, cols] = jnp.exp(s1 - t1[0:1]) / z
        e2_ref[:, cols] = jnp.exp(s2 - t2[0:1]).astype(e2_ref.dtype)


def peer_tables(cfg, qh, keys, *, rows):
    tile = 256
    nh = cfg.peer_heads
    shapes = [jax.ShapeDtypeStruct((nh, PEER_NKEYS, rows), dt) for dt in (BF16, BF16, F32, F32)]
    out_spec = pl.BlockSpec((None, PEER_NKEYS, tile), lambda i, h: (h, 0, i))
    return pl.pallas_call(
        functools.partial(_peer_tables_kernel, tile=tile),
        grid=(rows // tile, nh),
        in_specs=[
            pl.BlockSpec((tile, 2 * HEAD_DIM), lambda i, h: (i, h)),
            pl.BlockSpec((2, None, PEER_NKEYS, HEAD_DIM), lambda i, h: (0, h, 0, 0)),
        ],
        out_specs=[out_spec] * 4,
        out_shape=shapes,
        scratch_shapes=[pltpu.VMEM((PEER_TOPK, HEAD_DIM), F32)] * 3,
        compiler_params=_params(("arbitrary", "arbitrary")),
        name="peer_tables",
    )(qh, keys)


PEER_ROWS = 8
PEER_EB = PEER_ROWS * PEER_NKEYS
PEER_DOT_ROWS = 1


def _peer_dense_kernel(ht_ref, u_ref, vt_ref, b_ref, e2_ref, c_ref, e1_ref, x_ref, mod_ref, o_ref,
                       acc_ref, w_ref, *, heads):
    e = pl.program_id(1)

    @pl.when(e == 0)
    def _():
        acc_ref[...] = jnp.zeros_like(acc_ref)

    zero = jnp.zeros((), BF16)
    ht = ht_ref[...]
    for r0 in range(0, PEER_ROWS, PEER_DOT_ROWS):
        blk = slice(r0 * PEER_NKEYS, (r0 + PEER_DOT_ROWS) * PEER_NKEYS)
        act = _dot(u_ref[blk, :], ht)
        gelu = (0.5 * act * (1.0 + lax.erf(act * float(math.sqrt(0.5))))).astype(BF16)
        for q in range(PEER_DOT_ROWS):
            r = r0 + q
            g = jnp.zeros((PEER_NKEYS, act.shape[1]), BF16)
            for h in range(heads):
                crow = c_ref[h, r:r + 1, :].astype(BF16)
                erow = e1_ref[h, r:r + 1, :].astype(BF16)
                g = g + jnp.where(b_ref[h] < crow, e2_ref[h] * erow, zero)
            w_ref[r * PEER_NKEYS:(r + 1) * PEER_NKEYS, :] = g * gelu[q * PEER_NKEYS:(q + 1) * PEER_NKEYS]
    acc_ref[...] += _dot(vt_ref[...], w_ref[...])

    @pl.when(e == pl.num_programs(1) - 1)
    def _():
        o_ref[...] = x_ref[...] + mod_ref[5:6, :] * acc_ref[...].T


def peer_dense(cfg, ht, u, vt, tabs, x, mod_l, *, rows):
    tile = _pick(cfg.nc, (512, 256))
    eb = PEER_EB
    n_exp = u.shape[0]
    nh = cfg.peer_heads
    row_of = _mod_row_map(cfg, tile)
    tab_spec = pl.BlockSpec((nh, PEER_NKEYS, tile), lambda i, e: (0, 0, i))
    row_spec = pl.BlockSpec((nh, PEER_ROWS, tile), lambda i, e: (0, e, i))
    return pl.pallas_call(
        functools.partial(_peer_dense_kernel, heads=nh),
        grid=(rows // tile, n_exp // eb),
        in_specs=[
            pl.BlockSpec((cfg.d, tile), lambda i, e: (0, i)),
            pl.BlockSpec((eb, cfg.d), lambda i, e: (e, 0)),
            pl.BlockSpec((cfg.d, eb), lambda i, e: (0, e)),
            tab_spec, tab_spec, row_spec, row_spec,
            pl.BlockSpec((tile, cfg.d), lambda i, e: (i, 0)),
            pl.BlockSpec((None, 6, cfg.d), lambda i, e: (row_of(i), 0, 0)),
        ],
        out_specs=pl.BlockSpec((tile, cfg.d), lambda i, e: (i, 0)),
        out_shape=jax.ShapeDtypeStruct((rows, cfg.d), F32),
        scratch_shapes=[pltpu.VMEM((cfg.d, tile), F32), pltpu.VMEM((eb, tile), BF16)],
        compiler_params=_params(("arbitrary", "arbitrary")),
        name="peer_dense",
    )(ht, u, vt, *tabs, x, mod_l)


def _final_norm_kernel(x_ref, w_ref, o_ref):
    x = x_ref[...]
    o_ref[...] = x * lax.rsqrt(jnp.mean(x * x, axis=-1, keepdims=True) + RMS_EPS) * w_ref[...]


def final_norm(cfg, x, w):
    tm = _pick(cfg.nl, (1024, 512, 256))
    return pl.pallas_call(
        _final_norm_kernel,
        grid=(cfg.nl // tm,),
        in_specs=[pl.BlockSpec((tm, cfg.d), lambda i: (i, 0)), pl.BlockSpec((1, cfg.d), lambda i: (0, 0))],
        out_specs=pl.BlockSpec((tm, cfg.d), lambda i: (i, 0)),
        out_shape=jax.ShapeDtypeStruct((cfg.nl, cfg.d), F32),
        compiler_params=_params(("arbitrary",)),
        name="final_norm",
    )(x, w.reshape(1, cfg.d))


def _rot_cols(w):
    q = ROPE_DIM // 4
    return jnp.concatenate([-w[..., q:2 * q], w[..., 0:q], -w[..., 3 * q:4 * q], w[..., 2 * q:3 * q]], axis=-1)


def _in_proj_weight(cfg, w_in, n_cols):
    d = cfg.d
    hg_end = 5 * cfg.hgw
    pool_end = hg_end + POOL_WIDTH
    cq_end = pool_end + cfg.q_lora
    ckv_end = cq_end + cfg.kv_lora
    rope_end = ckv_end + ROPE_DIM
    k_rope = w_in[:, ckv_end:rope_end]
    parts = [w_in[:, rope_end:rope_end + 3 * d], w_in[:, :ckv_end], k_rope, _rot_cols(k_rope)]
    w = jnp.concatenate(parts, axis=1)
    return jnp.pad(w, ((0, 0), (0, n_cols - w.shape[1]))).astype(BF16)


def _mla_q_weight(cfg, w_uq):
    w = w_uq.reshape(cfg.q_lora, cfg.mla_heads, HEAD_DIM + ROPE_DIM)
    rope = w[..., HEAD_DIM:]
    w = jnp.concatenate([w[..., :HEAD_DIM], rope, _rot_cols(rope)], axis=-1)
    return jnp.transpose(w, (1, 0, 2)).astype(BF16)


def _mla_kv_weight(cfg, w_ukv):
    w = w_ukv.reshape(cfg.kv_lora, cfg.mla_heads, 2 * HEAD_DIM)
    return jnp.transpose(w, (1, 0, 2)).astype(BF16)


def _rope_table(cfg):
    rows = cfg.seq // cfg.grid_w
    r, col = jnp.meshgrid(jnp.arange(rows), jnp.arange(cfg.grid_w), indexing="ij")
    n_freq = ROPE_DIM // 4
    freqs = ROPE_THETA ** (-jnp.arange(n_freq, dtype=F32) / n_freq)
    ang_r = r.reshape(-1)[:, None] * freqs
    ang_c = col.reshape(-1)[:, None] * freqs
    cos = jnp.concatenate([jnp.cos(ang_r)] * 2 + [jnp.cos(ang_c)] * 2, axis=1)
    sin = jnp.concatenate([jnp.sin(ang_r)] * 2 + [jnp.sin(ang_c)] * 2, axis=1)
    lat = jnp.tile(jnp.concatenate([cos, sin], axis=1).astype(F32), (cfg.batch, 1))
    ctx = jnp.concatenate([jnp.ones((cfg.nc, ROPE_DIM), F32), jnp.zeros((cfg.nc, ROPE_DIM), F32)], axis=1)
    return jnp.concatenate([lat, ctx], axis=0)


def _forward(cfg, x, c, ctx, c_ctx, w_mod, b_mod, norm_mix, norm_ffn, w_in, hg_lb_logits, hg_norm,
             pool_w, pool_scale, mla_q_norm, mla_w_uq, mla_kv_norm, mla_w_ukv,
             w_branch_a, w_branch_b, w_branch_c, w_out, peer_wq, peer_keys, peer_u, peer_v, final_w):
    d = cfg.d
    assert cfg.seq % HG_CHUNK == 0 and cfg.ctx % HG_CHUNK == 0 and cfg.nl % cfg.ctx == 0
    assert cfg.off_pool % POOL_WIDTH == 0 and cfg.off_cq % cfg.q_lora == 0
    assert cfg.off_ckv % cfg.kv_lora == 0 and cfg.batch < MOD_ROWS
    tn_in = 768 if d % 256 == 0 and cfg.in_cols > 8192 else 256
    n_cols = -(-cfg.in_cols // tn_in) * tn_in
    tm = _pick(cfg.nc, (1024, 512, 256))

    xs = jnp.concatenate([x.reshape(cfg.nl, d), ctx.reshape(cfg.nc, d)], axis=0)
    c_all = jnp.concatenate([c, c_ctx[None], jnp.zeros((MOD_ROWS - cfg.batch - 1, d), F32)], axis=0)
    mod = adaln_tables(cfg, c_all, w_mod, b_mod)
    cs = _rope_table(cfg)

    for l in range(cfg.depth):
        last = l == cfg.depth - 1
        rows = cfg.nl if last else cfg.nt
        mod_l = mod[l]
        p = norm_matmul(cfg, xs, norm_mix[l], mod_l, _in_proj_weight(cfg, w_in[l], n_cols),
                        k0=0, rows=cfg.nt, tm=tm, tn=tn_in, emit_h=False)
        hg_lat, hg_ctx = hgrn_mixer(cfg, p, hg_lb_logits, hg_norm[l], l)
        pw = pool_w[l].astype(BF16)
        pool_lat = pool_mixer(cfg, p, pw, pool_scale[l], n_seq=cfg.batch, seq_len=cfg.seq, row0=0)
        q = mla_queries(cfg, p, mla_q_norm[l], _mla_q_weight(cfg, mla_w_uq[l]), cs)
        k, v = mla_keys_values(cfg, p, mla_kv_norm[l], _mla_kv_weight(cfg, mla_w_ukv[l]), cs)
        att_lat, att_ctx = mla_attention(cfg, q, k, v, with_ctx=not last)
        if last:
            ya, yb, yc = hg_lat, pool_lat, att_lat
        else:
            pool_ctx = pool_mixer(cfg, p, pw, pool_scale[l], n_seq=cfg.batch, seq_len=cfg.ctx, row0=cfg.nl)
            ya = jnp.concatenate([hg_lat, hg_ctx], axis=0)
            yb = jnp.concatenate([pool_lat, pool_ctx], axis=0)
            yc = jnp.concatenate([att_lat, att_ctx], axis=0)
        m = merge_branches(cfg, ya, yb, yc, p, w_branch_a[l].astype(BF16), w_branch_b[l].astype(BF16),
                           w_branch_c[l].astype(BF16), rows=rows)
        xs = matmul_residual(cfg, m, w_out[l].astype(BF16), xs, mod_l, k=2, rows=rows)
        qh, h2 = norm_matmul(cfg, xs, norm_ffn[l], mod_l, peer_wq[l].astype(BF16),
                             k0=3, rows=rows, tm=tm, tn=_pick(peer_wq.shape[2], (512, 256)), emit_h=True)
        tabs = peer_tables(cfg, qh, peer_keys[l].astype(BF16), rows=rows)
        xs = peer_dense(cfg, h2.T, peer_u[l].astype(BF16), peer_v[l].T.astype(BF16), tabs, xs, mod_l, rows=rows)

    return final_norm(cfg, xs, final_w).reshape(cfg.batch, cfg.seq, d)


def kernel(x, c, ctx, c_ctx, w_mod, b_mod, norm_mix, norm_ffn, w_in, hg_lb_logits, hg_norm, pool_w, pool_scale,
           mla_q_norm, mla_w_uq, mla_kv_norm, mla_w_ukv, w_branch_a, w_branch_b, w_branch_c, w_out,
           peer_wq, peer_keys, peer_u, peer_v, final_norm):
    batch, seq, d = x.shape
    cfg = Cfg(d=d, batch=batch, seq=seq, ctx=ctx.shape[1], grid_w=64, depth=w_mod.shape[0],
              hg_heads=hg_lb_logits.shape[2] // HEAD_DIM,
              mla_heads=mla_w_ukv.shape[2] // (2 * HEAD_DIM), q_lora=mla_q_norm.shape[1],
              kv_lora=mla_kv_norm.shape[1], peer_heads=peer_keys.shape[2])
    return _forward(cfg, x, c, ctx, c_ctx, w_mod, b_mod, norm_mix, norm_ffn, w_in, hg_lb_logits, hg_norm,
                    pool_w, pool_scale, mla_q_norm, mla_w_uq, mla_kv_norm, mla_w_ukv,
                    w_branch_a, w_branch_b, w_branch_c, w_out, peer_wq, peer_keys, peer_u, peer_v, final_norm)
```

```python
import functools
import math
from typing import NamedTuple

import numpy as np
import jax
import jax.numpy as jnp
from jax import lax
from jax.experimental import pallas as pl
from jax.experimental.pallas import tpu as pltpu

F32 = jnp.float32
BF16 = jnp.bfloat16

RMS_EPS = 1e-6
ROPE_THETA = 10000.0
HEAD_DIM = 128
ROPE_DIM = 64
QK_DIM = 256
LOG2_E = float(math.log2(math.e))
QK_LOG2_SCALE = float((HEAD_DIM + ROPE_DIM) ** -0.5) * LOG2_E
POOL_WINDOWS = (2, 4, 8, 16)
POOL_GROUP = 256
POOL_WIDTH = POOL_GROUP * len(POOL_WINDOWS)
POOL_HALO = 8
PEER_NKEYS = 128
PEER_TOPK = 16
HG_CHUNK = 128
HG_LEVELS = 7
UNRANKED = 99.0
MOD_ROWS = 8
VMEM_LIMIT = 56 * 1024 * 1024


class Cfg(NamedTuple):
    d: int
    batch: int
    seq: int
    ctx: int
    grid_w: int
    depth: int
    hg_heads: int
    mla_heads: int
    q_lora: int
    kv_lora: int
    peer_heads: int

    @property
    def nl(self):
        return self.batch * self.seq

    @property
    def nc(self):
        return self.batch * self.ctx

    @property
    def nt(self):
        return self.nl + self.nc

    @property
    def hgw(self):
        return self.hg_heads * HEAD_DIM

    @property
    def off_gate(self):
        return 0

    @property
    def off_hg(self):
        return 3 * self.d

    @property
    def off_pool(self):
        return self.off_hg + 5 * self.hgw

    @property
    def off_cq(self):
        return self.off_pool + POOL_WIDTH

    @property
    def off_ckv(self):
        return self.off_cq + self.q_lora

    @property
    def off_slab(self):
        return self.off_ckv + self.kv_lora

    @property
    def in_cols(self):
        return self.off_slab + HEAD_DIM


def _params(sem):
    return pltpu.CompilerParams(dimension_semantics=sem, vmem_limit_bytes=VMEM_LIMIT)


def _dot(a, b):
    return jnp.dot(a, b, preferred_element_type=F32)


def _dot_nt(a, b):
    return lax.dot_general(a, b, (((1,), (1,)), ((), ())), preferred_element_type=F32)


def _sigmoid(x):
    return 1.0 / (1.0 + jnp.exp(-x))


def _pick(n, prefs):
    for p in prefs:
        if n % p == 0:
            return p
    raise ValueError(f"no tile for {n} in {prefs}")


def _mod_row_map(cfg, tm):
    n_lat = cfg.nl // tm
    per_batch = cfg.seq // tm
    return lambda i: jnp.where(i < n_lat, i // per_batch, cfg.batch)


def _mod_kernel(c_ref, w_ref, b_ref, o_ref):
    c = c_ref[...]
    s = (c * _sigmoid(c)).astype(BF16)
    o_ref[...] = _dot(s, w_ref[...].astype(BF16)) + b_ref[...]


def adaln_tables(cfg, c_all, w_mod, b_mod):
    d6 = 6 * cfg.d
    tn = _pick(d6, (1024, 768, 512, 256))
    out = pl.pallas_call(
        _mod_kernel,
        grid=(cfg.depth, d6 // tn),
        in_specs=[
            pl.BlockSpec((MOD_ROWS, cfg.d), lambda l, j: (0, 0)),
            pl.BlockSpec((None, cfg.d, tn), lambda l, j: (l, 0, j)),
            pl.BlockSpec((None, 1, tn), lambda l, j: (l, 0, j)),
        ],
        out_specs=pl.BlockSpec((None, MOD_ROWS, tn), lambda l, j: (l, 0, j)),
        out_shape=jax.ShapeDtypeStruct((cfg.depth, MOD_ROWS, d6), F32),
        compiler_params=_params(("arbitrary", "arbitrary")),
        name="adaln_tables",
    )(c_all, w_mod, b_mod.reshape(cfg.depth, 1, d6))
    return out.reshape(cfg.depth, MOD_ROWS, 6, cfg.d)


def _norm_matmul_kernel(x_ref, gain_ref, mod_ref, w_ref, *rest, k0, emit_h):
    if emit_h:
        o_ref, hout_ref, h_ref = rest
    else:
        o_ref, h_ref = rest

    @pl.when(pl.program_id(1) == 0)
    def _():
        x = x_ref[...]
        y = x * lax.rsqrt(jnp.mean(x * x, axis=-1, keepdims=True) + RMS_EPS) * gain_ref[...]
        h = (y * (1.0 + mod_ref[k0 + 1:k0 + 2, :]) + mod_ref[k0:k0 + 1, :]).astype(BF16)
        h_ref[...] = h
        if emit_h:
            hout_ref[...] = h

    o_ref[...] = _dot(h_ref[...], w_ref[...])


def norm_matmul(cfg, x, gain, mod_l, w, *, k0, rows, tm, tn, emit_h):
    n = w.shape[1]
    row_of = _mod_row_map(cfg, tm)
    out_shape = [jax.ShapeDtypeStruct((rows, n), F32)]
    out_specs = [pl.BlockSpec((tm, tn), lambda i, j: (i, j))]
    if emit_h:
        out_shape.append(jax.ShapeDtypeStruct((rows, cfg.d), BF16))
        out_specs.append(pl.BlockSpec((tm, cfg.d), lambda i, j: (i, 0)))
    res = pl.pallas_call(
        functools.partial(_norm_matmul_kernel, k0=k0, emit_h=emit_h),
        grid=(rows // tm, n // tn),
        in_specs=[
            pl.BlockSpec((tm, cfg.d), lambda i, j: (i, 0)),
            pl.BlockSpec((1, cfg.d), lambda i, j: (0, 0)),
            pl.BlockSpec((None, 6, cfg.d), lambda i, j: (row_of(i), 0, 0)),
            pl.BlockSpec((cfg.d, tn), lambda i, j: (0, j)),
        ],
        out_specs=out_specs,
        out_shape=out_shape,
        scratch_shapes=[pltpu.VMEM((tm, cfg.d), BF16)],
        compiler_params=_params(("arbitrary", "arbitrary")),
        name="norm_matmul_h" if emit_h else "norm_matmul",
    )(x, gain.reshape(1, cfg.d), mod_l, w)
    return res if emit_h else res[0]


def _hgrn_consts():
    c = HG_CHUNK
    out = []
    for rev in (False, True):
        p = np.arange(c) if not rev else c - 1 - np.arange(c)
        pt, pu = p[:, None], p[None, :]
        g = np.zeros((HG_LEVELS + 1, c, c), np.float32)
        up = np.zeros((HG_LEVELS, c, c), np.float32)
        g[0] = pu <= pt
        for l in range(HG_LEVELS):
            m = 1 << l
            blk = p >> (l + 1)
            upper = ((p >> l) & 1) == 1
            mid = (blk * 2 * m + m)[:, None]
            same = blk[:, None] == blk[None, :]
            g_up = same & (pu >= mid) & (pu <= pt)
            g_lo = same & (pu > pt) & (pu < mid)
            g[1 + l] = np.where(upper[:, None], g_up, g_lo)
            up[l] = np.broadcast_to(upper[:, None], (c, c))
        x = pt ^ pu
        lv = np.where(pu < pt, np.floor(np.log2(np.maximum(x, 1))), -1.0).astype(np.float32)
        out.append((jnp.asarray(g.reshape(-1, c), BF16), jnp.asarray(up, F32), jnp.asarray(lv, F32)))
    return out


def _hgrn_chunk(q_raw, v, f_raw, log_lb, log_1mlb, one_m_lb, g_ref, up_ref, lv_ref, st_ref, end_row):
    c = HG_CHUNK
    q = q_raw * _sigmoid(q_raw)
    e = jnp.exp(-jnp.abs(f_raw))
    one_pe = 1.0 + e
    log_sig = jnp.minimum(f_raw, 0.0) - jnp.log(one_pe)
    t = log_1mlb + log_sig
    log_f = jnp.maximum(log_lb, t) + jnp.log(1.0 + jnp.exp(-jnp.abs(log_lb - t)))
    k = one_m_lb * jnp.where(f_raw >= 0.0, e, 1.0) / one_pe
    hi = log_f.astype(BF16)
    lo = (log_f - hi.astype(F32)).astype(BF16)
    a2 = _dot(g_ref[...], jnp.concatenate([hi, lo], axis=1))
    a = a2[:, :HEAD_DIM] + a2[:, HEAD_DIM:]
    b = a[0:c]
    lv = lv_ref[...]
    scores = jnp.zeros((c, c), F32)
    for l in range(HG_LEVELS):
        e_l = jnp.exp(a[(1 + l) * c:(2 + l) * c])
        x = (jnp.where(up_ref[l] > 0.5, q, k) * e_l).astype(BF16)
        scores = jnp.where(lv == float(l), _dot_nt(x, x), scores)
    b_end = b[end_row:end_row + 1, :]
    qb = (q * jnp.exp(b)).astype(BF16)
    kd = (k * jnp.exp(b_end - b)).astype(BF16)
    st = st_ref[...]
    vb = v.astype(BF16)
    o = (_dot(scores.astype(BF16), vb) + _dot_nt(qb, st.astype(BF16))
         + jnp.sum(q * k, axis=-1, keepdims=True) * v)
    st_ref[...] = st * jnp.exp(b_end) + _dot(v.T.astype(BF16), kd)
    return o


def _hgrn_kernel(ql, ffl, fbl, il, gl, qc, ffc, fbc, ic, gc, logit_ref, nw_ref,
                 gf_ref, upf_ref, lvf_ref, gb_ref, upb_ref, lvb_ref,
                 yl_ref, yc_ref, ofl, obl, ofc, obc, stf, stb, *, layer, seq, ctx):
    c = HG_CHUNK
    depth = logit_ref.shape[0]
    lg = [logit_ref[dd] for dd in range(depth)]
    mx = functools.reduce(jnp.maximum, lg)
    ex = [jnp.exp(v - mx) for v in lg]
    tot = functools.reduce(jnp.add, ex)
    cum = [ex[0] / tot]
    for dd in range(1, layer + 1):
        cum.append(cum[-1] + ex[dd] / tot)
    lb = cum[layer] - cum[0]
    log_lb = jnp.log(lb)
    log_1mlb = jnp.log1p(-lb)
    one_m_lb = 1.0 - lb

    stf[...] = jnp.zeros_like(stf)
    stb[...] = jnp.zeros_like(stb)

    def segment(q_ref, ff_ref, fb_ref, i_ref, of_ref, ob_ref, n):
        def body(j, carry):
            rf = pl.multiple_of(j * c, c)
            rb = pl.multiple_of((n - 1 - j) * c, c)
            of_ref[pl.ds(rf, c), :] = _hgrn_chunk(
                q_ref[pl.ds(rf, c), :], i_ref[pl.ds(rf, c), :], ff_ref[pl.ds(rf, c), :],
                log_lb[0:1], log_1mlb[0:1], one_m_lb[0:1], gf_ref, upf_ref, lvf_ref, stf, c - 1)
            ob_ref[pl.ds(rb, c), :] = _hgrn_chunk(
                q_ref[pl.ds(rb, c), :], i_ref[pl.ds(rb, c), :], fb_ref[pl.ds(rb, c), :],
                log_lb[1:2], log_1mlb[1:2], one_m_lb[1:2], gb_ref, upb_ref, lvb_ref, stb, 0)
            return carry
        lax.fori_loop(0, n, body, 0, unroll=4 if n % 4 == 0 else 2)

    segment(qc, ffc, fbc, ic, ofc, obc, ctx // c)
    segment(ql, ffl, fbl, il, ofl, obl, seq // c)

    nw = nw_ref[...]

    def readout(of_ref, ob_ref, g_ref, y_ref, n):
        def body(j, carry):
            r = pl.multiple_of(j * c, c)
            o = of_ref[pl.ds(r, c), :] + ob_ref[pl.ds(r, c), :]
            y = o * lax.rsqrt(jnp.mean(o * o, axis=-1, keepdims=True) + RMS_EPS) * nw
            g = g_ref[pl.ds(r, c), :]
            y_ref[pl.ds(r, c), :] = (y * (g * _sigmoid(g))).astype(y_ref.dtype)
            return carry
        lax.fori_loop(0, n, body, 0)

    readout(ofc, obc, gc, yc_ref, ctx // c)
    readout(ofl, obl, gl, yl_ref, seq // c)


def hgrn_mixer(cfg, p, logits, norm_w, layer):
    hd = HEAD_DIM
    nh = cfg.hg_heads
    col0 = cfg.off_hg // hd
    ctx_blk0 = cfg.nl // cfg.ctx
    (gf, upf, lvf), (gb, upb, lvb) = _hgrn_consts()

    def lat_spec(part):
        return pl.BlockSpec((cfg.seq, hd), lambda b, h, part=part: (b, col0 + part * nh + h))

    def ctx_spec(part):
        return pl.BlockSpec((cfg.ctx, hd), lambda b, h, part=part: (ctx_blk0 + b, col0 + part * nh + h))

    parts = (0, 1, 2, 3, 4)

    def const(arr):
        return pl.BlockSpec(arr.shape, lambda b, h, nd=arr.ndim: (0,) * nd)

    y_lat, y_ctx = pl.pallas_call(
        functools.partial(_hgrn_kernel, layer=layer, seq=cfg.seq, ctx=cfg.ctx),
        grid=(cfg.batch, nh),
        in_specs=[lat_spec(k) for k in parts] + [ctx_spec(k) for k in parts] + [
            pl.BlockSpec((cfg.depth, 2, hd), lambda b, h: (0, 0, h)),
            pl.BlockSpec((1, hd), lambda b, h: (0, 0)),
            const(gf), const(upf), const(lvf), const(gb), const(upb), const(lvb),
        ],
        out_specs=[
            pl.BlockSpec((cfg.seq, hd), lambda b, h: (b, h)),
            pl.BlockSpec((cfg.ctx, hd), lambda b, h: (b, h)),
        ],
        out_shape=[
            jax.ShapeDtypeStruct((cfg.nl, cfg.hgw), BF16),
            jax.ShapeDtypeStruct((cfg.nc, cfg.hgw), BF16),
        ],
        scratch_shapes=[
            pltpu.VMEM((cfg.seq, hd), F32), pltpu.VMEM((cfg.seq, hd), F32),
            pltpu.VMEM((cfg.ctx, hd), F32), pltpu.VMEM((cfg.ctx, hd), F32),
            pltpu.VMEM((hd, hd), F32), pltpu.VMEM((hd, hd), F32),
        ],
        compiler_params=_params(("arbitrary", "arbitrary")),
        name="hgrn_mixer",
    )(*([p] * 10), logits, norm_w.reshape(1, hd), gf, upf, lvf, gb, upb, lvb)
    return y_lat, y_ctx


def _pool_kernel(prev_ref, cur_ref, next_ref, w_ref, scale_ref, o_ref, buf_ref, *, seq_len, tile, n_tiles):
    i = pl.program_id(1)
    h = POOL_HALO
    cur = cur_ref[...]
    buf_ref[0:h, :] = jnp.where(i > 0, prev_ref[...], 0.0)
    buf_ref[h:h + tile, :] = cur
    buf_ref[h + tile:2 * h + tile, :] = jnp.where(i < n_tiles - 1, next_ref[...], 0.0)
    pos = i * tile + lax.broadcasted_iota(jnp.int32, (tile, POOL_GROUP), 0)
    for gi, win in enumerate(POOL_WINDOWS):
        half = win // 2
        cols = slice(gi * POOL_GROUP, (gi + 1) * POOL_GROUP)
        acc = buf_ref[h - half:h - half + tile, cols]
        for dlt in range(-half + 1, half):
            acc = acc + buf_ref[h + dlt:h + dlt + tile, cols]
        cnt = (jnp.minimum(pos + half, seq_len) - jnp.maximum(pos - half, 0)).astype(F32)
        pooled = acc / cnt - cur[:, cols]
        mixed = _dot(pooled.astype(BF16), w_ref[gi])
        o_ref[:, cols] = (mixed * scale_ref[:, cols]).astype(o_ref.dtype)


def pool_mixer(cfg, p, w_pool, scale, *, n_seq, seq_len, row0):
    tile = _pick(seq_len, (512, 256, 128))
    n_tiles = seq_len // tile
    blk0 = row0 // tile
    col = cfg.off_pool // POOL_WIDTH
    per8 = tile // POOL_HALO
    last8 = cfg.nt // POOL_HALO - 1

    def cur_map(s, i):
        return (blk0 + s * n_tiles + i, col)

    def prev_map(s, i):
        return (jnp.maximum((blk0 + s * n_tiles + i) * per8 - 1, 0), col)

    def next_map(s, i):
        return (jnp.minimum((blk0 + s * n_tiles + i + 1) * per8, last8), col)

    return pl.pallas_call(
        functools.partial(_pool_kernel, seq_len=seq_len, tile=tile, n_tiles=n_tiles),
        grid=(n_seq, n_tiles),
        in_specs=[
            pl.BlockSpec((POOL_HALO, POOL_WIDTH), prev_map),
            pl.BlockSpec((tile, POOL_WIDTH), cur_map),
            pl.BlockSpec((POOL_HALO, POOL_WIDTH), next_map),
            pl.BlockSpec((len(POOL_WINDOWS), POOL_GROUP, POOL_GROUP), lambda s, i: (0, 0, 0)),
            pl.BlockSpec((1, POOL_WIDTH), lambda s, i: (0, 0)),
        ],
        out_specs=pl.BlockSpec((tile, POOL_WIDTH), lambda s, i: (s * n_tiles + i, 0)),
        out_shape=jax.ShapeDtypeStruct((n_seq * seq_len, POOL_WIDTH), BF16),
        scratch_shapes=[pltpu.VMEM((tile + 2 * POOL_HALO, POOL_WIDTH), F32)],
        compiler_params=_params(("arbitrary", "arbitrary")),
        name="pool_mixer",
    )(p, p, p, w_pool, scale.reshape(1, POOL_WIDTH))


def _rope_slab(slab, cs):
    t = slab * cs
    r = t + pltpu.roll(t, ROPE_DIM, axis=1)
    lane = lax.broadcasted_iota(jnp.int32, r.shape, 1)
    return jnp.where(lane < ROPE_DIM, r, 0.0)


def _rms_bf16(x, w):
    return (x * lax.rsqrt(jnp.mean(x * x, axis=-1, keepdims=True) + RMS_EPS) * w).astype(BF16)


def _mla_q_kernel(cq_ref, nw_ref, cs_ref, w_ref, o_ref, n_ref):
    @pl.when(pl.program_id(1) == 0)
    def _():
        n_ref[...] = _rms_bf16(cq_ref[...], nw_ref[...])

    y = _dot(n_ref[...], w_ref[...])
    rope = _rope_slab(y[:, HEAD_DIM:], cs_ref[...])
    o_ref[...] = (jnp.concatenate([y[:, :HEAD_DIM], rope], axis=1) * QK_LOG2_SCALE).astype(o_ref.dtype)


def mla_queries(cfg, p, norm_w, w_q, cs):
    tm = _pick(cfg.nc, (1024, 512, 256))
    nh = cfg.mla_heads
    col = cfg.off_cq // cfg.q_lora
    return pl.pallas_call(
        _mla_q_kernel,
        grid=(cfg.nt // tm, nh),
        in_specs=[
            pl.BlockSpec((tm, cfg.q_lora), lambda i, h: (i, col)),
            pl.BlockSpec((1, cfg.q_lora), lambda i, h: (0, 0)),
            pl.BlockSpec((tm, HEAD_DIM), lambda i, h: (i, 0)),
            pl.BlockSpec((None, cfg.q_lora, QK_DIM), lambda i, h: (h, 0, 0)),
        ],
        out_specs=pl.BlockSpec((None, tm, QK_DIM), lambda i, h: (h, i, 0)),
        out_shape=jax.ShapeDtypeStruct((nh, cfg.nt, QK_DIM), BF16),
        scratch_shapes=[pltpu.VMEM((tm, cfg.q_lora), BF16)],
        compiler_params=_params(("arbitrary", "arbitrary")),
        name="mla_queries",
    )(p, norm_w.reshape(1, cfg.q_lora), cs, w_q)


def _mla_kv_kernel(ckv_ref, slab_ref, nw_ref, cs_ref, w_ref, k_ref, v_ref, n_ref, r_ref):
    @pl.when(pl.program_id(1) == 0)
    def _():
        n_ref[...] = _rms_bf16(ckv_ref[...], nw_ref[...])
        r_ref[...] = _rope_slab(slab_ref[...], cs_ref[...])

    y = _dot(n_ref[...], w_ref[...])
    k_ref[...] = jnp.concatenate([y[:, :HEAD_DIM], r_ref[...]], axis=1).astype(k_ref.dtype)
    lane = lax.broadcasted_iota(jnp.int32, (y.shape[0], HEAD_DIM), 1)
    ones_col = jnp.where(lane == 0, 1.0, 0.0)
    v_ref[...] = jnp.concatenate([y[:, HEAD_DIM:], ones_col], axis=1).astype(v_ref.dtype)


def mla_keys_values(cfg, p, norm_w, w_kv, cs):
    tm = _pick(cfg.nc, (1024, 512, 256))
    nh = cfg.mla_heads
    col = cfg.off_ckv // cfg.kv_lora
    slab_col = cfg.off_slab // HEAD_DIM
    return pl.pallas_call(
        _mla_kv_kernel,
        grid=(cfg.nt // tm, nh),
        in_specs=[
            pl.BlockSpec((tm, cfg.kv_lora), lambda i, h: (i, col)),
            pl.BlockSpec((tm, HEAD_DIM), lambda i, h: (i, slab_col)),
            pl.BlockSpec((1, cfg.kv_lora), lambda i, h: (0, 0)),
            pl.BlockSpec((tm, HEAD_DIM), lambda i, h: (i, 0)),
            pl.BlockSpec((None, cfg.kv_lora, 2 * HEAD_DIM), lambda i, h: (h, 0, 0)),
        ],
        out_specs=[
            pl.BlockSpec((None, tm, QK_DIM), lambda i, h: (h, i, 0)),
            pl.BlockSpec((None, tm, 2 * HEAD_DIM), lambda i, h: (h, i, 0)),
        ],
        out_shape=[
            jax.ShapeDtypeStruct((nh, cfg.nt, QK_DIM), BF16),
            jax.ShapeDtypeStruct((nh, cfg.nt, 2 * HEAD_DIM), BF16),
        ],
        scratch_shapes=[pltpu.VMEM((tm, cfg.kv_lora), BF16), pltpu.VMEM((tm, HEAD_DIM), F32)],
        compiler_params=_params(("arbitrary", "arbitrary")),
        name="mla_keys_values",
    )(p, p, norm_w.reshape(1, cfg.kv_lora), cs, w_kv)


ATTN_SUB = 256
ATTN_KT = 512


def _softmax_pv(s_list, v_list):
    m = functools.reduce(jnp.maximum, [jnp.max(s, axis=-1, keepdims=True) for s in s_list])
    o = None
    for s, v in zip(s_list, v_list):
        part = _dot(jnp.exp2(s - m).astype(BF16), v)
        o = part if o is None else o + part
    return o[:, :HEAD_DIM] / o[:, HEAD_DIM:HEAD_DIM + 1]


def _attn_lat_kernel(q_ref, kl_ref, kc_ref, vl_ref, vc_ref, o_ref, s_ref):
    n_keys = kl_ref.shape[0]
    for r in range(0, q_ref.shape[0], ATTN_SUB):
        rows = slice(r, r + ATTN_SUB)
        q = q_ref[rows, :]
        sc = _dot_nt(q, kc_ref[...])
        m = jnp.max(sc, axis=-1, keepdims=True)
        for c in range(0, n_keys, ATTN_KT):
            s = _dot_nt(q, kl_ref[c:c + ATTN_KT, :])
            s_ref[rows, c:c + ATTN_KT] = s
            m = jnp.maximum(m, jnp.max(s, axis=-1, keepdims=True))
        o = _dot(jnp.exp2(sc - m).astype(BF16), vc_ref[...])
        for c in range(0, n_keys, ATTN_KT):
            p = jnp.exp2((s_ref[rows, c:c + ATTN_KT] - m).astype(BF16))
            o = o + _dot(p, vl_ref[c:c + ATTN_KT, :])
        o_ref[rows, :] = (o[:, :HEAD_DIM] / o[:, HEAD_DIM:HEAD_DIM + 1]).astype(o_ref.dtype)


def _attn_ctx_kernel(q_ref, kc_ref, vc_ref, o_ref):
    o_ref[...] = _softmax_pv([_dot_nt(q_ref[...], kc_ref[...])], [vc_ref[...]]).astype(o_ref.dtype)


def mla_attention(cfg, q, k, v, *, with_ctx):
    nh = cfg.mla_heads
    vw = 2 * HEAD_DIM
    tq = _pick(cfg.seq, (1024, 512, 256))
    nq = cfg.seq // tq
    cb0 = cfg.nl // cfg.ctx
    y_lat = pl.pallas_call(
        _attn_lat_kernel,
        grid=(cfg.batch, nh, nq),
        in_specs=[
            pl.BlockSpec((None, tq, QK_DIM), lambda b, h, i: (h, b * nq + i, 0)),
            pl.BlockSpec((None, cfg.seq, QK_DIM), lambda b, h, i: (h, b, 0)),
            pl.BlockSpec((None, cfg.ctx, QK_DIM), lambda b, h, i: (h, cb0 + b, 0)),
            pl.BlockSpec((None, cfg.seq, vw), lambda b, h, i: (h, b, 0)),
            pl.BlockSpec((None, cfg.ctx, vw), lambda b, h, i: (h, cb0 + b, 0)),
        ],
        out_specs=pl.BlockSpec((tq, HEAD_DIM), lambda b, h, i: (b * nq + i, h)),
        out_shape=jax.ShapeDtypeStruct((cfg.nl, nh * HEAD_DIM), BF16),
        scratch_shapes=[pltpu.VMEM((tq, cfg.seq), F32)],
        compiler_params=_params(("arbitrary", "arbitrary", "arbitrary")),
        name="mla_attention",
    )(q, k, k, v, v)
    if not with_ctx:
        return y_lat, None
    y_ctx = pl.pallas_call(
        _attn_ctx_kernel,
        grid=(cfg.batch, nh),
        in_specs=[
            pl.BlockSpec((None, cfg.ctx, QK_DIM), lambda b, h: (h, cb0 + b, 0)),
            pl.BlockSpec((None, cfg.ctx, QK_DIM), lambda b, h: (h, cb0 + b, 0)),
            pl.BlockSpec((None, cfg.ctx, vw), lambda b, h: (h, cb0 + b, 0)),
        ],
        out_specs=pl.BlockSpec((cfg.ctx, HEAD_DIM), lambda b, h: (b, h)),
        out_shape=jax.ShapeDtypeStruct((cfg.nc, nh * HEAD_DIM), BF16),
        compiler_params=_params(("arbitrary", "arbitrary")),
        name="mla_attention_ctx",
    )(q, k, v)
    return y_lat, y_ctx


def _merge_kernel(ya_ref, yb_ref, yc_ref, ga_ref, gb_ref, gc_ref, wa_ref, wb_ref, wc_ref, o_ref):
    m = (_sigmoid(ga_ref[...]) * _dot(ya_ref[...], wa_ref[...])
         + _sigmoid(gb_ref[...]) * _dot(yb_ref[...], wb_ref[...])
         + _sigmoid(gc_ref[...]) * _dot(yc_ref[...], wc_ref[...]))
    o_ref[...] = m.astype(o_ref.dtype)


def merge_branches(cfg, ya, yb, yc, p, wa, wb, wc, *, rows):
    tm = _pick(cfg.nc, (512, 256))
    tn = _pick(cfg.d, (512, 256))
    gcols = cfg.d // tn

    def gate_spec(k):
        return pl.BlockSpec((tm, tn), lambda i, j, k=k: (i, k * gcols + j))

    return pl.pallas_call(
        _merge_kernel,
        grid=(rows // tm, cfg.d // tn),
        in_specs=[
            pl.BlockSpec((tm, ya.shape[1]), lambda i, j: (i, 0)),
            pl.BlockSpec((tm, yb.shape[1]), lambda i, j: (i, 0)),
            pl.BlockSpec((tm, yc.shape[1]), lambda i, j: (i, 0)),
            gate_spec(0), gate_spec(1), gate_spec(2),
            pl.BlockSpec((wa.shape[0], tn), lambda i, j: (0, j)),
            pl.BlockSpec((wb.shape[0], tn), lambda i, j: (0, j)),
            pl.BlockSpec((wc.shape[0], tn), lambda i, j: (0, j)),
        ],
        out_specs=pl.BlockSpec((tm, tn), lambda i, j: (i, j)),
        out_shape=jax.ShapeDtypeStruct((rows, cfg.d), BF16),
        compiler_params=_params(("arbitrary", "arbitrary")),
        name="merge_branches",
    )(ya, yb, yc, p, p, p, wa, wb, wc)


def _matmul_resid_kernel(a_ref, w_ref, x_ref, mod_ref, o_ref, *, k):
    o_ref[...] = x_ref[...] + mod_ref[k:k + 1, :] * _dot(a_ref[...], w_ref[...])


def matmul_residual(cfg, a, w, x, mod_l, *, k, rows):
    tm = _pick(cfg.nc, (1024, 512, 256))
    tn = _pick(cfg.d, (512, 256))
    row_of = _mod_row_map(cfg, tm)
    return pl.pallas_call(
        functools.partial(_matmul_resid_kernel, k=k),
        grid=(rows // tm, cfg.d // tn),
        in_specs=[
            pl.BlockSpec((tm, a.shape[1]), lambda i, j: (i, 0)),
            pl.BlockSpec((a.shape[1], tn), lambda i, j: (0, j)),
            pl.BlockSpec((tm, tn), lambda i, j: (i, j)),
            pl.BlockSpec((None, 6, tn), lambda i, j: (row_of(i), 0, j)),
        ],
        out_specs=pl.BlockSpec((tm, tn), lambda i, j: (i, j)),
        out_shape=jax.ShapeDtypeStruct((rows, cfg.d), F32),
        compiler_params=_params(("arbitrary", "arbitrary")),
        name="matmul_residual",
    )(a, w, x, mod_l)


def _top16_rows(s, top_ref):
    n = s.shape[0]
    row = lax.broadcasted_iota(jnp.int32, s.shape, 0).astype(F32)
    rank = jnp.full(s.shape, UNRANKED, F32)
    work = s
    for r in range(PEER_TOPK):
        m = jnp.max(work, axis=0, keepdims=True)
        first = jnp.min(jnp.where(work == m, row, float(n)), axis=0, keepdims=True)
        sel = row == first
        rank = jnp.where(sel, float(r), rank)
        work = jnp.where(sel, -jnp.inf, work)
        top_ref[r:r + 1, :] = m
    return rank


def _peer_tables_kernel(qh_ref, keys_ref, b_ref, e2_ref, c_ref, e1_ref, top1_ref, top2_ref, cnt_ref, *, tile):
    lanes = HEAD_DIM
    k1 = keys_ref[0]
    k2 = keys_ref[1]
    for part in range(tile // lanes):
        rows = slice(part * lanes, (part + 1) * lanes)
        cols = slice(part * lanes, (part + 1) * lanes)
        qh = qh_ref[rows, :].astype(BF16)
        s1 = _dot_nt(k1, qh[:, :lanes])
        s2 = _dot_nt(k2, qh[:, lanes:])
        rank1 = _top16_rows(s1, top1_ref)
        rank2 = _top16_rows(s2, top2_ref)
        t1 = top1_ref[...]
        t2 = top2_ref[...]
        pieces = [t1[0:1] + t2]
        pos = [lax.broadcasted_iota(jnp.int32, (PEER_TOPK, lanes), 0).astype(F32)]
        for a in range(1, 8):
            pieces.append(t1[a:a + 1] + t2[0:8])
            pos.append(lax.broadcasted_iota(jnp.int32, (8, lanes), 0).astype(F32) + float(a * PEER_TOPK))
        pieces.append(t1[8:16] + t2[0:1])
        pos.append((lax.broadcasted_iota(jnp.int32, (8, lanes), 0).astype(F32) + 8.0) * float(PEER_TOPK))
        cand = jnp.concatenate(pieces, axis=0)
        cpos = jnp.concatenate(pos, axis=0)
        a_row = lax.broadcasted_iota(jnp.int32, (PEER_TOPK, lanes), 0).astype(F32)
        cnt = jnp.zeros((PEER_TOPK, lanes), F32)
        z = jnp.zeros((1, lanes), F32)
        best0 = None
        for r in range(PEER_TOPK):
            m = jnp.max(cand, axis=0, keepdims=True)
            first = jnp.min(jnp.where(cand == m, cpos, 1e9), axis=0, keepdims=True)
            cand = jnp.where(cpos == first, -jnp.inf, cand)
            cnt = cnt + jnp.where(a_row == jnp.floor(first * (1.0 / PEER_TOPK)), 1.0, 0.0)
            if r == 0:
                best0 = m
            z = z + jnp.exp(m - best0)
        cnt_ref[...] = cnt
        c_tab = jnp.zeros((PEER_NKEYS, lanes), F32)
        for a in range(PEER_TOPK):
            c_tab = jnp.where(rank1 == float(a), cnt_ref[a:a + 1, :], c_tab)
        b_ref[:, cols] = rank2.astype(b_ref.dtype)
        c_ref[:, cols] = c_tab
        e1_ref[:, cols] = jnp.exp(s1 - t1[0:1]) / z
        e2_ref[:, cols] = jnp.exp(s2 - t2[0:1]).astype(e2_ref.dtype)


def peer_tables(cfg, qh, keys, *, rows):
    tile = 256
    nh = cfg.peer_heads
    shapes = [jax.ShapeDtypeStruct((nh, PEER_NKEYS, rows), dt) for dt in (BF16, BF16, F32, F32)]
    out_spec = pl.BlockSpec((None, PEER_NKEYS, tile), lambda i, h: (h, 0, i))
    return pl.pallas_call(
        functools.partial(_peer_tables_kernel, tile=tile),
        grid=(rows // tile, nh),
        in_specs=[
            pl.BlockSpec((tile, 2 * HEAD_DIM), lambda i, h: (i, h)),
            pl.BlockSpec((2, None, PEER_NKEYS, HEAD_DIM), lambda i, h: (0, h, 0, 0)),
        ],
        out_specs=[out_spec] * 4,
        out_shape=shapes,
        scratch_shapes=[pltpu.VMEM((PEER_TOPK, HEAD_DIM), F32)] * 3,
        compiler_params=_params(("arbitrary", "arbitrary")),
        name="peer_tables",
    )(qh, keys)


PEER_ROWS = 8
PEER_EB = PEER_ROWS * PEER_NKEYS
PEER_DOT_ROWS = 1


def _peer_dense_kernel(ht_ref, u_ref, vt_ref, b_ref, e2_ref, c_ref, e1_ref, x_ref, mod_ref, o_ref,
                       acc_ref, w_ref, *, heads):
    e = pl.program_id(1)

    @pl.when(e == 0)
    def _():
        acc_ref[...] = jnp.zeros_like(acc_ref)

    zero = jnp.zeros((), BF16)
    ht = ht_ref[...]
    for r0 in range(0, PEER_ROWS, PEER_DOT_ROWS):
        blk = slice(r0 * PEER_NKEYS, (r0 + PEER_DOT_ROWS) * PEER_NKEYS)
        act = _dot(u_ref[blk, :], ht)
        gelu = (0.5 * act * (1.0 + lax.erf(act * float(math.sqrt(0.5))))).astype(BF16)
        for q in range(PEER_DOT_ROWS):
            r = r0 + q
            g = jnp.zeros((PEER_NKEYS, act.shape[1]), BF16)
            for h in range(heads):
                crow = jnp.broadcast_to(c_ref[h, r:r + 1, :], g.shape).astype(BF16)
                erow = jnp.broadcast_to(e1_ref[h, r:r + 1, :], g.shape).astype(BF16)
                g = g + jnp.where(b_ref[h] < crow, e2_ref[h] * erow, zero)
            w_ref[r * PEER_NKEYS:(r + 1) * PEER_NKEYS, :] = g * gelu[q * PEER_NKEYS:(q + 1) * PEER_NKEYS]
    acc_ref[...] += _dot(vt_ref[...], w_ref[...])

    @pl.when(e == pl.num_programs(1) - 1)
    def _():
        o_ref[...] = x_ref[...] + mod_ref[5:6, :] * acc_ref[...].T


def peer_dense(cfg, ht, u, vt, tabs, x, mod_l, *, rows):
    tile = _pick(cfg.nc, (512, 256))
    eb = PEER_EB
    n_exp = u.shape[0]
    nh = cfg.peer_heads
    row_of = _mod_row_map(cfg, tile)
    tab_spec = pl.BlockSpec((nh, PEER_NKEYS, tile), lambda i, e: (0, 0, i))
    row_spec = pl.BlockSpec((nh, PEER_ROWS, tile), lambda i, e: (0, e, i))
    return pl.pallas_call(
        functools.partial(_peer_dense_kernel, heads=nh),
        grid=(rows // tile, n_exp // eb),
        in_specs=[
            pl.BlockSpec((cfg.d, tile), lambda i, e: (0, i)),
            pl.BlockSpec((eb, cfg.d), lambda i, e: (e, 0)),
            pl.BlockSpec((cfg.d, eb), lambda i, e: (0, e)),
            tab_spec, tab_spec, row_spec, row_spec,
            pl.BlockSpec((tile, cfg.d), lambda i, e: (i, 0)),
            pl.BlockSpec((None, 6, cfg.d), lambda i, e: (row_of(i), 0, 0)),
        ],
        out_specs=pl.BlockSpec((tile, cfg.d), lambda i, e: (i, 0)),
        out_shape=jax.ShapeDtypeStruct((rows, cfg.d), F32),
        scratch_shapes=[pltpu.VMEM((cfg.d, tile), F32), pltpu.VMEM((eb, tile), BF16)],
        compiler_params=_params(("arbitrary", "arbitrary")),
        name="peer_dense",
    )(ht, u, vt, *tabs, x, mod_l)


def _final_norm_kernel(x_ref, w_ref, o_ref):
    x = x_ref[...]
    o_ref[...] = x * lax.rsqrt(jnp.mean(x * x, axis=-1, keepdims=True) + RMS_EPS) * w_ref[...]


def final_norm(cfg, x, w):
    tm = _pick(cfg.nl, (1024, 512, 256))
    return pl.pallas_call(
        _final_norm_kernel,
        grid=(cfg.nl // tm,),
        in_specs=[pl.BlockSpec((tm, cfg.d), lambda i: (i, 0)), pl.BlockSpec((1, cfg.d), lambda i: (0, 0))],
        out_specs=pl.BlockSpec((tm, cfg.d), lambda i: (i, 0)),
        out_shape=jax.ShapeDtypeStruct((cfg.nl, cfg.d), F32),
        compiler_params=_params(("arbitrary",)),
        name="final_norm",
    )(x, w.reshape(1, cfg.d))


def _rot_cols(w):
    q = ROPE_DIM // 4
    return jnp.concatenate([-w[..., q:2 * q], w[..., 0:q], -w[..., 3 * q:4 * q], w[..., 2 * q:3 * q]], axis=-1)


def _in_proj_weight(cfg, w_in, n_cols):
    d = cfg.d
    hg_end = 5 * cfg.hgw
    pool_end = hg_end + POOL_WIDTH
    cq_end = pool_end + cfg.q_lora
    ckv_end = cq_end + cfg.kv_lora
    rope_end = ckv_end + ROPE_DIM
    k_rope = w_in[:, ckv_end:rope_end]
    parts = [w_in[:, rope_end:rope_end + 3 * d], w_in[:, :ckv_end], k_rope, _rot_cols(k_rope)]
    w = jnp.concatenate(parts, axis=1)
    return jnp.pad(w, ((0, 0), (0, n_cols - w.shape[1]))).astype(BF16)


def _mla_q_weight(cfg, w_uq):
    w = w_uq.reshape(cfg.q_lora, cfg.mla_heads, HEAD_DIM + ROPE_DIM)
    rope = w[..., HEAD_DIM:]
    w = jnp.concatenate([w[..., :HEAD_DIM], rope, _rot_cols(rope)], axis=-1)
    return jnp.transpose(w, (1, 0, 2)).astype(BF16)


def _mla_kv_weight(cfg, w_ukv):
    w = w_ukv.reshape(cfg.kv_lora, cfg.mla_heads, 2 * HEAD_DIM)
    return jnp.transpose(w, (1, 0, 2)).astype(BF16)


def _rope_table(cfg):
    rows = cfg.seq // cfg.grid_w
    r, col = jnp.meshgrid(jnp.arange(rows), jnp.arange(cfg.grid_w), indexing="ij")
    n_freq = ROPE_DIM // 4
    freqs = ROPE_THETA ** (-jnp.arange(n_freq, dtype=F32) / n_freq)
    ang_r = r.reshape(-1)[:, None] * freqs
    ang_c = col.reshape(-1)[:, None] * freqs
    cos = jnp.concatenate([jnp.cos(ang_r)] * 2 + [jnp.cos(ang_c)] * 2, axis=1)
    sin = jnp.concatenate([jnp.sin(ang_r)] * 2 + [jnp.sin(ang_c)] * 2, axis=1)
    lat = jnp.tile(jnp.concatenate([cos, sin], axis=1).astype(F32), (cfg.batch, 1))
    ctx = jnp.concatenate([jnp.ones((cfg.nc, ROPE_DIM), F32), jnp.zeros((cfg.nc, ROPE_DIM), F32)], axis=1)
    return jnp.concatenate([lat, ctx], axis=0)


def _forward(cfg, x, c, ctx, c_ctx, w_mod, b_mod, norm_mix, norm_ffn, w_in, hg_lb_logits, hg_norm,
             pool_w, pool_scale, mla_q_norm, mla_w_uq, mla_kv_norm, mla_w_ukv,
             w_branch_a, w_branch_b, w_branch_c, w_out, peer_wq, peer_keys, peer_u, peer_v, final_w):
    d = cfg.d
    assert cfg.seq % HG_CHUNK == 0 and cfg.ctx % HG_CHUNK == 0 and cfg.nl % cfg.ctx == 0
    assert cfg.off_pool % POOL_WIDTH == 0 and cfg.off_cq % cfg.q_lora == 0
    assert cfg.off_ckv % cfg.kv_lora == 0 and cfg.batch < MOD_ROWS
    tn_in = 768 if d % 256 == 0 and cfg.in_cols > 8192 else 256
    n_cols = -(-cfg.in_cols // tn_in) * tn_in
    tm = _pick(cfg.nc, (1024, 512, 256))

    xs = jnp.concatenate([x.reshape(cfg.nl, d), ctx.reshape(cfg.nc, d)], axis=0)
    c_all = jnp.concatenate([c, c_ctx[None], jnp.zeros((MOD_ROWS - cfg.batch - 1, d), F32)], axis=0)
    mod = adaln_tables(cfg, c_all, w_mod, b_mod)
    cs = _rope_table(cfg)

    for l in range(cfg.depth):
        last = l == cfg.depth - 1
        rows = cfg.nl if last else cfg.nt
        mod_l = mod[l]
        p = norm_matmul(cfg, xs, norm_mix[l], mod_l, _in_proj_weight(cfg, w_in[l], n_cols),
                        k0=0, rows=cfg.nt, tm=tm, tn=tn_in, emit_h=False)
        hg_lat, hg_ctx = hgrn_mixer(cfg, p, hg_lb_logits, hg_norm[l], l)
        pw = pool_w[l].astype(BF16)
        pool_lat = pool_mixer(cfg, p, pw, pool_scale[l], n_seq=cfg.batch, seq_len=cfg.seq, row0=0)
        q = mla_queries(cfg, p, mla_q_norm[l], _mla_q_weight(cfg, mla_w_uq[l]), cs)
        k, v = mla_keys_values(cfg, p, mla_kv_norm[l], _mla_kv_weight(cfg, mla_w_ukv[l]), cs)
        att_lat, att_ctx = mla_attention(cfg, q, k, v, with_ctx=not last)
        if last:
            ya, yb, yc = hg_lat, pool_lat, att_lat
        else:
            pool_ctx = pool_mixer(cfg, p, pw, pool_scale[l], n_seq=cfg.batch, seq_len=cfg.ctx, row0=cfg.nl)
            ya = jnp.concatenate([hg_lat, hg_ctx], axis=0)
            yb = jnp.concatenate([pool_lat, pool_ctx], axis=0)
            yc = jnp.concatenate([att_lat, att_ctx], axis=0)
        m = merge_branches(cfg, ya, yb, yc, p, w_branch_a[l].astype(BF16), w_branch_b[l].astype(BF16),
                           w_branch_c[l].astype(BF16), rows=rows)
        xs = matmul_residual(cfg, m, w_out[l].astype(BF16), xs, mod_l, k=2, rows=rows)
        qh, h2 = norm_matmul(cfg, xs, norm_ffn[l], mod_l, peer_wq[l].astype(BF16),
                             k0=3, rows=rows, tm=tm, tn=_pick(peer_wq.shape[2], (512, 256)), emit_h=True)
        tabs = peer_tables(cfg, qh, peer_keys[l].astype(BF16), rows=rows)
        xs = peer_dense(cfg, h2.T, peer_u[l].astype(BF16), peer_v[l].T.astype(BF16), tabs, xs, mod_l, rows=rows)

    return final_norm(cfg, xs, final_w).reshape(cfg.batch, cfg.seq, d)


def kernel(x, c, ctx, c_ctx, w_mod, b_mod, norm_mix, norm_ffn, w_in, hg_lb_logits, hg_norm, pool_w, pool_scale,
           mla_q_norm, mla_w_uq, mla_kv_norm, mla_w_ukv, w_branch_a, w_branch_b, w_branch_c, w_out,
           peer_wq, peer_keys, peer_u, peer_v, final_norm):
    batch, seq, d = x.shape
    cfg = Cfg(d=d, batch=batch, seq=seq, ctx=ctx.shape[1], grid_w=64, depth=w_mod.shape[0],
              hg_heads=hg_lb_logits.shape[2] // HEAD_DIM,
              mla_heads=mla_w_ukv.shape[2] // (2 * HEAD_DIM), q_lora=mla_q_norm.shape[1],
              kv_lora=mla_kv_norm.shape[1], peer_heads=peer_keys.shape[2])
    return _forward(cfg, x, c, ctx, c_ctx, w_mod, b_mod, norm_mix, norm_ffn, w_in, hg_lb_logits, hg_norm,
                    pool_w, pool_scale, mla_q_norm, mla_w_uq, mla_kv_norm, mla_w_ukv,
                    w_branch_a, w_branch_b, w_branch_c, w_out, peer_wq, peer_keys, peer_u, peer_v, final_norm)
```

```python
import functools
import math
from typing import NamedTuple

import numpy as np
import jax
import jax.numpy as jnp
from jax import lax
from jax.experimental import pallas as pl
from jax.experimental.pallas import tpu as pltpu

F32 = jnp.float32
BF16 = jnp.bfloat16

RMS_EPS = 1e-6
ROPE_THETA = 10000.0
HEAD_DIM = 128
ROPE_DIM = 64
QK_DIM = 256
LOG2_E = float(math.log2(math.e))
QK_LOG2_SCALE = float((HEAD_DIM + ROPE_DIM) ** -0.5) * LOG2_E
POOL_WINDOWS = (2, 4, 8, 16)
POOL_GROUP = 256
POOL_WIDTH = POOL_GROUP * len(POOL_WINDOWS)
POOL_HALO = 8
PEER_NKEYS = 128
PEER_TOPK = 16
HG_CHUNK = 128
HG_LEVELS = 7
UNRANKED = 99.0
MOD_ROWS = 8
VMEM_LIMIT = 56 * 1024 * 1024


class Cfg(NamedTuple):
    d: int
    batch: int
    seq: int
    ctx: int
    grid_w: int
    depth: int
    hg_heads: int
    mla_heads: int
    q_lora: int
    kv_lora: int
    peer_heads: int

    @property
    def nl(self):
        return self.batch * self.seq

    @property
    def nc(self):
        return self.batch * self.ctx

    @property
    def nt(self):
        return self.nl + self.nc

    @property
    def hgw(self):
        return self.hg_heads * HEAD_DIM

    @property
    def off_gate(self):
        return 0

    @property
    def off_hg(self):
        return 3 * self.d

    @property
    def off_pool(self):
        return self.off_hg + 5 * self.hgw

    @property
    def off_cq(self):
        return self.off_pool + POOL_WIDTH

    @property
    def off_ckv(self):
        return self.off_cq + self.q_lora

    @property
    def off_slab(self):
        return self.off_ckv + self.kv_lora

    @property
    def in_cols(self):
        return self.off_slab + HEAD_DIM


def _params(sem):
    return pltpu.CompilerParams(dimension_semantics=sem, vmem_limit_bytes=VMEM_LIMIT)


def _dot(a, b):
    return jnp.dot(a, b, preferred_element_type=F32)


def _dot_nt(a, b):
    return lax.dot_general(a, b, (((1,), (1,)), ((), ())), preferred_element_type=F32)


def _sigmoid(x):
    return 1.0 / (1.0 + jnp.exp(-x))


def _pick(n, prefs):
    for p in prefs:
        if n % p == 0:
            return p
    raise ValueError(f"no tile for {n} in {prefs}")


def _mod_row_map(cfg, tm):
    n_lat = cfg.nl // tm
    per_batch = cfg.seq // tm
    return lambda i: jnp.where(i < n_lat, i // per_batch, cfg.batch)


def _mod_kernel(c_ref, w_ref, b_ref, o_ref):
    c = c_ref[...]
    s = (c * _sigmoid(c)).astype(BF16)
    o_ref[...] = _dot(s, w_ref[...].astype(BF16)) + b_ref[...]


def adaln_tables(cfg, c_all, w_mod, b_mod):
    d6 = 6 * cfg.d
    tn = _pick(d6, (1024, 768, 512, 256))
    out = pl.pallas_call(
        _mod_kernel,
        grid=(cfg.depth, d6 // tn),
        in_specs=[
            pl.BlockSpec((MOD_ROWS, cfg.d), lambda l, j: (0, 0)),
            pl.BlockSpec((None, cfg.d, tn), lambda l, j: (l, 0, j)),
            pl.BlockSpec((None, 1, tn), lambda l, j: (l, 0, j)),
        ],
        out_specs=pl.BlockSpec((None, MOD_ROWS, tn), lambda l, j: (l, 0, j)),
        out_shape=jax.ShapeDtypeStruct((cfg.depth, MOD_ROWS, d6), F32),
        compiler_params=_params(("arbitrary", "arbitrary")),
        name="adaln_tables",
    )(c_all, w_mod, b_mod.reshape(cfg.depth, 1, d6))
    return out.reshape(cfg.depth, MOD_ROWS, 6, cfg.d)


def _norm_matmul_kernel(x_ref, gain_ref, mod_ref, w_ref, *rest, k0, emit_h):
    if emit_h:
        o_ref, hout_ref, h_ref = rest
    else:
        o_ref, h_ref = rest

    @pl.when(pl.program_id(1) == 0)
    def _():
        x = x_ref[...]
        y = x * lax.rsqrt(jnp.mean(x * x, axis=-1, keepdims=True) + RMS_EPS) * gain_ref[...]
        h = (y * (1.0 + mod_ref[k0 + 1:k0 + 2, :]) + mod_ref[k0:k0 + 1, :]).astype(BF16)
        h_ref[...] = h
        if emit_h:
            hout_ref[...] = h

    o_ref[...] = _dot(h_ref[...], w_ref[...])


def norm_matmul(cfg, x, gain, mod_l, w, *, k0, rows, tm, tn, emit_h):
    n = w.shape[1]
    row_of = _mod_row_map(cfg, tm)
    out_shape = [jax.ShapeDtypeStruct((rows, n), F32)]
    out_specs = [pl.BlockSpec((tm, tn), lambda i, j: (i, j))]
    if emit_h:
        out_shape.append(jax.ShapeDtypeStruct((rows, cfg.d), BF16))
        out_specs.append(pl.BlockSpec((tm, cfg.d), lambda i, j: (i, 0)))
    res = pl.pallas_call(
        functools.partial(_norm_matmul_kernel, k0=k0, emit_h=emit_h),
        grid=(rows // tm, n // tn),
        in_specs=[
            pl.BlockSpec((tm, cfg.d), lambda i, j: (i, 0)),
            pl.BlockSpec((1, cfg.d), lambda i, j: (0, 0)),
            pl.BlockSpec((None, 6, cfg.d), lambda i, j: (row_of(i), 0, 0)),
            pl.BlockSpec((cfg.d, tn), lambda i, j: (0, j)),
        ],
        out_specs=out_specs,
        out_shape=out_shape,
        scratch_shapes=[pltpu.VMEM((tm, cfg.d), BF16)],
        compiler_params=_params(("arbitrary", "arbitrary")),
        name="norm_matmul_h" if emit_h else "norm_matmul",
    )(x, gain.reshape(1, cfg.d), mod_l, w)
    return res if emit_h else res[0]


def _hgrn_consts():
    c = HG_CHUNK
    out = []
    for rev in (False, True):
        p = np.arange(c) if not rev else c - 1 - np.arange(c)
        pt, pu = p[:, None], p[None, :]
        g = np.zeros((HG_LEVELS + 1, c, c), np.float32)
        up = np.zeros((HG_LEVELS, c, c), np.float32)
        g[0] = pu <= pt
        for l in range(HG_LEVELS):
            m = 1 << l
            blk = p >> (l + 1)
            upper = ((p >> l) & 1) == 1
            mid = (blk * 2 * m + m)[:, None]
            same = blk[:, None] == blk[None, :]
            g_up = same & (pu >= mid) & (pu <= pt)
            g_lo = same & (pu > pt) & (pu < mid)
            g[1 + l] = np.where(upper[:, None], g_up, g_lo)
            up[l] = np.broadcast_to(upper[:, None], (c, c))
        x = pt ^ pu
        lv = np.where(pu < pt, np.floor(np.log2(np.maximum(x, 1))), -1.0).astype(np.float32)
        out.append((jnp.asarray(g.reshape(-1, c), BF16), jnp.asarray(up, F32), jnp.asarray(lv, F32)))
    return out


def _hgrn_chunk(q_raw, v, f_raw, log_lb, log_1mlb, one_m_lb, g_ref, up_ref, lv_ref, st_ref, end_row):
    c = HG_CHUNK
    q = q_raw * _sigmoid(q_raw)
    e = jnp.exp(-jnp.abs(f_raw))
    one_pe = 1.0 + e
    log_sig = jnp.minimum(f_raw, 0.0) - jnp.log(one_pe)
    t = log_1mlb + log_sig
    log_f = jnp.maximum(log_lb, t) + jnp.log(1.0 + jnp.exp(-jnp.abs(log_lb - t)))
    k = one_m_lb * jnp.where(f_raw >= 0.0, e, 1.0) / one_pe
    hi = log_f.astype(BF16)
    lo = (log_f - hi.astype(F32)).astype(BF16)
    a2 = _dot(g_ref[...], jnp.concatenate([hi, lo], axis=1))
    a = a2[:, :HEAD_DIM] + a2[:, HEAD_DIM:]
    b = a[0:c]
    lv = lv_ref[...]
    scores = jnp.zeros((c, c), F32)
    for l in range(HG_LEVELS):
        e_l = jnp.exp(a[(1 + l) * c:(2 + l) * c])
        x = (jnp.where(up_ref[l] > 0.5, q, k) * e_l).astype(BF16)
        scores = jnp.where(lv == float(l), _dot_nt(x, x), scores)
    b_end = b[end_row:end_row + 1, :]
    qb = (q * jnp.exp(b)).astype(BF16)
    kd = (k * jnp.exp(b_end - b)).astype(BF16)
    st = st_ref[...]
    vb = v.astype(BF16)
    o = (_dot(scores.astype(BF16), vb) + _dot_nt(qb, st.astype(BF16))
         + jnp.sum(q * k, axis=-1, keepdims=True) * v)
    st_ref[...] = st * jnp.exp(b_end) + _dot(v.T.astype(BF16), kd)
    return o


def _hgrn_kernel(ql, ffl, fbl, il, gl, qc, ffc, fbc, ic, gc, logit_ref, nw_ref,
                 gf_ref, upf_ref, lvf_ref, gb_ref, upb_ref, lvb_ref,
                 yl_ref, yc_ref, ofl, obl, ofc, obc, stf, stb, *, layer, seq, ctx):
    c = HG_CHUNK
    depth = logit_ref.shape[0]
    lg = [logit_ref[dd] for dd in range(depth)]
    mx = functools.reduce(jnp.maximum, lg)
    ex = [jnp.exp(v - mx) for v in lg]
    tot = functools.reduce(jnp.add, ex)
    cum = [ex[0] / tot]
    for dd in range(1, layer + 1):
        cum.append(cum[-1] + ex[dd] / tot)
    lb = cum[layer] - cum[0]
    log_lb = jnp.log(lb)
    log_1mlb = jnp.log1p(-lb)
    one_m_lb = 1.0 - lb

    stf[...] = jnp.zeros_like(stf)
    stb[...] = jnp.zeros_like(stb)

    def segment(q_ref, ff_ref, fb_ref, i_ref, of_ref, ob_ref, n):
        def body(j, carry):
            rf = pl.multiple_of(j * c, c)
            rb = pl.multiple_of((n - 1 - j) * c, c)
            of_ref[pl.ds(rf, c), :] = _hgrn_chunk(
                q_ref[pl.ds(rf, c), :], i_ref[pl.ds(rf, c), :], ff_ref[pl.ds(rf, c), :],
                log_lb[0:1], log_1mlb[0:1], one_m_lb[0:1], gf_ref, upf_ref, lvf_ref, stf, c - 1)
            ob_ref[pl.ds(rb, c), :] = _hgrn_chunk(
                q_ref[pl.ds(rb, c), :], i_ref[pl.ds(rb, c), :], fb_ref[pl.ds(rb, c), :],
                log_lb[1:2], log_1mlb[1:2], one_m_lb[1:2], gb_ref, upb_ref, lvb_ref, stb, 0)
            return carry
        lax.fori_loop(0, n, body, 0, unroll=4 if n % 4 == 0 else 2)

    segment(qc, ffc, fbc, ic, ofc, obc, ctx // c)
    segment(ql, ffl, fbl, il, ofl, obl, seq // c)

    nw = nw_ref[...]

    def readout(of_ref, ob_ref, g_ref, y_ref, n):
        def body(j, carry):
            r = pl.multiple_of(j * c, c)
            o = of_ref[pl.ds(r, c), :] + ob_ref[pl.ds(r, c), :]
            y = o * lax.rsqrt(jnp.mean(o * o, axis=-1, keepdims=True) + RMS_EPS) * nw
            g = g_ref[pl.ds(r, c), :]
            y_ref[pl.ds(r, c), :] = (y * (g * _sigmoid(g))).astype(y_ref.dtype)
            return carry
        lax.fori_loop(0, n, body, 0)

    readout(ofc, obc, gc, yc_ref, ctx // c)
    readout(ofl, obl, gl, yl_ref, seq // c)


def hgrn_mixer(cfg, p, logits, norm_w, layer, out_rows):
    hd = HEAD_DIM
    nh = cfg.hg_heads
    col0 = cfg.off_hg // hd
    ctx_blk0 = cfg.nl // cfg.ctx
    (gf, upf, lvf), (gb, upb, lvb) = _hgrn_consts()

    def lat_spec(part):
        return pl.BlockSpec((cfg.seq, hd), lambda b, h, part=part: (b, col0 + part * nh + h))

    def ctx_spec(part):
        return pl.BlockSpec((cfg.ctx, hd), lambda b, h, part=part: (ctx_blk0 + b, col0 + part * nh + h))

    parts = (0, 1, 2, 3, 4)

    def const(arr):
        return pl.BlockSpec(arr.shape, lambda b, h, nd=arr.ndim: (0,) * nd)

    y_lat, y_ctx = pl.pallas_call(
        functools.partial(_hgrn_kernel, layer=layer, seq=cfg.seq, ctx=cfg.ctx),
        grid=(cfg.batch, nh),
        in_specs=[lat_spec(k) for k in parts] + [ctx_spec(k) for k in parts] + [
            pl.BlockSpec((cfg.depth, 2, hd), lambda b, h: (0, 0, h)),
            pl.BlockSpec((1, hd), lambda b, h: (0, 0)),
            const(gf), const(upf), const(lvf), const(gb), const(upb), const(lvb),
        ],
        out_specs=[
            pl.BlockSpec((cfg.seq, hd), lambda b, h: (b, h)),
            pl.BlockSpec((cfg.ctx, hd), lambda b, h: (b, h)),
        ],
        out_shape=[
            jax.ShapeDtypeStruct((out_rows, cfg.hgw), BF16),
            jax.ShapeDtypeStruct((cfg.nc, cfg.hgw), BF16),
        ],
        scratch_shapes=[
            pltpu.VMEM((cfg.seq, hd), F32), pltpu.VMEM((cfg.seq, hd), F32),
            pltpu.VMEM((cfg.ctx, hd), F32), pltpu.VMEM((cfg.ctx, hd), F32),
            pltpu.VMEM((hd, hd), F32), pltpu.VMEM((hd, hd), F32),
        ],
        compiler_params=_params(("arbitrary", "arbitrary")),
        name="hgrn_mixer",
    )(*([p] * 10), logits, norm_w.reshape(1, hd), gf, upf, lvf, gb, upb, lvb)
    return y_lat, y_ctx


def _pool_kernel(prev_ref, cur_ref, next_ref, w_ref, scale_ref, o_ref, buf_ref, *, seq_len, tile, n_tiles):
    i = pl.program_id(1)
    h = POOL_HALO
    cur = cur_ref[...]
    buf_ref[0:h, :] = jnp.where(i > 0, prev_ref[...], 0.0)
    buf_ref[h:h + tile, :] = cur
    buf_ref[h + tile:2 * h + tile, :] = jnp.where(i < n_tiles - 1, next_ref[...], 0.0)
    pos = i * tile + lax.broadcasted_iota(jnp.int32, (tile, POOL_GROUP), 0)
    for gi, win in enumerate(POOL_WINDOWS):
        half = win // 2
        cols = slice(gi * POOL_GROUP, (gi + 1) * POOL_GROUP)
        acc = buf_ref[h - half:h - half + tile, cols]
        for dlt in range(-half + 1, half):
            acc = acc + buf_ref[h + dlt:h + dlt + tile, cols]
        cnt = (jnp.minimum(pos + half, seq_len) - jnp.maximum(pos - half, 0)).astype(F32)
        pooled = acc / cnt - cur[:, cols]
        mixed = _dot(pooled.astype(BF16), w_ref[gi])
        o_ref[:, cols] = (mixed * scale_ref[:, cols]).astype(o_ref.dtype)


def pool_mixer(cfg, p, w_pool, scale, *, n_seq, seq_len, row0, out_rows=None):
    out_rows = n_seq * seq_len if out_rows is None else out_rows
    tile = _pick(seq_len, (512, 256, 128))
    n_tiles = seq_len // tile
    blk0 = row0 // tile
    col = cfg.off_pool // POOL_WIDTH
    per8 = tile // POOL_HALO
    last8 = cfg.nt // POOL_HALO - 1

    def cur_map(s, i):
        return (blk0 + s * n_tiles + i, col)

    def prev_map(s, i):
        return (jnp.maximum((blk0 + s * n_tiles + i) * per8 - 1, 0), col)

    def next_map(s, i):
        return (jnp.minimum((blk0 + s * n_tiles + i + 1) * per8, last8), col)

    return pl.pallas_call(
        functools.partial(_pool_kernel, seq_len=seq_len, tile=tile, n_tiles=n_tiles),
        grid=(n_seq, n_tiles),
        in_specs=[
            pl.BlockSpec((POOL_HALO, POOL_WIDTH), prev_map),
            pl.BlockSpec((tile, POOL_WIDTH), cur_map),
            pl.BlockSpec((POOL_HALO, POOL_WIDTH), next_map),
            pl.BlockSpec((len(POOL_WINDOWS), POOL_GROUP, POOL_GROUP), lambda s, i: (0, 0, 0)),
            pl.BlockSpec((1, POOL_WIDTH), lambda s, i: (0, 0)),
        ],
        out_specs=pl.BlockSpec((tile, POOL_WIDTH), lambda s, i: (s * n_tiles + i, 0)),
        out_shape=jax.ShapeDtypeStruct((out_rows, POOL_WIDTH), BF16),
        scratch_shapes=[pltpu.VMEM((tile + 2 * POOL_HALO, POOL_WIDTH), F32)],
        compiler_params=_params(("arbitrary", "arbitrary")),
        name="pool_mixer",
    )(p, p, p, w_pool, scale.reshape(1, POOL_WIDTH))


def _rope_slab(slab, cs):
    t = slab * cs
    r = t + pltpu.roll(t, ROPE_DIM, axis=1)
    lane = lax.broadcasted_iota(jnp.int32, r.shape, 1)
    return jnp.where(lane < ROPE_DIM, r, 0.0)


def _rms_bf16(x, w):
    return (x * lax.rsqrt(jnp.mean(x * x, axis=-1, keepdims=True) + RMS_EPS) * w).astype(BF16)


def _mla_q_kernel(cq_ref, nw_ref, cs_ref, w_ref, o_ref):
    n = _rms_bf16(cq_ref[...], nw_ref[...])
    cs = cs_ref[...]
    for h in range(w_ref.shape[0]):
        y = _dot(n, w_ref[h])
        rope = _rope_slab(y[:, HEAD_DIM:], cs)
        o_ref[h] = (jnp.concatenate([y[:, :HEAD_DIM], rope], axis=1) * QK_LOG2_SCALE).astype(o_ref.dtype)


def mla_queries(cfg, p, norm_w, w_q, cs):
    tm = _pick(cfg.nc, (512, 256))
    nh = cfg.mla_heads
    col = cfg.off_cq // cfg.q_lora
    return pl.pallas_call(
        _mla_q_kernel,
        grid=(cfg.nt // tm,),
        in_specs=[
            pl.BlockSpec((tm, cfg.q_lora), lambda i: (i, col)),
            pl.BlockSpec((1, cfg.q_lora), lambda i: (0, 0)),
            pl.BlockSpec((tm, HEAD_DIM), lambda i: (i, 0)),
            pl.BlockSpec((nh, cfg.q_lora, QK_DIM), lambda i: (0, 0, 0)),
        ],
        out_specs=pl.BlockSpec((nh, tm, QK_DIM), lambda i: (0, i, 0)),
        out_shape=jax.ShapeDtypeStruct((nh, cfg.nt, QK_DIM), BF16),
        compiler_params=_params(("arbitrary",)),
        name="mla_queries",
    )(p, norm_w.reshape(1, cfg.q_lora), cs, w_q)


def _mla_kv_kernel(ckv_ref, slab_ref, nw_ref, cs_ref, w_ref, k_ref, v_ref):
    n = _rms_bf16(ckv_ref[...], nw_ref[...])
    k_rope = _rope_slab(slab_ref[...], cs_ref[...])
    lane = lax.broadcasted_iota(jnp.int32, k_rope.shape, 1)
    ones_col = jnp.where(lane == 0, 1.0, 0.0)
    for h in range(w_ref.shape[0]):
        y = _dot(n, w_ref[h])
        k_ref[h] = jnp.concatenate([y[:, :HEAD_DIM], k_rope], axis=1).astype(k_ref.dtype)
        v_ref[h] = jnp.concatenate([y[:, HEAD_DIM:], ones_col], axis=1).astype(v_ref.dtype)


def mla_keys_values(cfg, p, norm_w, w_kv, cs):
    tm = _pick(cfg.nc, (512, 256))
    nh = cfg.mla_heads
    col = cfg.off_ckv // cfg.kv_lora
    slab_col = cfg.off_slab // HEAD_DIM
    return pl.pallas_call(
        _mla_kv_kernel,
        grid=(cfg.nt // tm,),
        in_specs=[
            pl.BlockSpec((tm, cfg.kv_lora), lambda i: (i, col)),
            pl.BlockSpec((tm, HEAD_DIM), lambda i: (i, slab_col)),
            pl.BlockSpec((1, cfg.kv_lora), lambda i: (0, 0)),
            pl.BlockSpec((tm, HEAD_DIM), lambda i: (i, 0)),
            pl.BlockSpec((nh, cfg.kv_lora, 2 * HEAD_DIM), lambda i: (0, 0, 0)),
        ],
        out_specs=[
            pl.BlockSpec((nh, tm, QK_DIM), lambda i: (0, i, 0)),
            pl.BlockSpec((nh, tm, 2 * HEAD_DIM), lambda i: (0, i, 0)),
        ],
        out_shape=[
            jax.ShapeDtypeStruct((nh, cfg.nt, QK_DIM), BF16),
            jax.ShapeDtypeStruct((nh, cfg.nt, 2 * HEAD_DIM), BF16),
        ],
        compiler_params=_params(("arbitrary",)),
        name="mla_keys_values",
    )(p, p, norm_w.reshape(1, cfg.kv_lora), cs, w_kv)


ATTN_SUB = 256
ATTN_KT = 512


def _softmax_pv(s_list, v_list):
    m = functools.reduce(jnp.maximum, [jnp.max(s, axis=-1, keepdims=True) for s in s_list])
    o = None
    for s, v in zip(s_list, v_list):
        part = _dot(jnp.exp2(s - m).astype(BF16), v)
        o = part if o is None else o + part
    return o[:, :HEAD_DIM] / o[:, HEAD_DIM:HEAD_DIM + 1]


def _attn_lat_kernel(q_ref, kl_ref, kc_ref, vl_ref, vc_ref, o_ref, s_ref):
    n_keys = kl_ref.shape[0]
    for r in range(0, q_ref.shape[0], ATTN_SUB):
        rows = slice(r, r + ATTN_SUB)
        q = q_ref[rows, :]
        sc = _dot_nt(q, kc_ref[...])
        m = jnp.max(sc, axis=-1, keepdims=True)
        for c in range(0, n_keys, ATTN_KT):
            s = _dot_nt(q, kl_ref[c:c + ATTN_KT, :])
            s_ref[rows, c:c + ATTN_KT] = s
            m = jnp.maximum(m, jnp.max(s, axis=-1, keepdims=True))
        o = _dot(jnp.exp2(sc - m).astype(BF16), vc_ref[...])
        for c in range(0, n_keys, ATTN_KT):
            p = jnp.exp2((s_ref[rows, c:c + ATTN_KT] - m).astype(BF16))
            o = o + _dot(p, vl_ref[c:c + ATTN_KT, :])
        o_ref[rows, :] = (o[:, :HEAD_DIM] / o[:, HEAD_DIM:HEAD_DIM + 1]).astype(o_ref.dtype)


def _attn_ctx_kernel(q_ref, kc_ref, vc_ref, o_ref):
    o_ref[...] = _softmax_pv([_dot_nt(q_ref[...], kc_ref[...])], [vc_ref[...]]).astype(o_ref.dtype)


def mla_attention(cfg, q, k, v, *, with_ctx, out_rows):
    nh = cfg.mla_heads
    vw = 2 * HEAD_DIM
    tq = _pick(cfg.seq, (1024, 512, 256))
    nq = cfg.seq // tq
    cb0 = cfg.nl // cfg.ctx
    y_lat = pl.pallas_call(
        _attn_lat_kernel,
        grid=(cfg.batch, nh, nq),
        in_specs=[
            pl.BlockSpec((None, tq, QK_DIM), lambda b, h, i: (h, b * nq + i, 0)),
            pl.BlockSpec((None, cfg.seq, QK_DIM), lambda b, h, i: (h, b, 0)),
            pl.BlockSpec((None, cfg.ctx, QK_DIM), lambda b, h, i: (h, cb0 + b, 0)),
            pl.BlockSpec((None, cfg.seq, vw), lambda b, h, i: (h, b, 0)),
            pl.BlockSpec((None, cfg.ctx, vw), lambda b, h, i: (h, cb0 + b, 0)),
        ],
        out_specs=pl.BlockSpec((tq, HEAD_DIM), lambda b, h, i: (b * nq + i, h)),
        out_shape=jax.ShapeDtypeStruct((out_rows, nh * HEAD_DIM), BF16),
        scratch_shapes=[pltpu.VMEM((tq, cfg.seq), F32)],
        compiler_params=_params(("arbitrary", "arbitrary", "arbitrary")),
        name="mla_attention",
    )(q, k, k, v, v)
    if not with_ctx:
        return y_lat, None
    y_ctx = pl.pallas_call(
        _attn_ctx_kernel,
        grid=(cfg.batch, nh),
        in_specs=[
            pl.BlockSpec((None, cfg.ctx, QK_DIM), lambda b, h: (h, cb0 + b, 0)),
            pl.BlockSpec((None, cfg.ctx, QK_DIM), lambda b, h: (h, cb0 + b, 0)),
            pl.BlockSpec((None, cfg.ctx, vw), lambda b, h: (h, cb0 + b, 0)),
        ],
        out_specs=pl.BlockSpec((cfg.ctx, HEAD_DIM), lambda b, h: (b, h)),
        out_shape=jax.ShapeDtypeStruct((cfg.nc, nh * HEAD_DIM), BF16),
        compiler_params=_params(("arbitrary", "arbitrary")),
        name="mla_attention_ctx",
    )(q, k, v)
    return y_lat, y_ctx


def _merge_kernel(ya_ref, yb_ref, yc_ref, ga_ref, gb_ref, gc_ref, wa_ref, wb_ref, wc_ref, o_ref):
    m = (_sigmoid(ga_ref[...]) * _dot(ya_ref[...], wa_ref[...])
         + _sigmoid(gb_ref[...]) * _dot(yb_ref[...], wb_ref[...])
         + _sigmoid(gc_ref[...]) * _dot(yc_ref[...], wc_ref[...]))
    o_ref[...] = m.astype(o_ref.dtype)


def merge_branches(cfg, ya, yb, yc, p, wa, wb, wc, *, rows):
    tm = _pick(cfg.nc, (512, 256))
    tn = _pick(cfg.d, (512, 256))
    gcols = cfg.d // tn

    def gate_spec(k):
        return pl.BlockSpec((tm, tn), lambda i, j, k=k: (i, k * gcols + j))

    return pl.pallas_call(
        _merge_kernel,
        grid=(rows // tm, cfg.d // tn),
        in_specs=[
            pl.BlockSpec((tm, ya.shape[1]), lambda i, j: (i, 0)),
            pl.BlockSpec((tm, yb.shape[1]), lambda i, j: (i, 0)),
            pl.BlockSpec((tm, yc.shape[1]), lambda i, j: (i, 0)),
            gate_spec(0), gate_spec(1), gate_spec(2),
            pl.BlockSpec((wa.shape[0], tn), lambda i, j: (0, j)),
            pl.BlockSpec((wb.shape[0], tn), lambda i, j: (0, j)),
            pl.BlockSpec((wc.shape[0], tn), lambda i, j: (0, j)),
        ],
        out_specs=pl.BlockSpec((tm, tn), lambda i, j: (i, j)),
        out_shape=jax.ShapeDtypeStruct((rows, cfg.d), BF16),
        compiler_params=_params(("arbitrary", "arbitrary")),
        name="merge_branches",
    )(ya, yb, yc, p, p, p, wa, wb, wc)


def _matmul_resid_kernel(a_ref, w_ref, x_ref, mod_ref, o_ref, *, k):
    o_ref[...] = x_ref[...] + mod_ref[k:k + 1, :] * _dot(a_ref[...], w_ref[...])


def matmul_residual(cfg, a, w, x, mod_l, *, k, rows):
    tm = _pick(cfg.nc, (1024, 512, 256))
    tn = _pick(cfg.d, (512, 256))
    row_of = _mod_row_map(cfg, tm)
    return pl.pallas_call(
        functools.partial(_matmul_resid_kernel, k=k),
        grid=(rows // tm, cfg.d // tn),
        in_specs=[
            pl.BlockSpec((tm, a.shape[1]), lambda i, j: (i, 0)),
            pl.BlockSpec((a.shape[1], tn), lambda i, j: (0, j)),
            pl.BlockSpec((tm, tn), lambda i, j: (i, j)),
            pl.BlockSpec((None, 6, tn), lambda i, j: (row_of(i), 0, j)),
        ],
        out_specs=pl.BlockSpec((tm, tn), lambda i, j: (i, j)),
        out_shape=jax.ShapeDtypeStruct((rows, cfg.d), F32),
        compiler_params=_params(("arbitrary", "arbitrary")),
        name="matmul_residual",
    )(a, w, x, mod_l)


def _top16_rows(s, top_ref):
    n = s.shape[0]
    row = lax.broadcasted_iota(jnp.int32, s.shape, 0).astype(F32)
    rank = jnp.full(s.shape, UNRANKED, F32)
    work = s
    for r in range(PEER_TOPK):
        m = jnp.max(work, axis=0, keepdims=True)
        first = jnp.min(jnp.where(work == m, row, float(n)), axis=0, keepdims=True)
        sel = row == first
        rank = jnp.where(sel, float(r), rank)
        work = jnp.where(sel, -jnp.inf, work)
        top_ref[r:r + 1, :] = m
    return rank


def _peer_tables_kernel(qh_ref, keys_ref, b_ref, e2_ref, c_ref, e1_ref, top1_ref, top2_ref, cnt_ref, *, tile):
    lanes = HEAD_DIM
    k1 = keys_ref[0]
    k2 = keys_ref[1]
    for part in range(tile // lanes):
        rows = slice(part * lanes, (part + 1) * lanes)
        cols = slice(part * lanes, (part + 1) * lanes)
        qh = qh_ref[rows, :].astype(BF16)
        s1 = _dot_nt(k1, qh[:, :lanes])
        s2 = _dot_nt(k2, qh[:, lanes:])
        rank1 = _top16_rows(s1, top1_ref)
        rank2 = _top16_rows(s2, top2_ref)
        t1 = top1_ref[...]
        t2 = top2_ref[...]
        pieces = [t1[0:1] + t2]
        pos = [lax.broadcasted_iota(jnp.int32, (PEER_TOPK, lanes), 0).astype(F32)]
        for a in range(1, 8):
            pieces.append(t1[a:a + 1] + t2[0:8])
            pos.append(lax.broadcasted_iota(jnp.int32, (8, lanes), 0).astype(F32) + float(a * PEER_TOPK))
        pieces.append(t1[8:16] + t2[0:1])
        pos.append((lax.broadcasted_iota(jnp.int32, (8, lanes), 0).astype(F32) + 8.0) * float(PEER_TOPK))
        cand = jnp.concatenate(pieces, axis=0)
        cpos = jnp.concatenate(pos, axis=0)
        a_row = lax.broadcasted_iota(jnp.int32, (PEER_TOPK, lanes), 0).astype(F32)
        cnt = jnp.zeros((PEER_TOPK, lanes), F32)
        z = jnp.zeros((1, lanes), F32)
        best0 = None
        for r in range(PEER_TOPK):
            m = jnp.max(cand, axis=0, keepdims=True)
            first = jnp.min(jnp.where(cand == m, cpos, 1e9), axis=0, keepdims=True)
            cand = jnp.where(cpos == first, -jnp.inf, cand)
            cnt = cnt + jnp.where(a_row == jnp.floor(first * (1.0 / PEER_TOPK)), 1.0, 0.0)
            if r == 0:
                best0 = m
            z = z + jnp.exp(m - best0)
        cnt_ref[...] = cnt
        c_tab = jnp.zeros((PEER_NKEYS, lanes), F32)
        for a in range(PEER_TOPK):
            c_tab = jnp.where(rank1 == float(a), cnt_ref[a:a + 1, :], c_tab)
        b_ref[:, cols] = rank2.astype(b_ref.dtype)
        c_ref[:, cols] = c_tab
        e1_ref[:, cols] = jnp.exp(s1 - t1[0:1]) / z
        e2_ref[:, cols] = jnp.exp(s2 - t2[0:1]).astype(e2_ref.dtype)


def peer_tables(cfg, qh, keys, *, rows):
    tile = 256
    nh = cfg.peer_heads
    shapes = [jax.ShapeDtypeStruct((nh, PEER_NKEYS, rows), dt) for dt in (BF16, BF16, F32, F32)]
    out_spec = pl.BlockSpec((None, PEER_NKEYS, tile), lambda i, h: (h, 0, i))
    return pl.pallas_call(
        functools.partial(_peer_tables_kernel, tile=tile),
        grid=(rows // tile, nh),
        in_specs=[
            pl.BlockSpec((tile, 2 * HEAD_DIM), lambda i, h: (i, h)),
            pl.BlockSpec((2, None, PEER_NKEYS, HEAD_DIM), lambda i, h: (0, h, 0, 0)),
        ],
        out_specs=[out_spec] * 4,
        out_shape=shapes,
        scratch_shapes=[pltpu.VMEM((PEER_TOPK, HEAD_DIM), F32) for _ in range(3)],
        compiler_params=_params(("arbitrary", "arbitrary")),
        name="peer_tables",
    )(qh, keys)


PEER_ROWS = 8
PEER_EB = PEER_ROWS * PEER_NKEYS
PEER_DOT_ROWS = 1


def _peer_dense_kernel(ht_ref, u_ref, vt_ref, b_ref, e2_ref, c_ref, e1_ref, x_ref, mod_ref, o_ref,
                       acc_ref, w_ref, *, heads):
    e = pl.program_id(1)

    @pl.when(e == 0)
    def _():
        acc_ref[...] = jnp.zeros_like(acc_ref)

    zero = jnp.zeros((), BF16)
    ht = ht_ref[...]
    for r0 in range(0, PEER_ROWS, PEER_DOT_ROWS):
        blk = slice(r0 * PEER_NKEYS, (r0 + PEER_DOT_ROWS) * PEER_NKEYS)
        act = _dot(u_ref[blk, :], ht)
        gelu = (0.5 * act * (1.0 + lax.erf(act * float(math.sqrt(0.5))))).astype(BF16)
        for q in range(PEER_DOT_ROWS):
            r = r0 + q
            g = jnp.zeros((PEER_NKEYS, act.shape[1]), BF16)
            for h in range(heads):
                crow = jnp.broadcast_to(c_ref[h, r:r + 1, :], g.shape).astype(BF16)
                erow = jnp.broadcast_to(e1_ref[h, r:r + 1, :], g.shape).astype(BF16)
                g = g + jnp.where(b_ref[h] < crow, e2_ref[h] * erow, zero)
            w_ref[r * PEER_NKEYS:(r + 1) * PEER_NKEYS, :] = g * gelu[q * PEER_NKEYS:(q + 1) * PEER_NKEYS]
    acc_ref[...] += _dot(vt_ref[...], w_ref[...])

    @pl.when(e == pl.num_programs(1) - 1)
    def _():
        o_ref[...] = x_ref[...] + mod_ref[5:6, :] * acc_ref[...].T


def peer_dense(cfg, ht, u, vt, tabs, x, mod_l, *, rows):
    tile = _pick(cfg.nc, (512, 256))
    eb = PEER_EB
    n_exp = u.shape[0]
    nh = cfg.peer_heads
    row_of = _mod_row_map(cfg, tile)
    tab_spec = pl.BlockSpec((nh, PEER_NKEYS, tile), lambda i, e: (0, 0, i))
    row_spec = pl.BlockSpec((nh, PEER_ROWS, tile), lambda i, e: (0, e, i))
    return pl.pallas_call(
        functools.partial(_peer_dense_kernel, heads=nh),
        grid=(rows // tile, n_exp // eb),
        in_specs=[
            pl.BlockSpec((cfg.d, tile), lambda i, e: (0, i)),
            pl.BlockSpec((eb, cfg.d), lambda i, e: (e, 0)),
            pl.BlockSpec((cfg.d, eb), lambda i, e: (0, e)),
            tab_spec, tab_spec, row_spec, row_spec,
            pl.BlockSpec((tile, cfg.d), lambda i, e: (i, 0)),
            pl.BlockSpec((None, 6, cfg.d), lambda i, e: (row_of(i), 0, 0)),
        ],
        out_specs=pl.BlockSpec((tile, cfg.d), lambda i, e: (i, 0)),
        out_shape=jax.ShapeDtypeStruct((rows, cfg.d), F32),
        scratch_shapes=[pltpu.VMEM((cfg.d, tile), F32), pltpu.VMEM((eb, tile), BF16)],
        compiler_params=_params(("arbitrary", "arbitrary")),
        name="peer_dense",
    )(ht, u, vt, *tabs, x, mod_l)


def _final_norm_kernel(x_ref, w_ref, o_ref):
    x = x_ref[...]
    o_ref[...] = x * lax.rsqrt(jnp.mean(x * x, axis=-1, keepdims=True) + RMS_EPS) * w_ref[...]


def final_norm(cfg, x, w):
    tm = _pick(cfg.nl, (1024, 512, 256))
    return pl.pallas_call(
        _final_norm_kernel,
        grid=(cfg.nl // tm,),
        in_specs=[pl.BlockSpec((tm, cfg.d), lambda i: (i, 0)), pl.BlockSpec((1, cfg.d), lambda i: (0, 0))],
        out_specs=pl.BlockSpec((tm, cfg.d), lambda i: (i, 0)),
        out_shape=jax.ShapeDtypeStruct((cfg.nl, cfg.d), F32),
        compiler_params=_params(("arbitrary",)),
        name="final_norm",
    )(x, w.reshape(1, cfg.d))


def _rot_cols(w):
    q = ROPE_DIM // 4
    return jnp.concatenate([-w[..., q:2 * q], w[..., 0:q], -w[..., 3 * q:4 * q], w[..., 2 * q:3 * q]], axis=-1)


def _in_proj_weight(cfg, w_in, n_cols):
    d = cfg.d
    hg_end = 5 * cfg.hgw
    pool_end = hg_end + POOL_WIDTH
    cq_end = pool_end + cfg.q_lora
    ckv_end = cq_end + cfg.kv_lora
    rope_end = ckv_end + ROPE_DIM
    k_rope = w_in[:, ckv_end:rope_end]
    parts = [w_in[:, rope_end:rope_end + 3 * d], w_in[:, :ckv_end], k_rope, _rot_cols(k_rope)]
    w = jnp.concatenate(parts, axis=1)
    return jnp.pad(w, ((0, 0), (0, n_cols - w.shape[1]))).astype(BF16)


def _mla_q_weight(cfg, w_uq):
    w = w_uq.reshape(cfg.q_lora, cfg.mla_heads, HEAD_DIM + ROPE_DIM)
    rope = w[..., HEAD_DIM:]
    w = jnp.concatenate([w[..., :HEAD_DIM], rope, _rot_cols(rope)], axis=-1)
    return jnp.transpose(w, (1, 0, 2)).astype(BF16)


def _mla_kv_weight(cfg, w_ukv):
    w = w_ukv.reshape(cfg.kv_lora, cfg.mla_heads, 2 * HEAD_DIM)
    return jnp.transpose(w, (1, 0, 2)).astype(BF16)


def _rope_table(cfg):
    rows = cfg.seq // cfg.grid_w
    r, col = jnp.meshgrid(jnp.arange(rows), jnp.arange(cfg.grid_w), indexing="ij")
    n_freq = ROPE_DIM // 4
    freqs = ROPE_THETA ** (-jnp.arange(n_freq, dtype=F32) / n_freq)
    ang_r = r.reshape(-1)[:, None] * freqs
    ang_c = col.reshape(-1)[:, None] * freqs
    cos = jnp.concatenate([jnp.cos(ang_r)] * 2 + [jnp.cos(ang_c)] * 2, axis=1)
    sin = jnp.concatenate([jnp.sin(ang_r)] * 2 + [jnp.sin(ang_c)] * 2, axis=1)
    lat = jnp.tile(jnp.concatenate([cos, sin], axis=1).astype(F32), (cfg.batch, 1))
    ctx = jnp.concatenate([jnp.ones((cfg.nc, ROPE_DIM), F32), jnp.zeros((cfg.nc, ROPE_DIM), F32)], axis=1)
    return jnp.concatenate([lat, ctx], axis=0)


def _forward(cfg, x, c, ctx, c_ctx, w_mod, b_mod, norm_mix, norm_ffn, w_in, hg_lb_logits, hg_norm,
             pool_w, pool_scale, mla_q_norm, mla_w_uq, mla_kv_norm, mla_w_ukv,
             w_branch_a, w_branch_b, w_branch_c, w_out, peer_wq, peer_keys, peer_u, peer_v, final_w):
    d = cfg.d
    assert cfg.seq % HG_CHUNK == 0 and cfg.ctx % HG_CHUNK == 0 and cfg.nl % cfg.ctx == 0
    assert cfg.off_pool % POOL_WIDTH == 0 and cfg.off_cq % cfg.q_lora == 0
    assert cfg.off_ckv % cfg.kv_lora == 0 and cfg.batch < MOD_ROWS
    tn_in = 768 if d % 256 == 0 and cfg.in_cols > 8192 else 256
    n_cols = -(-cfg.in_cols // tn_in) * tn_in
    tm = _pick(cfg.nc, (1024, 512, 256))

    xs = jnp.concatenate([x.reshape(cfg.nl, d), ctx.reshape(cfg.nc, d)], axis=0)
    c_all = jnp.concatenate([c, c_ctx[None], jnp.zeros((MOD_ROWS - cfg.batch - 1, d), F32)], axis=0)
    mod = adaln_tables(cfg, c_all, w_mod, b_mod)
    cs = _rope_table(cfg)

    for l in range(cfg.depth):
        last = l == cfg.depth - 1
        rows = cfg.nl if last else cfg.nt
        mod_l = mod[l]
        p = norm_matmul(cfg, xs, norm_mix[l], mod_l, _in_proj_weight(cfg, w_in[l], n_cols),
                        k0=0, rows=cfg.nt, tm=tm, tn=tn_in, emit_h=False)
        ya, hg_ctx = hgrn_mixer(cfg, p, hg_lb_logits, hg_norm[l], l, rows)
        pw = pool_w[l].astype(BF16)
        yb = pool_mixer(cfg, p, pw, pool_scale[l], n_seq=cfg.batch, seq_len=cfg.seq, row0=0, out_rows=rows)
        q = mla_queries(cfg, p, mla_q_norm[l], _mla_q_weight(cfg, mla_w_uq[l]), cs)
        k, v = mla_keys_values(cfg, p, mla_kv_norm[l], _mla_kv_weight(cfg, mla_w_ukv[l]), cs)
        yc, att_ctx = mla_attention(cfg, q, k, v, with_ctx=not last, out_rows=rows)
        if not last:
            pool_ctx = pool_mixer(cfg, p, pw, pool_scale[l], n_seq=cfg.batch, seq_len=cfg.ctx, row0=cfg.nl)
            ya = lax.dynamic_update_slice(ya, hg_ctx, (cfg.nl, 0))
            yb = lax.dynamic_update_slice(yb, pool_ctx, (cfg.nl, 0))
            yc = lax.dynamic_update_slice(yc, att_ctx, (cfg.nl, 0))
        m = merge_branches(cfg, ya, yb, yc, p, w_branch_a[l].astype(BF16), w_branch_b[l].astype(BF16),
                           w_branch_c[l].astype(BF16), rows=rows)
        xs = matmul_residual(cfg, m, w_out[l].astype(BF16), xs, mod_l, k=2, rows=rows)
        qh, h2 = norm_matmul(cfg, xs, norm_ffn[l], mod_l, peer_wq[l].astype(BF16),
                             k0=3, rows=rows, tm=tm, tn=_pick(peer_wq.shape[2], (512, 256)), emit_h=True)
        tabs = peer_tables(cfg, qh, peer_keys[l].astype(BF16), rows=rows)
        xs = peer_dense(cfg, h2.T, peer_u[l].astype(BF16), peer_v[l].T.astype(BF16), tabs, xs, mod_l, rows=rows)

    return final_norm(cfg, xs, final_w).reshape(cfg.batch, cfg.seq, d)


def kernel(x, c, ctx, c_ctx, w_mod, b_mod, norm_mix, norm_ffn, w_in, hg_lb_logits, hg_norm, pool_w, pool_scale,
           mla_q_norm, mla_w_uq, mla_kv_norm, mla_w_ukv, w_branch_a, w_branch_b, w_branch_c, w_out,
           peer_wq, peer_keys, peer_u, peer_v, final_norm):
    batch, seq, d = x.shape
    cfg = Cfg(d=d, batch=batch, seq=seq, ctx=ctx.shape[1], grid_w=64, depth=w_mod.shape[0],
              hg_heads=hg_lb_logits.shape[2] // HEAD_DIM,
              mla_heads=mla_w_ukv.shape[2] // (2 * HEAD_DIM), q_lora=mla_q_norm.shape[1],
              kv_lora=mla_kv_norm.shape[1], peer_heads=peer_keys.shape[2])
    return _forward(cfg, x, c, ctx, c_ctx, w_mod, b_mod, norm_mix, norm_ffn, w_in, hg_lb_logits, hg_norm,
                    pool_w, pool_scale, mla_q_norm, mla_w_uq, mla_kv_norm, mla_w_ukv,
                    w_branch_a, w_branch_b, w_branch_c, w_out, peer_wq, peer_keys, peer_u, peer_v, final_norm)
```

```python
import functools
import math
from typing import NamedTuple

import numpy as np
import jax
import jax.numpy as jnp
from jax import lax
from jax.experimental import pallas as pl
from jax.experimental.pallas import tpu as pltpu

F32 = jnp.float32
BF16 = jnp.bfloat16

RMS_EPS = 1e-6
ROPE_THETA = 10000.0
HEAD_DIM = 128
ROPE_DIM = 64
QK_DIM = 256
LOG2_E = float(math.log2(math.e))
QK_LOG2_SCALE = float((HEAD_DIM + ROPE_DIM) ** -0.5) * LOG2_E
POOL_WINDOWS = (2, 4, 8, 16)
POOL_GROUP = 256
POOL_WIDTH = POOL_GROUP * len(POOL_WINDOWS)
POOL_HALO = 8
PEER_NKEYS = 128
PEER_TOPK = 16
HG_CHUNK = 128
HG_LEVELS = 7
UNRANKED = 99.0
MOD_ROWS = 8
VMEM_LIMIT = 56 * 1024 * 1024


class Cfg(NamedTuple):
    d: int
    batch: int
    seq: int
    ctx: int
    grid_w: int
    depth: int
    hg_heads: int
    mla_heads: int
    q_lora: int
    kv_lora: int
    peer_heads: int

    @property
    def nl(self):
        return self.batch * self.seq

    @property
    def nc(self):
        return self.batch * self.ctx

    @property
    def nt(self):
        return self.nl + self.nc

    @property
    def hgw(self):
        return self.hg_heads * HEAD_DIM

    @property
    def off_gate(self):
        return 0

    @property
    def off_hg(self):
        return 3 * self.d

    @property
    def off_pool(self):
        return self.off_hg + 5 * self.hgw

    @property
    def off_cq(self):
        return self.off_pool + POOL_WIDTH

    @property
    def off_ckv(self):
        return self.off_cq + self.q_lora

    @property
    def off_slab(self):
        return self.off_ckv + self.kv_lora

    @property
    def in_cols(self):
        return self.off_slab + HEAD_DIM


def _params(sem):
    return pltpu.CompilerParams(dimension_semantics=sem, vmem_limit_bytes=VMEM_LIMIT)


def _dot(a, b):
    return jnp.dot(a, b, preferred_element_type=F32)


def _dot_nt(a, b):
    return lax.dot_general(a, b, (((1,), (1,)), ((), ())), preferred_element_type=F32)


def _sigmoid(x):
    return 1.0 / (1.0 + jnp.exp(-x))


def _pick(n, prefs):
    for p in prefs:
        if n % p == 0:
            return p
    raise ValueError(f"no tile for {n} in {prefs}")


def _mod_row_map(cfg, tm):
    n_lat = cfg.nl // tm
    per_batch = cfg.seq // tm
    return lambda i: jnp.where(i < n_lat, i // per_batch, cfg.batch)


def _mod_kernel(c_ref, w_ref, b_ref, o_ref):
    c = c_ref[...]
    s = (c * _sigmoid(c)).astype(BF16)
    o_ref[...] = _dot(s, w_ref[...].astype(BF16)) + b_ref[...]


def adaln_tables(cfg, c_all, w_mod, b_mod):
    d6 = 6 * cfg.d
    tn = _pick(d6, (1024, 768, 512, 256))
    out = pl.pallas_call(
        _mod_kernel,
        grid=(cfg.depth, d6 // tn),
        in_specs=[
            pl.BlockSpec((MOD_ROWS, cfg.d), lambda l, j: (0, 0)),
            pl.BlockSpec((None, cfg.d, tn), lambda l, j: (l, 0, j)),
            pl.BlockSpec((None, 1, tn), lambda l, j: (l, 0, j)),
        ],
        out_specs=pl.BlockSpec((None, MOD_ROWS, tn), lambda l, j: (l, 0, j)),
        out_shape=jax.ShapeDtypeStruct((cfg.depth, MOD_ROWS, d6), F32),
        compiler_params=_params(("arbitrary", "arbitrary")),
        name="adaln_tables",
    )(c_all, w_mod, b_mod.reshape(cfg.depth, 1, d6))
    return out.reshape(cfg.depth, MOD_ROWS, 6, cfg.d)


def _norm_matmul_kernel(x_ref, gain_ref, mod_ref, w_ref, *rest, k0, emit_h):
    if emit_h:
        o_ref, hout_ref, h_ref = rest
    else:
        o_ref, h_ref = rest

    @pl.when(pl.program_id(1) == 0)
    def _():
        x = x_ref[...]
        y = x * lax.rsqrt(jnp.mean(x * x, axis=-1, keepdims=True) + RMS_EPS) * gain_ref[...]
        h = (y * (1.0 + mod_ref[k0 + 1:k0 + 2, :]) + mod_ref[k0:k0 + 1, :]).astype(BF16)
        h_ref[...] = h
        if emit_h:
            hout_ref[...] = h

    o_ref[...] = _dot(h_ref[...], w_ref[...])


def norm_matmul(cfg, x, gain, mod_l, w, *, k0, rows, tm, tn, emit_h):
    n = w.shape[1]
    row_of = _mod_row_map(cfg, tm)
    out_shape = [jax.ShapeDtypeStruct((rows, n), F32)]
    out_specs = [pl.BlockSpec((tm, tn), lambda i, j: (i, j))]
    if emit_h:
        out_shape.append(jax.ShapeDtypeStruct((rows, cfg.d), BF16))
        out_specs.append(pl.BlockSpec((tm, cfg.d), lambda i, j: (i, 0)))
    res = pl.pallas_call(
        functools.partial(_norm_matmul_kernel, k0=k0, emit_h=emit_h),
        grid=(rows // tm, n // tn),
        in_specs=[
            pl.BlockSpec((tm, cfg.d), lambda i, j: (i, 0)),
            pl.BlockSpec((1, cfg.d), lambda i, j: (0, 0)),
            pl.BlockSpec((None, 6, cfg.d), lambda i, j: (row_of(i), 0, 0)),
            pl.BlockSpec((cfg.d, tn), lambda i, j: (0, j)),
        ],
        out_specs=out_specs,
        out_shape=out_shape,
        scratch_shapes=[pltpu.VMEM((tm, cfg.d), BF16)],
        compiler_params=_params(("arbitrary", "arbitrary")),
        name="norm_matmul_h" if emit_h else "norm_matmul",
    )(x, gain.reshape(1, cfg.d), mod_l, w)
    return res if emit_h else res[0]


def _hgrn_consts():
    c = HG_CHUNK
    out = []
    for rev in (False, True):
        p = np.arange(c) if not rev else c - 1 - np.arange(c)
        pt, pu = p[:, None], p[None, :]
        g = np.zeros((HG_LEVELS + 1, c, c), np.float32)
        up = np.zeros((HG_LEVELS, c, c), np.float32)
        g[0] = pu <= pt
        for l in range(HG_LEVELS):
            m = 1 << l
            blk = p >> (l + 1)
            upper = ((p >> l) & 1) == 1
            mid = (blk * 2 * m + m)[:, None]
            same = blk[:, None] == blk[None, :]
            g_up = same & (pu >= mid) & (pu <= pt)
            g_lo = same & (pu > pt) & (pu < mid)
            g[1 + l] = np.where(upper[:, None], g_up, g_lo)
            up[l] = np.broadcast_to(upper[:, None], (c, c))
        x = pt ^ pu
        lv = np.where(pu < pt, np.floor(np.log2(np.maximum(x, 1))), -1.0).astype(np.float32)
        out.append((jnp.asarray(g.reshape(-1, c), BF16), jnp.asarray(up, F32), jnp.asarray(lv, F32)))
    return out


def _hgrn_chunk(q_raw, v, f_raw, log_lb, log_1mlb, one_m_lb, g_ref, up_ref, lv_ref, st_ref, end_row):
    c = HG_CHUNK
    q = q_raw * _sigmoid(q_raw)
    e = jnp.exp(-jnp.abs(f_raw))
    one_pe = 1.0 + e
    log_sig = jnp.minimum(f_raw, 0.0) - jnp.log(one_pe)
    t = log_1mlb + log_sig
    log_f = jnp.maximum(log_lb, t) + jnp.log(1.0 + jnp.exp(-jnp.abs(log_lb - t)))
    k = one_m_lb * jnp.where(f_raw >= 0.0, e, 1.0) / one_pe
    hi = log_f.astype(BF16)
    lo = (log_f - hi.astype(F32)).astype(BF16)
    a2 = _dot(g_ref[...], jnp.concatenate([hi, lo], axis=1))
    a = a2[:, :HEAD_DIM] + a2[:, HEAD_DIM:]
    b = a[0:c]
    lv = lv_ref[...]
    scores = jnp.zeros((c, c), F32)
    for l in range(HG_LEVELS):
        e_l = jnp.exp(a[(1 + l) * c:(2 + l) * c])
        x = (jnp.where(up_ref[l] > 0.5, q, k) * e_l).astype(BF16)
        scores = jnp.where(lv == float(l), _dot_nt(x, x), scores)
    b_end = b[end_row:end_row + 1, :]
    qb = (q * jnp.exp(b)).astype(BF16)
    kd = (k * jnp.exp(b_end - b)).astype(BF16)
    st = st_ref[...]
    vb = v.astype(BF16)
    o = (_dot(scores.astype(BF16), vb) + _dot_nt(qb, st.astype(BF16))
         + jnp.sum(q * k, axis=-1, keepdims=True) * v)
    st_ref[...] = st * jnp.exp(b_end) + _dot(v.T.astype(BF16), kd)
    return o


def _hgrn_kernel(ql, ffl, fbl, il, gl, qc, ffc, fbc, ic, gc, logit_ref, nw_ref,
                 gf_ref, upf_ref, lvf_ref, gb_ref, upb_ref, lvb_ref,
                 yl_ref, yc_ref, ofl, obl, ofc, obc, stf, stb, *, layer, seq, ctx):
    c = HG_CHUNK
    depth = logit_ref.shape[0]
    lg = [logit_ref[dd] for dd in range(depth)]
    mx = functools.reduce(jnp.maximum, lg)
    ex = [jnp.exp(v - mx) for v in lg]
    tot = functools.reduce(jnp.add, ex)
    cum = [ex[0] / tot]
    for dd in range(1, layer + 1):
        cum.append(cum[-1] + ex[dd] / tot)
    lb = cum[layer] - cum[0]
    log_lb = jnp.log(lb)
    log_1mlb = jnp.log1p(-lb)
    one_m_lb = 1.0 - lb

    stf[...] = jnp.zeros_like(stf)
    stb[...] = jnp.zeros_like(stb)

    def segment(q_ref, ff_ref, fb_ref, i_ref, of_ref, ob_ref, n):
        def body(j, carry):
            rf = pl.multiple_of(j * c, c)
            rb = pl.multiple_of((n - 1 - j) * c, c)
            of_ref[pl.ds(rf, c), :] = _hgrn_chunk(
                q_ref[pl.ds(rf, c), :], i_ref[pl.ds(rf, c), :], ff_ref[pl.ds(rf, c), :],
                log_lb[0:1], log_1mlb[0:1], one_m_lb[0:1], gf_ref, upf_ref, lvf_ref, stf, c - 1)
            ob_ref[pl.ds(rb, c), :] = _hgrn_chunk(
                q_ref[pl.ds(rb, c), :], i_ref[pl.ds(rb, c), :], fb_ref[pl.ds(rb, c), :],
                log_lb[1:2], log_1mlb[1:2], one_m_lb[1:2], gb_ref, upb_ref, lvb_ref, stb, 0)
            return carry
        lax.fori_loop(0, n, body, 0, unroll=4 if n % 4 == 0 else 2)

    segment(qc, ffc, fbc, ic, ofc, obc, ctx // c)
    segment(ql, ffl, fbl, il, ofl, obl, seq // c)

    nw = nw_ref[...]

    def readout(of_ref, ob_ref, g_ref, y_ref, n):
        def body(j, carry):
            r = pl.multiple_of(j * c, c)
            o = of_ref[pl.ds(r, c), :] + ob_ref[pl.ds(r, c), :]
            y = o * lax.rsqrt(jnp.mean(o * o, axis=-1, keepdims=True) + RMS_EPS) * nw
            g = g_ref[pl.ds(r, c), :]
            y_ref[pl.ds(r, c), :] = (y * (g * _sigmoid(g))).astype(y_ref.dtype)
            return carry
        lax.fori_loop(0, n, body, 0)

    readout(ofc, obc, gc, yc_ref, ctx // c)
    readout(ofl, obl, gl, yl_ref, seq // c)


def hgrn_mixer(cfg, p, logits, norm_w, layer, out_rows):
    hd = HEAD_DIM
    nh = cfg.hg_heads
    col0 = cfg.off_hg // hd
    ctx_blk0 = cfg.nl // cfg.ctx
    (gf, upf, lvf), (gb, upb, lvb) = _hgrn_consts()

    def lat_spec(part):
        return pl.BlockSpec((cfg.seq, hd), lambda b, h, part=part: (b, col0 + part * nh + h))

    def ctx_spec(part):
        return pl.BlockSpec((cfg.ctx, hd), lambda b, h, part=part: (ctx_blk0 + b, col0 + part * nh + h))

    parts = (0, 1, 2, 3, 4)

    def const(arr):
        return pl.BlockSpec(arr.shape, lambda b, h, nd=arr.ndim: (0,) * nd)

    y_lat, y_ctx = pl.pallas_call(
        functools.partial(_hgrn_kernel, layer=layer, seq=cfg.seq, ctx=cfg.ctx),
        grid=(cfg.batch, nh),
        in_specs=[lat_spec(k) for k in parts] + [ctx_spec(k) for k in parts] + [
            pl.BlockSpec((cfg.depth, 2, hd), lambda b, h: (0, 0, h)),
            pl.BlockSpec((1, hd), lambda b, h: (0, 0)),
            const(gf), const(upf), const(lvf), const(gb), const(upb), const(lvb),
        ],
        out_specs=[
            pl.BlockSpec((cfg.seq, hd), lambda b, h: (b, h)),
            pl.BlockSpec((cfg.ctx, hd), lambda b, h: (b, h)),
        ],
        out_shape=[
            jax.ShapeDtypeStruct((out_rows, cfg.hgw), BF16),
            jax.ShapeDtypeStruct((cfg.nc, cfg.hgw), BF16),
        ],
        scratch_shapes=[
            pltpu.VMEM((cfg.seq, hd), F32), pltpu.VMEM((cfg.seq, hd), F32),
            pltpu.VMEM((cfg.ctx, hd), F32), pltpu.VMEM((cfg.ctx, hd), F32),
            pltpu.VMEM((hd, hd), F32), pltpu.VMEM((hd, hd), F32),
        ],
        compiler_params=_params(("arbitrary", "arbitrary")),
        name="hgrn_mixer",
    )(*([p] * 10), logits, norm_w.reshape(1, hd), gf, upf, lvf, gb, upb, lvb)
    return y_lat, y_ctx


def _pool_kernel(prev_ref, cur_ref, next_ref, w_ref, scale_ref, o_ref, buf_ref, *, seq_len, tile, n_tiles):
    i = pl.program_id(1)
    h = POOL_HALO
    cur = cur_ref[...]
    buf_ref[0:h, :] = jnp.where(i > 0, prev_ref[...], 0.0)
    buf_ref[h:h + tile, :] = cur
    buf_ref[h + tile:2 * h + tile, :] = jnp.where(i < n_tiles - 1, next_ref[...], 0.0)
    pos = i * tile + lax.broadcasted_iota(jnp.int32, (tile, POOL_GROUP), 0)
    for gi, win in enumerate(POOL_WINDOWS):
        half = win // 2
        cols = slice(gi * POOL_GROUP, (gi + 1) * POOL_GROUP)
        acc = buf_ref[h - half:h - half + tile, cols]
        for dlt in range(-half + 1, half):
            acc = acc + buf_ref[h + dlt:h + dlt + tile, cols]
        cnt = (jnp.minimum(pos + half, seq_len) - jnp.maximum(pos - half, 0)).astype(F32)
        pooled = acc / cnt - cur[:, cols]
        mixed = _dot(pooled.astype(BF16), w_ref[gi])
        o_ref[:, cols] = (mixed * scale_ref[:, cols]).astype(o_ref.dtype)


def pool_mixer(cfg, p, w_pool, scale, *, n_seq, seq_len, row0, out_rows=None):
    out_rows = n_seq * seq_len if out_rows is None else out_rows
    tile = _pick(seq_len, (512, 256, 128))
    n_tiles = seq_len // tile
    blk0 = row0 // tile
    col = cfg.off_pool // POOL_WIDTH
    per8 = tile // POOL_HALO
    last8 = cfg.nt // POOL_HALO - 1

    def cur_map(s, i):
        return (blk0 + s * n_tiles + i, col)

    def prev_map(s, i):
        return (jnp.maximum((blk0 + s * n_tiles + i) * per8 - 1, 0), col)

    def next_map(s, i):
        return (jnp.minimum((blk0 + s * n_tiles + i + 1) * per8, last8), col)

    return pl.pallas_call(
        functools.partial(_pool_kernel, seq_len=seq_len, tile=tile, n_tiles=n_tiles),
        grid=(n_seq, n_tiles),
        in_specs=[
            pl.BlockSpec((POOL_HALO, POOL_WIDTH), prev_map),
            pl.BlockSpec((tile, POOL_WIDTH), cur_map),
            pl.BlockSpec((POOL_HALO, POOL_WIDTH), next_map),
            pl.BlockSpec((len(POOL_WINDOWS), POOL_GROUP, POOL_GROUP), lambda s, i: (0, 0, 0)),
            pl.BlockSpec((1, POOL_WIDTH), lambda s, i: (0, 0)),
        ],
        out_specs=pl.BlockSpec((tile, POOL_WIDTH), lambda s, i: (s * n_tiles + i, 0)),
        out_shape=jax.ShapeDtypeStruct((out_rows, POOL_WIDTH), BF16),
        scratch_shapes=[pltpu.VMEM((tile + 2 * POOL_HALO, POOL_WIDTH), F32)],
        compiler_params=_params(("arbitrary", "arbitrary")),
        name="pool_mixer",
    )(p, p, p, w_pool, scale.reshape(1, POOL_WIDTH))


def _rope_slab(slab, cs):
    t = slab * cs
    r = t + pltpu.roll(t, ROPE_DIM, axis=1)
    lane = lax.broadcasted_iota(jnp.int32, r.shape, 1)
    return jnp.where(lane < ROPE_DIM, r, 0.0)


def _rms_bf16(x, w):
    return (x * lax.rsqrt(jnp.mean(x * x, axis=-1, keepdims=True) + RMS_EPS) * w).astype(BF16)


def _mla_q_kernel(cq_ref, nw_ref, cs_ref, w_ref, o_ref):
    n = _rms_bf16(cq_ref[...], nw_ref[...])
    cs = cs_ref[...]
    for h in range(w_ref.shape[0]):
        y = _dot(n, w_ref[h])
        rope = _rope_slab(y[:, HEAD_DIM:], cs)
        o_ref[h] = (jnp.concatenate([y[:, :HEAD_DIM], rope], axis=1) * QK_LOG2_SCALE).astype(o_ref.dtype)


def mla_queries(cfg, p, norm_w, w_q, cs):
    tm = _pick(cfg.nc, (512, 256))
    nh = cfg.mla_heads
    col = cfg.off_cq // cfg.q_lora
    return pl.pallas_call(
        _mla_q_kernel,
        grid=(cfg.nt // tm,),
        in_specs=[
            pl.BlockSpec((tm, cfg.q_lora), lambda i: (i, col)),
            pl.BlockSpec((1, cfg.q_lora), lambda i: (0, 0)),
            pl.BlockSpec((tm, HEAD_DIM), lambda i: (i, 0)),
            pl.BlockSpec((nh, cfg.q_lora, QK_DIM), lambda i: (0, 0, 0)),
        ],
        out_specs=pl.BlockSpec((nh, tm, QK_DIM), lambda i: (0, i, 0)),
        out_shape=jax.ShapeDtypeStruct((nh, cfg.nt, QK_DIM), BF16),
        compiler_params=_params(("arbitrary",)),
        name="mla_queries",
    )(p, norm_w.reshape(1, cfg.q_lora), cs, w_q)


def _mla_kv_kernel(ckv_ref, slab_ref, nw_ref, cs_ref, w_ref, k_ref, v_ref):
    n = _rms_bf16(ckv_ref[...], nw_ref[...])
    k_rope = _rope_slab(slab_ref[...], cs_ref[...])
    lane = lax.broadcasted_iota(jnp.int32, k_rope.shape, 1)
    ones_col = jnp.where(lane == 0, 1.0, 0.0)
    for h in range(w_ref.shape[0]):
        y = _dot(n, w_ref[h])
        k_ref[h] = jnp.concatenate([y[:, :HEAD_DIM], k_rope], axis=1).astype(k_ref.dtype)
        v_ref[h] = jnp.concatenate([y[:, HEAD_DIM:], ones_col], axis=1).astype(v_ref.dtype)


def mla_keys_values(cfg, p, norm_w, w_kv, cs):
    tm = _pick(cfg.nc, (512, 256))
    nh = cfg.mla_heads
    col = cfg.off_ckv // cfg.kv_lora
    slab_col = cfg.off_slab // HEAD_DIM
    return pl.pallas_call(
        _mla_kv_kernel,
        grid=(cfg.nt // tm,),
        in_specs=[
            pl.BlockSpec((tm, cfg.kv_lora), lambda i: (i, col)),
            pl.BlockSpec((tm, HEAD_DIM), lambda i: (i, slab_col)),
            pl.BlockSpec((1, cfg.kv_lora), lambda i: (0, 0)),
            pl.BlockSpec((tm, HEAD_DIM), lambda i: (i, 0)),
            pl.BlockSpec((nh, cfg.kv_lora, 2 * HEAD_DIM), lambda i: (0, 0, 0)),
        ],
        out_specs=[
            pl.BlockSpec((nh, tm, QK_DIM), lambda i: (0, i, 0)),
            pl.BlockSpec((nh, tm, 2 * HEAD_DIM), lambda i: (0, i, 0)),
        ],
        out_shape=[
            jax.ShapeDtypeStruct((nh, cfg.nt, QK_DIM), BF16),
            jax.ShapeDtypeStruct((nh, cfg.nt, 2 * HEAD_DIM), BF16),
        ],
        compiler_params=_params(("arbitrary",)),
        name="mla_keys_values",
    )(p, p, norm_w.reshape(1, cfg.kv_lora), cs, w_kv)


ATTN_SUB = 256
ATTN_KT = 512
ATTN_SCORE_ROWS = 1024


def _softmax_pv(s_list, v_list):
    m = functools.reduce(jnp.maximum, [jnp.max(s, axis=-1, keepdims=True) for s in s_list])
    o = None
    for s, v in zip(s_list, v_list):
        part = _dot(jnp.exp2(s - m).astype(BF16), v)
        o = part if o is None else o + part
    return o[:, :HEAD_DIM] / o[:, HEAD_DIM:HEAD_DIM + 1]


def _attn_lat_kernel(q_ref, kl_ref, kc_ref, vl_ref, vc_ref, o_ref, s_ref):
    n_keys = kl_ref.shape[0]
    n_slots = s_ref.shape[0] // ATTN_SUB
    for r in range(0, q_ref.shape[0], ATTN_SUB):
        rows = slice(r, r + ATTN_SUB)
        slot = (r // ATTN_SUB) % n_slots
        srows = slice(slot * ATTN_SUB, (slot + 1) * ATTN_SUB)
        q = q_ref[rows, :]
        sc = _dot_nt(q, kc_ref[...])
        m = jnp.max(sc, axis=-1, keepdims=True)
        for c in range(0, n_keys, ATTN_KT):
            s = _dot_nt(q, kl_ref[c:c + ATTN_KT, :])
            s_ref[srows, c:c + ATTN_KT] = s
            m = jnp.maximum(m, jnp.max(s, axis=-1, keepdims=True))
        o = _dot(jnp.exp2(sc - m).astype(BF16), vc_ref[...])
        for c in range(0, n_keys, ATTN_KT):
            p = jnp.exp2((s_ref[srows, c:c + ATTN_KT] - m).astype(BF16))
            o = o + _dot(p, vl_ref[c:c + ATTN_KT, :])
        o_ref[rows, :] = (o[:, :HEAD_DIM] / o[:, HEAD_DIM:HEAD_DIM + 1]).astype(o_ref.dtype)


def _attn_ctx_kernel(q_ref, kc_ref, vc_ref, o_ref):
    o_ref[...] = _softmax_pv([_dot_nt(q_ref[...], kc_ref[...])], [vc_ref[...]]).astype(o_ref.dtype)


def mla_attention(cfg, q, k, v, *, with_ctx, out_rows):
    nh = cfg.mla_heads
    vw = 2 * HEAD_DIM
    tq = _pick(cfg.seq, (2048, 1024, 512, 256))
    nq = cfg.seq // tq
    s_rows = min(tq, ATTN_SCORE_ROWS)
    cb0 = cfg.nl // cfg.ctx
    y_lat = pl.pallas_call(
        _attn_lat_kernel,
        grid=(cfg.batch, nh, nq),
        in_specs=[
            pl.BlockSpec((None, tq, QK_DIM), lambda b, h, i: (h, b * nq + i, 0)),
            pl.BlockSpec((None, cfg.seq, QK_DIM), lambda b, h, i: (h, b, 0)),
            pl.BlockSpec((None, cfg.ctx, QK_DIM), lambda b, h, i: (h, cb0 + b, 0)),
            pl.BlockSpec((None, cfg.seq, vw), lambda b, h, i: (h, b, 0)),
            pl.BlockSpec((None, cfg.ctx, vw), lambda b, h, i: (h, cb0 + b, 0)),
        ],
        out_specs=pl.BlockSpec((tq, HEAD_DIM), lambda b, h, i: (b * nq + i, h)),
        out_shape=jax.ShapeDtypeStruct((out_rows, nh * HEAD_DIM), BF16),
        scratch_shapes=[pltpu.VMEM((s_rows, cfg.seq), F32)],
        compiler_params=_params(("arbitrary", "arbitrary", "arbitrary")),
        name="mla_attention",
    )(q, k, k, v, v)
    if not with_ctx:
        return y_lat, None
    y_ctx = pl.pallas_call(
        _attn_ctx_kernel,
        grid=(cfg.batch, nh),
        in_specs=[
            pl.BlockSpec((None, cfg.ctx, QK_DIM), lambda b, h: (h, cb0 + b, 0)),
            pl.BlockSpec((None, cfg.ctx, QK_DIM), lambda b, h: (h, cb0 + b, 0)),
            pl.BlockSpec((None, cfg.ctx, vw), lambda b, h: (h, cb0 + b, 0)),
        ],
        out_specs=pl.BlockSpec((cfg.ctx, HEAD_DIM), lambda b, h: (b, h)),
        out_shape=jax.ShapeDtypeStruct((cfg.nc, nh * HEAD_DIM), BF16),
        compiler_params=_params(("arbitrary", "arbitrary")),
        name="mla_attention_ctx",
    )(q, k, v)
    return y_lat, y_ctx


def _merge_kernel(ya_ref, yb_ref, yc_ref, ga_ref, gb_ref, gc_ref, wa_ref, wb_ref, wc_ref, o_ref):
    m = (_sigmoid(ga_ref[...]) * _dot(ya_ref[...], wa_ref[...])
         + _sigmoid(gb_ref[...]) * _dot(yb_ref[...], wb_ref[...])
         + _sigmoid(gc_ref[...]) * _dot(yc_ref[...], wc_ref[...]))
    o_ref[...] = m.astype(o_ref.dtype)


def merge_branches(cfg, ya, yb, yc, p, wa, wb, wc, *, rows):
    tm = _pick(cfg.nc, (512, 256))
    tn = _pick(cfg.d, (512, 256))
    gcols = cfg.d // tn

    def gate_spec(k):
        return pl.BlockSpec((tm, tn), lambda i, j, k=k: (i, k * gcols + j))

    return pl.pallas_call(
        _merge_kernel,
        grid=(rows // tm, cfg.d // tn),
        in_specs=[
            pl.BlockSpec((tm, ya.shape[1]), lambda i, j: (i, 0)),
            pl.BlockSpec((tm, yb.shape[1]), lambda i, j: (i, 0)),
            pl.BlockSpec((tm, yc.shape[1]), lambda i, j: (i, 0)),
            gate_spec(0), gate_spec(1), gate_spec(2),
            pl.BlockSpec((wa.shape[0], tn), lambda i, j: (0, j)),
            pl.BlockSpec((wb.shape[0], tn), lambda i, j: (0, j)),
            pl.BlockSpec((wc.shape[0], tn), lambda i, j: (0, j)),
        ],
        out_specs=pl.BlockSpec((tm, tn), lambda i, j: (i, j)),
        out_shape=jax.ShapeDtypeStruct((rows, cfg.d), BF16),
        compiler_params=_params(("arbitrary", "arbitrary")),
        name="merge_branches",
    )(ya, yb, yc, p, p, p, wa, wb, wc)


def _matmul_resid_kernel(a_ref, w_ref, x_ref, mod_ref, o_ref, *, k):
    o_ref[...] = x_ref[...] + mod_ref[k:k + 1, :] * _dot(a_ref[...], w_ref[...])


def matmul_residual(cfg, a, w, x, mod_l, *, k, rows):
    tm = _pick(cfg.nc, (1024, 512, 256))
    tn = _pick(cfg.d, (512, 256))
    row_of = _mod_row_map(cfg, tm)
    return pl.pallas_call(
        functools.partial(_matmul_resid_kernel, k=k),
        grid=(rows // tm, cfg.d // tn),
        in_specs=[
            pl.BlockSpec((tm, a.shape[1]), lambda i, j: (i, 0)),
            pl.BlockSpec((a.shape[1], tn), lambda i, j: (0, j)),
            pl.BlockSpec((tm, tn), lambda i, j: (i, j)),
            pl.BlockSpec((None, 6, tn), lambda i, j: (row_of(i), 0, j)),
        ],
        out_specs=pl.BlockSpec((tm, tn), lambda i, j: (i, j)),
        out_shape=jax.ShapeDtypeStruct((rows, cfg.d), F32),
        compiler_params=_params(("arbitrary", "arbitrary")),
        name="matmul_residual",
    )(a, w, x, mod_l)


def _top16_rows(s, top_ref):
    n = s.shape[0]
    row = lax.broadcasted_iota(jnp.int32, s.shape, 0).astype(F32)
    rank = jnp.full(s.shape, UNRANKED, F32)
    work = s
    for r in range(PEER_TOPK):
        m = jnp.max(work, axis=0, keepdims=True)
        first = jnp.min(jnp.where(work == m, row, float(n)), axis=0, keepdims=True)
        sel = row == first
        rank = jnp.where(sel, float(r), rank)
        work = jnp.where(sel, -jnp.inf, work)
        top_ref[r:r + 1, :] = m
    return rank


def _peer_tables_kernel(qh_ref, keys_ref, b_ref, e2_ref, c_ref, e1_ref, top1_ref, top2_ref, cnt_ref, *, tile):
    lanes = HEAD_DIM
    k1 = keys_ref[0]
    k2 = keys_ref[1]
    for part in range(tile // lanes):
        rows = slice(part * lanes, (part + 1) * lanes)
        cols = slice(part * lanes, (part + 1) * lanes)
        qh = qh_ref[rows, :].astype(BF16)
        s1 = _dot_nt(k1, qh[:, :lanes])
        s2 = _dot_nt(k2, qh[:, lanes:])
        rank1 = _top16_rows(s1, top1_ref)
        rank2 = _top16_rows(s2, top2_ref)
        t1 = top1_ref[...]
        t2 = top2_ref[...]
        pieces = [t1[0:1] + t2]
        pos = [lax.broadcasted_iota(jnp.int32, (PEER_TOPK, lanes), 0).astype(F32)]
        for a in range(1, 8):
            pieces.append(t1[a:a + 1] + t2[0:8])
            pos.append(lax.broadcasted_iota(jnp.int32, (8, lanes), 0).astype(F32) + float(a * PEER_TOPK))
        pieces.append(t1[8:16] + t2[0:1])
        pos.append((lax.broadcasted_iota(jnp.int32, (8, lanes), 0).astype(F32) + 8.0) * float(PEER_TOPK))
        cand = jnp.concatenate(pieces, axis=0)
        cpos = jnp.concatenate(pos, axis=0)
        a_row = lax.broadcasted_iota(jnp.int32, (PEER_TOPK, lanes), 0).astype(F32)
        cnt = jnp.zeros((PEER_TOPK, lanes), F32)
        z = jnp.zeros((1, lanes), F32)
        best0 = None
        for r in range(PEER_TOPK):
            m = jnp.max(cand, axis=0, keepdims=True)
            first = jnp.min(jnp.where(cand == m, cpos, 1e9), axis=0, keepdims=True)
            cand = jnp.where(cpos == first, -jnp.inf, cand)
            cnt = cnt + jnp.where(a_row == jnp.floor(first * (1.0 / PEER_TOPK)), 1.0, 0.0)
            if r == 0:
                best0 = m
            z = z + jnp.exp(m - best0)
        cnt_ref[...] = cnt
        c_tab = jnp.zeros((PEER_NKEYS, lanes), F32)
        for a in range(PEER_TOPK):
            c_tab = jnp.where(rank1 == float(a), cnt_ref[a:a + 1, :], c_tab)
        b_ref[:, cols] = rank2.astype(b_ref.dtype)
        c_ref[:, cols] = c_tab
        e1_ref[:, cols] = jnp.exp(s1 - t1[0:1]) / z
        e2_ref[:, cols] = jnp.exp(s2 - t2[0:1]).astype(e2_ref.dtype)


def peer_tables(cfg, qh, keys, *, rows):
    tile = 256
    nh = cfg.peer_heads
    shapes = [jax.ShapeDtypeStruct((nh, PEER_NKEYS, rows), dt) for dt in (BF16, BF16, F32, F32)]
    out_spec = pl.BlockSpec((None, PEER_NKEYS, tile), lambda i, h: (h, 0, i))
    return pl.pallas_call(
        functools.partial(_peer_tables_kernel, tile=tile),
        grid=(rows // tile, nh),
        in_specs=[
            pl.BlockSpec((tile, 2 * HEAD_DIM), lambda i, h: (i, h)),
            pl.BlockSpec((2, None, PEER_NKEYS, HEAD_DIM), lambda i, h: (0, h, 0, 0)),
        ],
        out_specs=[out_spec] * 4,
        out_shape=shapes,
        scratch_shapes=[pltpu.VMEM((PEER_TOPK, HEAD_DIM), F32) for _ in range(3)],
        compiler_params=_params(("arbitrary", "arbitrary")),
        name="peer_tables",
    )(qh, keys)


PEER_ROWS = 8
PEER_EB = PEER_ROWS * PEER_NKEYS
PEER_DOT_ROWS = 1


def _peer_dense_kernel(ht_ref, u_ref, vt_ref, b_ref, e2_ref, c_ref, e1_ref, x_ref, mod_ref, o_ref,
                       acc_ref, w_ref, *, heads):
    e = pl.program_id(1)

    @pl.when(e == 0)
    def _():
        acc_ref[...] = jnp.zeros_like(acc_ref)

    zero = jnp.zeros((), BF16)
    ht = ht_ref[...]
    for r0 in range(0, PEER_ROWS, PEER_DOT_ROWS):
        blk = slice(r0 * PEER_NKEYS, (r0 + PEER_DOT_ROWS) * PEER_NKEYS)
        act = _dot(u_ref[blk, :], ht)
        gelu = (0.5 * act * (1.0 + lax.erf(act * float(math.sqrt(0.5))))).astype(BF16)
        for q in range(PEER_DOT_ROWS):
            r = r0 + q
            g = jnp.zeros((PEER_NKEYS, act.shape[1]), BF16)
            for h in range(heads):
                crow = jnp.broadcast_to(c_ref[h, r:r + 1, :], g.shape).astype(BF16)
                erow = jnp.broadcast_to(e1_ref[h, r:r + 1, :], g.shape).astype(BF16)
                g = g + jnp.where(b_ref[h] < crow, e2_ref[h] * erow, zero)
            w_ref[r * PEER_NKEYS:(r + 1) * PEER_NKEYS, :] = g * gelu[q * PEER_NKEYS:(q + 1) * PEER_NKEYS]
    acc_ref[...] += _dot(vt_ref[...], w_ref[...])

    @pl.when(e == pl.num_programs(1) - 1)
    def _():
        o_ref[...] = x_ref[...] + mod_ref[5:6, :] * acc_ref[...].T


def peer_dense(cfg, ht, u, vt, tabs, x, mod_l, *, rows):
    tile = _pick(cfg.nc, (512, 256))
    eb = PEER_EB
    n_exp = u.shape[0]
    nh = cfg.peer_heads
    row_of = _mod_row_map(cfg, tile)
    tab_spec = pl.BlockSpec((nh, PEER_NKEYS, tile), lambda i, e: (0, 0, i))
    row_spec = pl.BlockSpec((nh, PEER_ROWS, tile), lambda i, e: (0, e, i))
    return pl.pallas_call(
        functools.partial(_peer_dense_kernel, heads=nh),
        grid=(rows // tile, n_exp // eb),
        in_specs=[
            pl.BlockSpec((cfg.d, tile), lambda i, e: (0, i)),
            pl.BlockSpec((eb, cfg.d), lambda i, e: (e, 0)),
            pl.BlockSpec((cfg.d, eb), lambda i, e: (0, e)),
            tab_spec, tab_spec, row_spec, row_spec,
            pl.BlockSpec((tile, cfg.d), lambda i, e: (i, 0)),
            pl.BlockSpec((None, 6, cfg.d), lambda i, e: (row_of(i), 0, 0)),
        ],
        out_specs=pl.BlockSpec((tile, cfg.d), lambda i, e: (i, 0)),
        out_shape=jax.ShapeDtypeStruct((rows, cfg.d), F32),
        scratch_shapes=[pltpu.VMEM((cfg.d, tile), F32), pltpu.VMEM((eb, tile), BF16)],
        compiler_params=_params(("arbitrary", "arbitrary")),
        name="peer_dense",
    )(ht, u, vt, *tabs, x, mod_l)


def _final_norm_kernel(x_ref, w_ref, o_ref):
    x = x_ref[...]
    o_ref[...] = x * lax.rsqrt(jnp.mean(x * x, axis=-1, keepdims=True) + RMS_EPS) * w_ref[...]


def final_norm(cfg, x, w):
    tm = _pick(cfg.nl, (1024, 512, 256))
    return pl.pallas_call(
        _final_norm_kernel,
        grid=(cfg.nl // tm,),
        in_specs=[pl.BlockSpec((tm, cfg.d), lambda i: (i, 0)), pl.BlockSpec((1, cfg.d), lambda i: (0, 0))],
        out_specs=pl.BlockSpec((tm, cfg.d), lambda i: (i, 0)),
        out_shape=jax.ShapeDtypeStruct((cfg.nl, cfg.d), F32),
        compiler_params=_params(("arbitrary",)),
        name="final_norm",
    )(x, w.reshape(1, cfg.d))


def _rot_cols(w):
    q = ROPE_DIM // 4
    return jnp.concatenate([-w[..., q:2 * q], w[..., 0:q], -w[..., 3 * q:4 * q], w[..., 2 * q:3 * q]], axis=-1)


def _in_proj_weight(cfg, w_in, n_cols):
    d = cfg.d
    hg_end = 5 * cfg.hgw
    pool_end = hg_end + POOL_WIDTH
    cq_end = pool_end + cfg.q_lora
    ckv_end = cq_end + cfg.kv_lora
    rope_end = ckv_end + ROPE_DIM
    k_rope = w_in[:, ckv_end:rope_end]
    parts = [w_in[:, rope_end:rope_end + 3 * d], w_in[:, :ckv_end], k_rope, _rot_cols(k_rope)]
    parts.append(jnp.zeros((d, n_cols - cfg.in_cols), w_in.dtype))
    return jnp.concatenate([part.astype(BF16) for part in parts], axis=1)


def _mla_q_weight(cfg, w_uq):
    w = w_uq.reshape(cfg.q_lora, cfg.mla_heads, HEAD_DIM + ROPE_DIM)
    rope = w[..., HEAD_DIM:]
    w = jnp.concatenate([w[..., :HEAD_DIM], rope, _rot_cols(rope)], axis=-1)
    return jnp.transpose(w, (1, 0, 2)).astype(BF16)


def _mla_kv_weight(cfg, w_ukv):
    w = w_ukv.reshape(cfg.kv_lora, cfg.mla_heads, 2 * HEAD_DIM)
    return jnp.transpose(w, (1, 0, 2)).astype(BF16)


def _rope_table(cfg):
    rows = cfg.seq // cfg.grid_w
    r, col = jnp.meshgrid(jnp.arange(rows), jnp.arange(cfg.grid_w), indexing="ij")
    n_freq = ROPE_DIM // 4
    freqs = ROPE_THETA ** (-jnp.arange(n_freq, dtype=F32) / n_freq)
    ang_r = r.reshape(-1)[:, None] * freqs
    ang_c = col.reshape(-1)[:, None] * freqs
    cos = jnp.concatenate([jnp.cos(ang_r)] * 2 + [jnp.cos(ang_c)] * 2, axis=1)
    sin = jnp.concatenate([jnp.sin(ang_r)] * 2 + [jnp.sin(ang_c)] * 2, axis=1)
    lat = jnp.tile(jnp.concatenate([cos, sin], axis=1).astype(F32), (cfg.batch, 1))
    ctx = jnp.concatenate([jnp.ones((cfg.nc, ROPE_DIM), F32), jnp.zeros((cfg.nc, ROPE_DIM), F32)], axis=1)
    return jnp.concatenate([lat, ctx], axis=0)


def _forward(cfg, x, c, ctx, c_ctx, w_mod, b_mod, norm_mix, norm_ffn, w_in, hg_lb_logits, hg_norm,
             pool_w, pool_scale, mla_q_norm, mla_w_uq, mla_kv_norm, mla_w_ukv,
             w_branch_a, w_branch_b, w_branch_c, w_out, peer_wq, peer_keys, peer_u, peer_v, final_w):
    d = cfg.d
    assert cfg.seq % HG_CHUNK == 0 and cfg.ctx % HG_CHUNK == 0 and cfg.nl % cfg.ctx == 0
    assert cfg.off_pool % POOL_WIDTH == 0 and cfg.off_cq % cfg.q_lora == 0
    assert cfg.off_ckv % cfg.kv_lora == 0 and cfg.batch < MOD_ROWS
    tn_in = 768 if d % 256 == 0 and cfg.in_cols > 8192 else 256
    n_cols = -(-cfg.in_cols // tn_in) * tn_in
    tm = _pick(cfg.nc, (1024, 512, 256))

    xs = jnp.concatenate([x.reshape(cfg.nl, d), ctx.reshape(cfg.nc, d)], axis=0)
    c_all = jnp.concatenate([c, c_ctx[None], jnp.zeros((MOD_ROWS - cfg.batch - 1, d), F32)], axis=0)
    mod = adaln_tables(cfg, c_all, w_mod, b_mod)
    cs = _rope_table(cfg)

    for l in range(cfg.depth):
        last = l == cfg.depth - 1
        rows = cfg.nl if last else cfg.nt
        mod_l = mod[l]
        p = norm_matmul(cfg, xs, norm_mix[l], mod_l, _in_proj_weight(cfg, w_in[l], n_cols),
                        k0=0, rows=cfg.nt, tm=tm, tn=tn_in, emit_h=False)
        ya, hg_ctx = hgrn_mixer(cfg, p, hg_lb_logits, hg_norm[l], l, rows)
        pw = pool_w[l].astype(BF16)
        yb = pool_mixer(cfg, p, pw, pool_scale[l], n_seq=cfg.batch, seq_len=cfg.seq, row0=0, out_rows=rows)
        q = mla_queries(cfg, p, mla_q_norm[l], _mla_q_weight(cfg, mla_w_uq[l]), cs)
        k, v = mla_keys_values(cfg, p, mla_kv_norm[l], _mla_kv_weight(cfg, mla_w_ukv[l]), cs)
        yc, att_ctx = mla_attention(cfg, q, k, v, with_ctx=not last, out_rows=rows)
        if not last:
            pool_ctx = pool_mixer(cfg, p, pw, pool_scale[l], n_seq=cfg.batch, seq_len=cfg.ctx, row0=cfg.nl)
            ya = lax.dynamic_update_slice(ya, hg_ctx, (cfg.nl, 0))
            yb = lax.dynamic_update_slice(yb, pool_ctx, (cfg.nl, 0))
            yc = lax.dynamic_update_slice(yc, att_ctx, (cfg.nl, 0))
        m = merge_branches(cfg, ya, yb, yc, p, w_branch_a[l].astype(BF16), w_branch_b[l].astype(BF16),
                           w_branch_c[l].astype(BF16), rows=rows)
        xs = matmul_residual(cfg, m, w_out[l].astype(BF16), xs, mod_l, k=2, rows=rows)
        qh, h2 = norm_matmul(cfg, xs, norm_ffn[l], mod_l, peer_wq[l].astype(BF16),
                             k0=3, rows=rows, tm=tm, tn=_pick(peer_wq.shape[2], (512, 256)), emit_h=True)
        tabs = peer_tables(cfg, qh, peer_keys[l].astype(BF16), rows=rows)
        xs = peer_dense(cfg, h2.T, peer_u[l].astype(BF16), peer_v[l].astype(BF16).T, tabs, xs, mod_l, rows=rows)

    return final_norm(cfg, xs, final_w).reshape(cfg.batch, cfg.seq, d)


def kernel(x, c, ctx, c_ctx, w_mod, b_mod, norm_mix, norm_ffn, w_in, hg_lb_logits, hg_norm, pool_w, pool_scale,
           mla_q_norm, mla_w_uq, mla_kv_norm, mla_w_ukv, w_branch_a, w_branch_b, w_branch_c, w_out,
           peer_wq, peer_keys, peer_u, peer_v, final_norm):
    batch, seq, d = x.shape
    cfg = Cfg(d=d, batch=batch, seq=seq, ctx=ctx.shape[1], grid_w=64, depth=w_mod.shape[0],
              hg_heads=hg_lb_logits.shape[2] // HEAD_DIM,
              mla_heads=mla_w_ukv.shape[2] // (2 * HEAD_DIM), q_lora=mla_q_norm.shape[1],
              kv_lora=mla_kv_norm.shape[1], peer_heads=peer_keys.shape[2])
    return _forward(cfg, x, c, ctx, c_ctx, w_mod, b_mod, norm_mix, norm_ffn, w_in, hg_lb_logits, hg_norm,
                    pool_w, pool_scale, mla_q_norm, mla_w_uq, mla_kv_norm, mla_w_ukv,
                    w_branch_a, w_branch_b, w_branch_c, w_out, peer_wq, peer_keys, peer_u, peer_v, final_norm)
```

```python
import functools
import math
from typing import NamedTuple

import numpy as np
import jax
import jax.numpy as jnp
from jax import lax
from jax.experimental import pallas as pl
from jax.experimental.pallas import tpu as pltpu

F32 = jnp.float32
BF16 = jnp.bfloat16

RMS_EPS = 1e-6
ROPE_THETA = 10000.0
HEAD_DIM = 128
ROPE_DIM = 64
QK_DIM = 256
LOG2_E = float(math.log2(math.e))
QK_LOG2_SCALE = float((HEAD_DIM + ROPE_DIM) ** -0.5) * LOG2_E
POOL_WINDOWS = (2, 4, 8, 16)
POOL_GROUP = 256
POOL_WIDTH = POOL_GROUP * len(POOL_WINDOWS)
POOL_HALO = 8
PEER_NKEYS = 128
PEER_TOPK = 16
HG_CHUNK = 128
HG_LEVELS = 7
UNRANKED = 99.0
MOD_ROWS = 8
VMEM_LIMIT = 56 * 1024 * 1024


class Cfg(NamedTuple):
    d: int
    batch: int
    seq: int
    ctx: int
    grid_w: int
    depth: int
    hg_heads: int
    mla_heads: int
    q_lora: int
    kv_lora: int
    peer_heads: int

    @property
    def nl(self):
        return self.batch * self.seq

    @property
    def nc(self):
        return self.batch * self.ctx

    @property
    def nt(self):
        return self.nl + self.nc

    @property
    def hgw(self):
        return self.hg_heads * HEAD_DIM

    @property
    def off_gate(self):
        return 0

    @property
    def off_hg(self):
        return 3 * self.d

    @property
    def off_pool(self):
        return self.off_hg + 5 * self.hgw

    @property
    def off_cq(self):
        return self.off_pool + POOL_WIDTH

    @property
    def off_ckv(self):
        return self.off_cq + self.q_lora

    @property
    def off_slab(self):
        return self.off_ckv + self.kv_lora

    @property
    def in_cols(self):
        return self.off_slab + HEAD_DIM


def _params(sem):
    return pltpu.CompilerParams(dimension_semantics=sem, vmem_limit_bytes=VMEM_LIMIT)


def _dot(a, b):
    return jnp.dot(a, b, preferred_element_type=F32)


def _dot_nt(a, b):
    return lax.dot_general(a, b, (((1,), (1,)), ((), ())), preferred_element_type=F32)


def _sigmoid(x):
    return 1.0 / (1.0 + jnp.exp(-x))


def _pick(n, prefs):
    for p in prefs:
        if n % p == 0:
            return p
    raise ValueError(f"no tile for {n} in {prefs}")


def _mod_row_map(cfg, tm):
    n_lat = cfg.nl // tm
    per_batch = cfg.seq // tm
    return lambda i: jnp.where(i < n_lat, i // per_batch, cfg.batch)


def _mod_kernel(c_ref, w_ref, b_ref, o_ref):
    c = c_ref[...]
    s = (c * _sigmoid(c)).astype(BF16)
    o_ref[...] = _dot(s, w_ref[...].astype(BF16)) + b_ref[...]


def adaln_tables(cfg, c_all, w_mod, b_mod):
    d6 = 6 * cfg.d
    tn = _pick(d6, (1024, 768, 512, 256))
    out = pl.pallas_call(
        _mod_kernel,
        grid=(cfg.depth, d6 // tn),
        in_specs=[
            pl.BlockSpec((MOD_ROWS, cfg.d), lambda l, j: (0, 0)),
            pl.BlockSpec((None, cfg.d, tn), lambda l, j: (l, 0, j)),
            pl.BlockSpec((None, 1, tn), lambda l, j: (l, 0, j)),
        ],
        out_specs=pl.BlockSpec((None, MOD_ROWS, tn), lambda l, j: (l, 0, j)),
        out_shape=jax.ShapeDtypeStruct((cfg.depth, MOD_ROWS, d6), F32),
        compiler_params=_params(("arbitrary", "arbitrary")),
        name="adaln_tables",
    )(c_all, w_mod, b_mod.reshape(cfg.depth, 1, d6))
    return out.reshape(cfg.depth, MOD_ROWS, 6, cfg.d)


def _norm_matmul_kernel(x_ref, gain_ref, mod_ref, w_ref, *rest, k0, emit_h):
    if emit_h:
        o_ref, hout_ref, h_ref = rest
    else:
        o_ref, h_ref = rest

    @pl.when(pl.program_id(1) == 0)
    def _():
        x = x_ref[...]
        y = x * lax.rsqrt(jnp.mean(x * x, axis=-1, keepdims=True) + RMS_EPS) * gain_ref[...]
        h = (y * (1.0 + mod_ref[k0 + 1:k0 + 2, :]) + mod_ref[k0:k0 + 1, :]).astype(BF16)
        h_ref[...] = h
        if emit_h:
            hout_ref[...] = h

    o_ref[...] = _dot(h_ref[...], w_ref[...])


def norm_matmul(cfg, x, gain, mod_l, w, *, k0, rows, tm, tn, emit_h):
    n = w.shape[1]
    row_of = _mod_row_map(cfg, tm)
    out_shape = [jax.ShapeDtypeStruct((rows, n), F32)]
    out_specs = [pl.BlockSpec((tm, tn), lambda i, j: (i, j))]
    if emit_h:
        out_shape.append(jax.ShapeDtypeStruct((rows, cfg.d), BF16))
        out_specs.append(pl.BlockSpec((tm, cfg.d), lambda i, j: (i, 0)))
    res = pl.pallas_call(
        functools.partial(_norm_matmul_kernel, k0=k0, emit_h=emit_h),
        grid=(rows // tm, n // tn),
        in_specs=[
            pl.BlockSpec((tm, cfg.d), lambda i, j: (i, 0)),
            pl.BlockSpec((1, cfg.d), lambda i, j: (0, 0)),
            pl.BlockSpec((None, 6, cfg.d), lambda i, j: (row_of(i), 0, 0)),
            pl.BlockSpec((cfg.d, tn), lambda i, j: (0, j)),
        ],
        out_specs=out_specs,
        out_shape=out_shape,
        scratch_shapes=[pltpu.VMEM((tm, cfg.d), BF16)],
        compiler_params=_params(("arbitrary", "arbitrary")),
        name="norm_matmul_h" if emit_h else "norm_matmul",
    )(x, gain.reshape(1, cfg.d), mod_l, w)
    return res if emit_h else res[0]


def _hgrn_consts():
    c = HG_CHUNK
    out = []
    for rev in (False, True):
        p = np.arange(c) if not rev else c - 1 - np.arange(c)
        pt, pu = p[:, None], p[None, :]
        g = np.zeros((HG_LEVELS + 1, c, c), np.float32)
        up = np.zeros((HG_LEVELS, c, c), np.float32)
        g[0] = pu <= pt
        for l in range(HG_LEVELS):
            m = 1 << l
            blk = p >> (l + 1)
            upper = ((p >> l) & 1) == 1
            mid = (blk * 2 * m + m)[:, None]
            same = blk[:, None] == blk[None, :]
            g_up = same & (pu >= mid) & (pu <= pt)
            g_lo = same & (pu > pt) & (pu < mid)
            g[1 + l] = np.where(upper[:, None], g_up, g_lo)
            up[l] = np.broadcast_to(upper[:, None], (c, c))
        x = pt ^ pu
        lv = np.where(pu < pt, np.floor(np.log2(np.maximum(x, 1))), -1.0).astype(np.float32)
        out.append((jnp.asarray(g.reshape(-1, c), BF16), jnp.asarray(up, F32), jnp.asarray(lv, F32)))
    return out


def _hgrn_chunk(q_raw, v, f_raw, log_lb, log_1mlb, one_m_lb, g_ref, up_ref, lv_ref, st_ref, end_row):
    c = HG_CHUNK
    q = q_raw * _sigmoid(q_raw)
    e = jnp.exp(-jnp.abs(f_raw))
    one_pe = 1.0 + e
    log_sig = jnp.minimum(f_raw, 0.0) - jnp.log(one_pe)
    t = log_1mlb + log_sig
    log_f = jnp.maximum(log_lb, t) + jnp.log(1.0 + jnp.exp(-jnp.abs(log_lb - t)))
    k = one_m_lb * jnp.where(f_raw >= 0.0, e, 1.0) / one_pe
    hi = log_f.astype(BF16)
    lo = (log_f - hi.astype(F32)).astype(BF16)
    a2 = _dot(g_ref[...], jnp.concatenate([hi, lo], axis=1))
    a = a2[:, :HEAD_DIM] + a2[:, HEAD_DIM:]
    b = a[0:c]
    lv = lv_ref[...]
    scores = jnp.zeros((c, c), F32)
    for l in range(HG_LEVELS):
        e_l = jnp.exp(a[(1 + l) * c:(2 + l) * c])
        x = (jnp.where(up_ref[l] > 0.5, q, k) * e_l).astype(BF16)
        scores = jnp.where(lv == float(l), _dot_nt(x, x), scores)
    b_end = b[end_row:end_row + 1, :]
    qb = (q * jnp.exp(b)).astype(BF16)
    kd = (k * jnp.exp(b_end - b)).astype(BF16)
    st = st_ref[...]
    vb = v.astype(BF16)
    o = (_dot(scores.astype(BF16), vb) + _dot_nt(qb, st.astype(BF16))
         + jnp.sum(q * k, axis=-1, keepdims=True) * v)
    st_ref[...] = st * jnp.exp(b_end) + _dot(v.T.astype(BF16), kd)
    return o


def _hgrn_kernel(ql, ffl, fbl, il, gl, qc, ffc, fbc, ic, gc, logit_ref, nw_ref,
                 gf_ref, upf_ref, lvf_ref, gb_ref, upb_ref, lvb_ref,
                 yl_ref, yc_ref, ofl, obl, ofc, obc, stf, stb, *, layer, seq, ctx):
    c = HG_CHUNK
    depth = logit_ref.shape[0]
    lg = [logit_ref[dd] for dd in range(depth)]
    mx = functools.reduce(jnp.maximum, lg)
    ex = [jnp.exp(v - mx) for v in lg]
    tot = functools.reduce(jnp.add, ex)
    cum = [ex[0] / tot]
    for dd in range(1, layer + 1):
        cum.append(cum[-1] + ex[dd] / tot)
    lb = cum[layer] - cum[0]
    log_lb = jnp.log(lb)
    log_1mlb = jnp.log1p(-lb)
    one_m_lb = 1.0 - lb

    stf[...] = jnp.zeros_like(stf)
    stb[...] = jnp.zeros_like(stb)

    def segment(q_ref, ff_ref, fb_ref, i_ref, of_ref, ob_ref, n):
        def body(j, carry):
            rf = pl.multiple_of(j * c, c)
            rb = pl.multiple_of((n - 1 - j) * c, c)
            of_ref[pl.ds(rf, c), :] = _hgrn_chunk(
                q_ref[pl.ds(rf, c), :], i_ref[pl.ds(rf, c), :], ff_ref[pl.ds(rf, c), :],
                log_lb[0:1], log_1mlb[0:1], one_m_lb[0:1], gf_ref, upf_ref, lvf_ref, stf, c - 1)
            ob_ref[pl.ds(rb, c), :] = _hgrn_chunk(
                q_ref[pl.ds(rb, c), :], i_ref[pl.ds(rb, c), :], fb_ref[pl.ds(rb, c), :],
                log_lb[1:2], log_1mlb[1:2], one_m_lb[1:2], gb_ref, upb_ref, lvb_ref, stb, 0)
            return carry
        lax.fori_loop(0, n, body, 0, unroll=4 if n % 4 == 0 else 2)

    segment(qc, ffc, fbc, ic, ofc, obc, ctx // c)
    segment(ql, ffl, fbl, il, ofl, obl, seq // c)

    nw = nw_ref[...]

    def readout(of_ref, ob_ref, g_ref, y_ref, n):
        def body(j, carry):
            r = pl.multiple_of(j * c, c)
            o = of_ref[pl.ds(r, c), :] + ob_ref[pl.ds(r, c), :]
            y = o * lax.rsqrt(jnp.mean(o * o, axis=-1, keepdims=True) + RMS_EPS) * nw
            g = g_ref[pl.ds(r, c), :]
            y_ref[pl.ds(r, c), :] = (y * (g * _sigmoid(g))).astype(y_ref.dtype)
            return carry
        lax.fori_loop(0, n, body, 0)

    readout(ofc, obc, gc, yc_ref, ctx // c)
    readout(ofl, obl, gl, yl_ref, seq // c)


def hgrn_mixer(cfg, p, logits, norm_w, layer, out_rows):
    hd = HEAD_DIM
    nh = cfg.hg_heads
    col0 = cfg.off_hg // hd
    ctx_blk0 = cfg.nl // cfg.ctx
    (gf, upf, lvf), (gb, upb, lvb) = _hgrn_consts()

    def lat_spec(part):
        return pl.BlockSpec((cfg.seq, hd), lambda b, h, part=part: (b, col0 + part * nh + h))

    def ctx_spec(part):
        return pl.BlockSpec((cfg.ctx, hd), lambda b, h, part=part: (ctx_blk0 + b, col0 + part * nh + h))

    parts = (0, 1, 2, 3, 4)

    def const(arr):
        return pl.BlockSpec(arr.shape, lambda b, h, nd=arr.ndim: (0,) * nd)

    y_lat, y_ctx = pl.pallas_call(
        functools.partial(_hgrn_kernel, layer=layer, seq=cfg.seq, ctx=cfg.ctx),
        grid=(cfg.batch, nh),
        in_specs=[lat_spec(k) for k in parts] + [ctx_spec(k) for k in parts] + [
            pl.BlockSpec((cfg.depth, 2, hd), lambda b, h: (0, 0, h)),
            pl.BlockSpec((1, hd), lambda b, h: (0, 0)),
            const(gf), const(upf), const(lvf), const(gb), const(upb), const(lvb),
        ],
        out_specs=[
            pl.BlockSpec((cfg.seq, hd), lambda b, h: (b, h)),
            pl.BlockSpec((cfg.ctx, hd), lambda b, h: (b, h)),
        ],
        out_shape=[
            jax.ShapeDtypeStruct((out_rows, cfg.hgw), BF16),
            jax.ShapeDtypeStruct((cfg.nc, cfg.hgw), BF16),
        ],
        scratch_shapes=[
            pltpu.VMEM((cfg.seq, hd), F32), pltpu.VMEM((cfg.seq, hd), F32),
            pltpu.VMEM((cfg.ctx, hd), F32), pltpu.VMEM((cfg.ctx, hd), F32),
            pltpu.VMEM((hd, hd), F32), pltpu.VMEM((hd, hd), F32),
        ],
        compiler_params=_params(("arbitrary", "arbitrary")),
        name="hgrn_mixer",
    )(*([p] * 10), logits, norm_w.reshape(1, hd), gf, upf, lvf, gb, upb, lvb)
    return y_lat, y_ctx


def _pool_kernel(prev_ref, cur_ref, next_ref, w_ref, scale_ref, o_ref, buf_ref, *, seq_len, tile, n_tiles):
    i = pl.program_id(1)
    h = POOL_HALO
    cur = cur_ref[...]
    buf_ref[0:h, :] = jnp.where(i > 0, prev_ref[...], 0.0)
    buf_ref[h:h + tile, :] = cur
    buf_ref[h + tile:2 * h + tile, :] = jnp.where(i < n_tiles - 1, next_ref[...], 0.0)
    pos = i * tile + lax.broadcasted_iota(jnp.int32, (tile, POOL_GROUP), 0)
    for gi, win in enumerate(POOL_WINDOWS):
        half = win // 2
        cols = slice(gi * POOL_GROUP, (gi + 1) * POOL_GROUP)
        acc = buf_ref[h - half:h - half + tile, cols]
        for dlt in range(-half + 1, half):
            acc = acc + buf_ref[h + dlt:h + dlt + tile, cols]
        cnt = (jnp.minimum(pos + half, seq_len) - jnp.maximum(pos - half, 0)).astype(F32)
        pooled = acc / cnt - cur[:, cols]
        mixed = _dot(pooled.astype(BF16), w_ref[gi])
        o_ref[:, cols] = (mixed * scale_ref[:, cols]).astype(o_ref.dtype)


def pool_mixer(cfg, p, w_pool, scale, *, n_seq, seq_len, row0, out_rows=None):
    out_rows = n_seq * seq_len if out_rows is None else out_rows
    tile = _pick(seq_len, (512, 256, 128))
    n_tiles = seq_len // tile
    blk0 = row0 // tile
    col = cfg.off_pool // POOL_WIDTH
    per8 = tile // POOL_HALO
    last8 = cfg.nt // POOL_HALO - 1

    def cur_map(s, i):
        return (blk0 + s * n_tiles + i, col)

    def prev_map(s, i):
        return (jnp.maximum((blk0 + s * n_tiles + i) * per8 - 1, 0), col)

    def next_map(s, i):
        return (jnp.minimum((blk0 + s * n_tiles + i + 1) * per8, last8), col)

    return pl.pallas_call(
        functools.partial(_pool_kernel, seq_len=seq_len, tile=tile, n_tiles=n_tiles),
        grid=(n_seq, n_tiles),
        in_specs=[
            pl.BlockSpec((POOL_HALO, POOL_WIDTH), prev_map),
            pl.BlockSpec((tile, POOL_WIDTH), cur_map),
            pl.BlockSpec((POOL_HALO, POOL_WIDTH), next_map),
            pl.BlockSpec((len(POOL_WINDOWS), POOL_GROUP, POOL_GROUP), lambda s, i: (0, 0, 0)),
            pl.BlockSpec((1, POOL_WIDTH), lambda s, i: (0, 0)),
        ],
        out_specs=pl.BlockSpec((tile, POOL_WIDTH), lambda s, i: (s * n_tiles + i, 0)),
        out_shape=jax.ShapeDtypeStruct((out_rows, POOL_WIDTH), BF16),
        scratch_shapes=[pltpu.VMEM((tile + 2 * POOL_HALO, POOL_WIDTH), F32)],
        compiler_params=_params(("arbitrary", "arbitrary")),
        name="pool_mixer",
    )(p, p, p, w_pool, scale.reshape(1, POOL_WIDTH))


def _rope_slab(slab, cs):
    t = slab * cs
    r = t + pltpu.roll(t, ROPE_DIM, axis=1)
    lane = lax.broadcasted_iota(jnp.int32, r.shape, 1)
    return jnp.where(lane < ROPE_DIM, r, 0.0)


def _rms_bf16(x, w):
    return (x * lax.rsqrt(jnp.mean(x * x, axis=-1, keepdims=True) + RMS_EPS) * w).astype(BF16)


def _mla_q_kernel(cq_ref, nw_ref, cs_ref, w_ref, o_ref):
    n = _rms_bf16(cq_ref[...], nw_ref[...])
    cs = cs_ref[...]
    for h in range(w_ref.shape[0]):
        y = _dot(n, w_ref[h])
        rope = _rope_slab(y[:, HEAD_DIM:], cs)
        o_ref[h] = (jnp.concatenate([y[:, :HEAD_DIM], rope], axis=1) * QK_LOG2_SCALE).astype(o_ref.dtype)


def mla_queries(cfg, p, norm_w, w_q, cs):
    tm = _pick(cfg.nc, (512, 256))
    nh = cfg.mla_heads
    col = cfg.off_cq // cfg.q_lora
    return pl.pallas_call(
        _mla_q_kernel,
        grid=(cfg.nt // tm,),
        in_specs=[
            pl.BlockSpec((tm, cfg.q_lora), lambda i: (i, col)),
            pl.BlockSpec((1, cfg.q_lora), lambda i: (0, 0)),
            pl.BlockSpec((tm, HEAD_DIM), lambda i: (i, 0)),
            pl.BlockSpec((nh, cfg.q_lora, QK_DIM), lambda i: (0, 0, 0)),
        ],
        out_specs=pl.BlockSpec((nh, tm, QK_DIM), lambda i: (0, i, 0)),
        out_shape=jax.ShapeDtypeStruct((nh, cfg.nt, QK_DIM), BF16),
        compiler_params=_params(("arbitrary",)),
        name="mla_queries",
    )(p, norm_w.reshape(1, cfg.q_lora), cs, w_q)


def _mla_kv_kernel(ckv_ref, slab_ref, nw_ref, cs_ref, w_ref, k_ref, v_ref):
    n = _rms_bf16(ckv_ref[...], nw_ref[...])
    k_rope = _rope_slab(slab_ref[...], cs_ref[...])
    lane = lax.broadcasted_iota(jnp.int32, k_rope.shape, 1)
    ones_col = jnp.where(lane == 0, 1.0, 0.0)
    for h in range(w_ref.shape[0]):
        y = _dot(n, w_ref[h])
        k_ref[h] = jnp.concatenate([y[:, :HEAD_DIM], k_rope], axis=1).astype(k_ref.dtype)
        v_ref[h] = jnp.concatenate([y[:, HEAD_DIM:], ones_col], axis=1).astype(v_ref.dtype)


def mla_keys_values(cfg, p, norm_w, w_kv, cs):
    tm = _pick(cfg.nc, (512, 256))
    nh = cfg.mla_heads
    col = cfg.off_ckv // cfg.kv_lora
    slab_col = cfg.off_slab // HEAD_DIM
    return pl.pallas_call(
        _mla_kv_kernel,
        grid=(cfg.nt // tm,),
        in_specs=[
            pl.BlockSpec((tm, cfg.kv_lora), lambda i: (i, col)),
            pl.BlockSpec((tm, HEAD_DIM), lambda i: (i, slab_col)),
            pl.BlockSpec((1, cfg.kv_lora), lambda i: (0, 0)),
            pl.BlockSpec((tm, HEAD_DIM), lambda i: (i, 0)),
            pl.BlockSpec((nh, cfg.kv_lora, 2 * HEAD_DIM), lambda i: (0, 0, 0)),
        ],
        out_specs=[
            pl.BlockSpec((nh, tm, QK_DIM), lambda i: (0, i, 0)),
            pl.BlockSpec((nh, tm, 2 * HEAD_DIM), lambda i: (0, i, 0)),
        ],
        out_shape=[
            jax.ShapeDtypeStruct((nh, cfg.nt, QK_DIM), BF16),
            jax.ShapeDtypeStruct((nh, cfg.nt, 2 * HEAD_DIM), BF16),
        ],
        compiler_params=_params(("arbitrary",)),
        name="mla_keys_values",
    )(p, p, norm_w.reshape(1, cfg.kv_lora), cs, w_kv)


ATTN_SUB = 256
ATTN_KT = 512
ATTN_SCORE_ROWS = 1024


def _softmax_pv(s_list, v_list):
    m = functools.reduce(jnp.maximum, [jnp.max(s, axis=-1, keepdims=True) for s in s_list])
    o = None
    for s, v in zip(s_list, v_list):
        part = _dot(jnp.exp2(s - m).astype(BF16), v)
        o = part if o is None else o + part
    return o[:, :HEAD_DIM] / o[:, HEAD_DIM:HEAD_DIM + 1]


def _attn_lat_kernel(q_ref, kl_ref, kc_ref, vl_ref, vc_ref, o_ref, s_ref):
    n_keys = kl_ref.shape[0]
    n_slots = s_ref.shape[0] // ATTN_SUB
    for r in range(0, q_ref.shape[0], ATTN_SUB):
        rows = slice(r, r + ATTN_SUB)
        slot = (r // ATTN_SUB) % n_slots
        srows = slice(slot * ATTN_SUB, (slot + 1) * ATTN_SUB)
        q = q_ref[rows, :]
        sc = _dot_nt(q, kc_ref[...])
        m = jnp.max(sc, axis=-1, keepdims=True)
        for c in range(0, n_keys, ATTN_KT):
            s = _dot_nt(q, kl_ref[c:c + ATTN_KT, :])
            s_ref[srows, c:c + ATTN_KT] = s
            m = jnp.maximum(m, jnp.max(s, axis=-1, keepdims=True))
        o = _dot(jnp.exp2(sc - m).astype(BF16), vc_ref[...])
        for c in range(0, n_keys, ATTN_KT):
            p = jnp.exp2((s_ref[srows, c:c + ATTN_KT] - m).astype(BF16))
            o = o + _dot(p, vl_ref[c:c + ATTN_KT, :])
        o_ref[rows, :] = (o[:, :HEAD_DIM] / o[:, HEAD_DIM:HEAD_DIM + 1]).astype(o_ref.dtype)


def _attn_ctx_kernel(q_ref, kc_ref, vc_ref, o_ref):
    o_ref[...] = _softmax_pv([_dot_nt(q_ref[...], kc_ref[...])], [vc_ref[...]]).astype(o_ref.dtype)


def mla_attention(cfg, q, k, v, *, with_ctx, out_rows):
    nh = cfg.mla_heads
    vw = 2 * HEAD_DIM
    tq = _pick(cfg.seq, (2048, 1024, 512, 256))
    nq = cfg.seq // tq
    s_rows = min(tq, ATTN_SCORE_ROWS)
    cb0 = cfg.nl // cfg.ctx
    y_lat = pl.pallas_call(
        _attn_lat_kernel,
        grid=(cfg.batch, nh, nq),
        in_specs=[
            pl.BlockSpec((None, tq, QK_DIM), lambda b, h, i: (h, b * nq + i, 0)),
            pl.BlockSpec((None, cfg.seq, QK_DIM), lambda b, h, i: (h, b, 0)),
            pl.BlockSpec((None, cfg.ctx, QK_DIM), lambda b, h, i: (h, cb0 + b, 0)),
            pl.BlockSpec((None, cfg.seq, vw), lambda b, h, i: (h, b, 0)),
            pl.BlockSpec((None, cfg.ctx, vw), lambda b, h, i: (h, cb0 + b, 0)),
        ],
        out_specs=pl.BlockSpec((tq, HEAD_DIM), lambda b, h, i: (b * nq + i, h)),
        out_shape=jax.ShapeDtypeStruct((out_rows, nh * HEAD_DIM), BF16),
        scratch_shapes=[pltpu.VMEM((s_rows, cfg.seq), F32)],
        compiler_params=_params(("arbitrary", "arbitrary", "arbitrary")),
        name="mla_attention",
    )(q, k, k, v, v)
    if not with_ctx:
        return y_lat, None
    y_ctx = pl.pallas_call(
        _attn_ctx_kernel,
        grid=(cfg.batch, nh),
        in_specs=[
            pl.BlockSpec((None, cfg.ctx, QK_DIM), lambda b, h: (h, cb0 + b, 0)),
            pl.BlockSpec((None, cfg.ctx, QK_DIM), lambda b, h: (h, cb0 + b, 0)),
            pl.BlockSpec((None, cfg.ctx, vw), lambda b, h: (h, cb0 + b, 0)),
        ],
        out_specs=pl.BlockSpec((cfg.ctx, HEAD_DIM), lambda b, h: (b, h)),
        out_shape=jax.ShapeDtypeStruct((cfg.nc, nh * HEAD_DIM), BF16),
        compiler_params=_params(("arbitrary", "arbitrary")),
        name="mla_attention_ctx",
    )(q, k, v)
    return y_lat, y_ctx


def _merge_kernel(ya_ref, yb_ref, yc_ref, ga_ref, gb_ref, gc_ref, wa_ref, wb_ref, wc_ref, o_ref):
    m = (_sigmoid(ga_ref[...]) * _dot(ya_ref[...], wa_ref[...])
         + _sigmoid(gb_ref[...]) * _dot(yb_ref[...], wb_ref[...])
         + _sigmoid(gc_ref[...]) * _dot(yc_ref[...], wc_ref[...]))
    o_ref[...] = m.astype(o_ref.dtype)


def merge_branches(cfg, ya, yb, yc, p, wa, wb, wc, *, rows):
    tm = _pick(cfg.nc, (512, 256))
    tn = _pick(cfg.d, (512, 256))
    gcols = cfg.d // tn

    def gate_spec(k):
        return pl.BlockSpec((tm, tn), lambda i, j, k=k: (i, k * gcols + j))

    return pl.pallas_call(
        _merge_kernel,
        grid=(rows // tm, cfg.d // tn),
        in_specs=[
            pl.BlockSpec((tm, ya.shape[1]), lambda i, j: (i, 0)),
            pl.BlockSpec((tm, yb.shape[1]), lambda i, j: (i, 0)),
            pl.BlockSpec((tm, yc.shape[1]), lambda i, j: (i, 0)),
            gate_spec(0), gate_spec(1), gate_spec(2),
            pl.BlockSpec((wa.shape[0], tn), lambda i, j: (0, j)),
            pl.BlockSpec((wb.shape[0], tn), lambda i, j: (0, j)),
            pl.BlockSpec((wc.shape[0], tn), lambda i, j: (0, j)),
        ],
        out_specs=pl.BlockSpec((tm, tn), lambda i, j: (i, j)),
        out_shape=jax.ShapeDtypeStruct((rows, cfg.d), BF16),
        compiler_params=_params(("arbitrary", "arbitrary")),
        name="merge_branches",
    )(ya, yb, yc, p, p, p, wa, wb, wc)


def _matmul_resid_kernel(a_ref, w_ref, x_ref, mod_ref, o_ref, *, k):
    o_ref[...] = x_ref[...] + mod_ref[k:k + 1, :] * _dot(a_ref[...], w_ref[...])


def matmul_residual(cfg, a, w, x, mod_l, *, k, rows):
    tm = _pick(cfg.nc, (1024, 512, 256))
    tn = _pick(cfg.d, (512, 256))
    row_of = _mod_row_map(cfg, tm)
    return pl.pallas_call(
        functools.partial(_matmul_resid_kernel, k=k),
        grid=(rows // tm, cfg.d // tn),
        in_specs=[
            pl.BlockSpec((tm, a.shape[1]), lambda i, j: (i, 0)),
            pl.BlockSpec((a.shape[1], tn), lambda i, j: (0, j)),
            pl.BlockSpec((tm, tn), lambda i, j: (i, j)),
            pl.BlockSpec((None, 6, tn), lambda i, j: (row_of(i), 0, j)),
        ],
        out_specs=pl.BlockSpec((tm, tn), lambda i, j: (i, j)),
        out_shape=jax.ShapeDtypeStruct((rows, cfg.d), F32),
        compiler_params=_params(("arbitrary", "arbitrary")),
        name="matmul_residual",
    )(a, w, x, mod_l)


def _top16_pass(s_ref, top_ref, rank_ref, break_ties):
    s = s_ref[...]
    n = s.shape[0]
    row = lax.broadcasted_iota(jnp.int32, s.shape, 0).astype(F32)
    rank = jnp.full(s.shape, UNRANKED, F32)
    work = s
    for r in range(PEER_TOPK):
        m = jnp.max(work, axis=0, keepdims=True)
        sel = work == m
        if break_ties:
            sel = row == jnp.min(jnp.where(sel, row, float(n)), axis=0, keepdims=True)
        rank = jnp.where(sel, float(r), rank)
        work = jnp.where(sel, -jnp.inf, work)
        top_ref[r:r + 1, :] = m
    rank_ref[...] = rank


def _peer_tables_kernel(qh_ref, keys_ref, b_ref, e2_ref, c_ref, e1_ref, cnt_ref, *scratch, tile):
    lanes = HEAD_DIM
    n_arr = 2 * (tile // lanes)
    s_refs, top_refs, rank_refs = (scratch[i * n_arr:(i + 1) * n_arr] for i in range(3))
    k1 = keys_ref[0]
    k2 = keys_ref[1]
    ranked = jnp.zeros((1, lanes), F32)
    for part in range(tile // lanes):
        qh = qh_ref[part * lanes:(part + 1) * lanes, :].astype(BF16)
        s_refs[2 * part][...] = _dot_nt(k1, qh[:, :lanes])
        s_refs[2 * part + 1][...] = _dot_nt(k2, qh[:, lanes:])
        for idx in (2 * part, 2 * part + 1):
            _top16_pass(s_refs[idx], top_refs[idx], rank_refs[idx], break_ties=False)
            n_ranked = jnp.sum(jnp.where(rank_refs[idx][...] < UNRANKED, 1.0, 0.0), axis=0, keepdims=True)
            ranked = jnp.maximum(ranked, n_ranked)

    @pl.when(jnp.max(ranked) > float(PEER_TOPK))
    def _():
        for idx in range(n_arr):
            _top16_pass(s_refs[idx], top_refs[idx], rank_refs[idx], break_ties=True)

    for part in range(tile // lanes):
        cols = slice(part * lanes, (part + 1) * lanes)
        s1 = s_refs[2 * part][...]
        s2 = s_refs[2 * part + 1][...]
        rank1 = rank_refs[2 * part][...]
        rank2 = rank_refs[2 * part + 1][...]
        t1 = top_refs[2 * part][...]
        t2 = top_refs[2 * part + 1][...]
        pieces = [t1[0:1] + t2]
        pos = [lax.broadcasted_iota(jnp.int32, (PEER_TOPK, lanes), 0).astype(F32)]
        for a in range(1, 8):
            pieces.append(t1[a:a + 1] + t2[0:8])
            pos.append(lax.broadcasted_iota(jnp.int32, (8, lanes), 0).astype(F32) + float(a * PEER_TOPK))
        pieces.append(t1[8:16] + t2[0:1])
        pos.append((lax.broadcasted_iota(jnp.int32, (8, lanes), 0).astype(F32) + 8.0) * float(PEER_TOPK))
        cand = jnp.concatenate(pieces, axis=0)
        cpos = jnp.concatenate(pos, axis=0)
        a_row = lax.broadcasted_iota(jnp.int32, (PEER_TOPK, lanes), 0).astype(F32)
        cnt = jnp.zeros((PEER_TOPK, lanes), F32)
        z = jnp.zeros((1, lanes), F32)
        best0 = None
        for r in range(PEER_TOPK):
            m = jnp.max(cand, axis=0, keepdims=True)
            first = jnp.min(jnp.where(cand == m, cpos, 1e9), axis=0, keepdims=True)
            cand = jnp.where(cpos == first, -jnp.inf, cand)
            cnt = cnt + jnp.where(a_row == jnp.floor(first * (1.0 / PEER_TOPK)), 1.0, 0.0)
            if r == 0:
                best0 = m
            z = z + jnp.exp(m - best0)
        cnt_ref[...] = cnt
        c_tab = jnp.zeros((PEER_NKEYS, lanes), F32)
        for a in range(PEER_TOPK):
            c_tab = jnp.where(rank1 == float(a), cnt_ref[a:a + 1, :], c_tab)
        b_ref[:, cols] = rank2.astype(b_ref.dtype)
        c_ref[:, cols] = c_tab
        e1_ref[:, cols] = jnp.exp(s1 - t1[0:1]) / z
        e2_ref[:, cols] = jnp.exp(s2 - t2[0:1]).astype(e2_ref.dtype)


def peer_tables(cfg, qh, keys, *, rows):
    tile = 256
    n_arr = 2 * (tile // HEAD_DIM)
    nh = cfg.peer_heads
    shapes = [jax.ShapeDtypeStruct((nh, PEER_NKEYS, rows), dt) for dt in (BF16, BF16, F32, F32)]
    out_spec = pl.BlockSpec((None, PEER_NKEYS, tile), lambda i, h: (h, 0, i))
    return pl.pallas_call(
        functools.partial(_peer_tables_kernel, tile=tile),
        grid=(rows // tile, nh),
        in_specs=[
            pl.BlockSpec((tile, 2 * HEAD_DIM), lambda i, h: (i, h)),
            pl.BlockSpec((2, None, PEER_NKEYS, HEAD_DIM), lambda i, h: (0, h, 0, 0)),
        ],
        out_specs=[out_spec] * 4,
        out_shape=shapes,
        scratch_shapes=([pltpu.VMEM((PEER_TOPK, HEAD_DIM), F32)]
                        + [pltpu.VMEM((PEER_NKEYS, HEAD_DIM), F32) for _ in range(n_arr)]
                        + [pltpu.VMEM((PEER_TOPK, HEAD_DIM), F32) for _ in range(n_arr)]
                        + [pltpu.VMEM((PEER_NKEYS, HEAD_DIM), F32) for _ in range(n_arr)]),
        compiler_params=_params(("arbitrary", "arbitrary")),
        name="peer_tables",
    )(qh, keys)


PEER_ROWS = 8
PEER_EB = PEER_ROWS * PEER_NKEYS
PEER_DOT_ROWS = 1


def _peer_dense_kernel(ht_ref, u_ref, vt_ref, b_ref, e2_ref, c_ref, e1_ref, x_ref, mod_ref, o_ref,
                       acc_ref, w_ref, *, heads):
    e = pl.program_id(1)

    @pl.when(e == 0)
    def _():
        acc_ref[...] = jnp.zeros_like(acc_ref)

    zero = jnp.zeros((), BF16)
    ht = ht_ref[...]
    for r0 in range(0, PEER_ROWS, PEER_DOT_ROWS):
        blk = slice(r0 * PEER_NKEYS, (r0 + PEER_DOT_ROWS) * PEER_NKEYS)
        act = _dot(u_ref[blk, :], ht)
        gelu = (0.5 * act * (1.0 + lax.erf(act * float(math.sqrt(0.5))))).astype(BF16)
        for q in range(PEER_DOT_ROWS):
            r = r0 + q
            g = jnp.zeros((PEER_NKEYS, act.shape[1]), BF16)
            for h in range(heads):
                crow = jnp.broadcast_to(c_ref[h, r:r + 1, :], g.shape).astype(BF16)
                erow = jnp.broadcast_to(e1_ref[h, r:r + 1, :], g.shape).astype(BF16)
                g = g + jnp.where(b_ref[h] < crow, e2_ref[h] * erow, zero)
            w_ref[r * PEER_NKEYS:(r + 1) * PEER_NKEYS, :] = g * gelu[q * PEER_NKEYS:(q + 1) * PEER_NKEYS]
    acc_ref[...] += _dot(vt_ref[...], w_ref[...])

    @pl.when(e == pl.num_programs(1) - 1)
    def _():
        o_ref[...] = x_ref[...] + mod_ref[5:6, :] * acc_ref[...].T


def peer_dense(cfg, ht, u, vt, tabs, x, mod_l, *, rows):
    tile = _pick(cfg.nc, (512, 256))
    eb = PEER_EB
    n_exp = u.shape[0]
    nh = cfg.peer_heads
    row_of = _mod_row_map(cfg, tile)
    tab_spec = pl.BlockSpec((nh, PEER_NKEYS, tile), lambda i, e: (0, 0, i))
    row_spec = pl.BlockSpec((nh, PEER_ROWS, tile), lambda i, e: (0, e, i))
    return pl.pallas_call(
        functools.partial(_peer_dense_kernel, heads=nh),
        grid=(rows // tile, n_exp // eb),
        in_specs=[
            pl.BlockSpec((cfg.d, tile), lambda i, e: (0, i)),
            pl.BlockSpec((eb, cfg.d), lambda i, e: (e, 0)),
            pl.BlockSpec((cfg.d, eb), lambda i, e: (0, e)),
            tab_spec, tab_spec, row_spec, row_spec,
            pl.BlockSpec((tile, cfg.d), lambda i, e: (i, 0)),
            pl.BlockSpec((None, 6, cfg.d), lambda i, e: (row_of(i), 0, 0)),
        ],
        out_specs=pl.BlockSpec((tile, cfg.d), lambda i, e: (i, 0)),
        out_shape=jax.ShapeDtypeStruct((rows, cfg.d), F32),
        scratch_shapes=[pltpu.VMEM((cfg.d, tile), F32), pltpu.VMEM((eb, tile), BF16)],
        compiler_params=_params(("arbitrary", "arbitrary")),
        name="peer_dense",
    )(ht, u, vt, *tabs, x, mod_l)


def _final_norm_kernel(x_ref, w_ref, o_ref):
    x = x_ref[...]
    o_ref[...] = x * lax.rsqrt(jnp.mean(x * x, axis=-1, keepdims=True) + RMS_EPS) * w_ref[...]


def final_norm(cfg, x, w):
    tm = _pick(cfg.nl, (1024, 512, 256))
    return pl.pallas_call(
        _final_norm_kernel,
        grid=(cfg.nl // tm,),
        in_specs=[pl.BlockSpec((tm, cfg.d), lambda i: (i, 0)), pl.BlockSpec((1, cfg.d), lambda i: (0, 0))],
        out_specs=pl.BlockSpec((tm, cfg.d), lambda i: (i, 0)),
        out_shape=jax.ShapeDtypeStruct((cfg.nl, cfg.d), F32),
        compiler_params=_params(("arbitrary",)),
        name="final_norm",
    )(x, w.reshape(1, cfg.d))


def _rot_cols(w):
    q = ROPE_DIM // 4
    return jnp.concatenate([-w[..., q:2 * q], w[..., 0:q], -w[..., 3 * q:4 * q], w[..., 2 * q:3 * q]], axis=-1)


def _in_proj_weight(cfg, w_in, n_cols):
    d = cfg.d
    hg_end = 5 * cfg.hgw
    pool_end = hg_end + POOL_WIDTH
    cq_end = pool_end + cfg.q_lora
    ckv_end = cq_end + cfg.kv_lora
    rope_end = ckv_end + ROPE_DIM
    k_rope = w_in[:, ckv_end:rope_end]
    parts = [w_in[:, rope_end:rope_end + 3 * d], w_in[:, :ckv_end], k_rope, _rot_cols(k_rope)]
    parts.append(jnp.zeros((d, n_cols - cfg.in_cols), w_in.dtype))
    return jnp.concatenate([part.astype(BF16) for part in parts], axis=1)


def _mla_q_weight(cfg, w_uq):
    w = w_uq.reshape(cfg.q_lora, cfg.mla_heads, HEAD_DIM + ROPE_DIM)
    rope = w[..., HEAD_DIM:]
    w = jnp.concatenate([w[..., :HEAD_DIM], rope, _rot_cols(rope)], axis=-1)
    return jnp.transpose(w, (1, 0, 2)).astype(BF16)


def _mla_kv_weight(cfg, w_ukv):
    w = w_ukv.reshape(cfg.kv_lora, cfg.mla_heads, 2 * HEAD_DIM)
    return jnp.transpose(w, (1, 0, 2)).astype(BF16)


def _rope_table(cfg):
    rows = cfg.seq // cfg.grid_w
    r, col = jnp.meshgrid(jnp.arange(rows), jnp.arange(cfg.grid_w), indexing="ij")
    n_freq = ROPE_DIM // 4
    freqs = ROPE_THETA ** (-jnp.arange(n_freq, dtype=F32) / n_freq)
    ang_r = r.reshape(-1)[:, None] * freqs
    ang_c = col.reshape(-1)[:, None] * freqs
    cos = jnp.concatenate([jnp.cos(ang_r)] * 2 + [jnp.cos(ang_c)] * 2, axis=1)
    sin = jnp.concatenate([jnp.sin(ang_r)] * 2 + [jnp.sin(ang_c)] * 2, axis=1)
    lat = jnp.tile(jnp.concatenate([cos, sin], axis=1).astype(F32), (cfg.batch, 1))
    ctx = jnp.concatenate([jnp.ones((cfg.nc, ROPE_DIM), F32), jnp.zeros((cfg.nc, ROPE_DIM), F32)], axis=1)
    return jnp.concatenate([lat, ctx], axis=0)


def _forward(cfg, x, c, ctx, c_ctx, w_mod, b_mod, norm_mix, norm_ffn, w_in, hg_lb_logits, hg_norm,
             pool_w, pool_scale, mla_q_norm, mla_w_uq, mla_kv_norm, mla_w_ukv,
             w_branch_a, w_branch_b, w_branch_c, w_out, peer_wq, peer_keys, peer_u, peer_v, final_w):
    d = cfg.d
    assert cfg.seq % HG_CHUNK == 0 and cfg.ctx % HG_CHUNK == 0 and cfg.nl % cfg.ctx == 0
    assert cfg.off_pool % POOL_WIDTH == 0 and cfg.off_cq % cfg.q_lora == 0
    assert cfg.off_ckv % cfg.kv_lora == 0 and cfg.batch < MOD_ROWS
    tn_in = 768 if d % 256 == 0 and cfg.in_cols > 8192 else 256
    n_cols = -(-cfg.in_cols // tn_in) * tn_in
    tm = _pick(cfg.nc, (1024, 512, 256))

    xs = jnp.concatenate([x.reshape(cfg.nl, d), ctx.reshape(cfg.nc, d)], axis=0)
    c_all = jnp.concatenate([c, c_ctx[None], jnp.zeros((MOD_ROWS - cfg.batch - 1, d), F32)], axis=0)
    mod = adaln_tables(cfg, c_all, w_mod, b_mod)
    cs = _rope_table(cfg)

    for l in range(cfg.depth):
        last = l == cfg.depth - 1
        rows = cfg.nl if last else cfg.nt
        mod_l = mod[l]
        p = norm_matmul(cfg, xs, norm_mix[l], mod_l, _in_proj_weight(cfg, w_in[l], n_cols),
                        k0=0, rows=cfg.nt, tm=tm, tn=tn_in, emit_h=False)
        ya, hg_ctx = hgrn_mixer(cfg, p, hg_lb_logits, hg_norm[l], l, rows)
        pw = pool_w[l].astype(BF16)
        yb = pool_mixer(cfg, p, pw, pool_scale[l], n_seq=cfg.batch, seq_len=cfg.seq, row0=0, out_rows=rows)
        q = mla_queries(cfg, p, mla_q_norm[l], _mla_q_weight(cfg, mla_w_uq[l]), cs)
        k, v = mla_keys_values(cfg, p, mla_kv_norm[l], _mla_kv_weight(cfg, mla_w_ukv[l]), cs)
        yc, att_ctx = mla_attention(cfg, q, k, v, with_ctx=not last, out_rows=rows)
        if not last:
            pool_ctx = pool_mixer(cfg, p, pw, pool_scale[l], n_seq=cfg.batch, seq_len=cfg.ctx, row0=cfg.nl)
            ya = lax.dynamic_update_slice(ya, hg_ctx, (cfg.nl, 0))
            yb = lax.dynamic_update_slice(yb, pool_ctx, (cfg.nl, 0))
            yc = lax.dynamic_update_slice(yc, att_ctx, (cfg.nl, 0))
        m = merge_branches(cfg, ya, yb, yc, p, w_branch_a[l].astype(BF16), w_branch_b[l].astype(BF16),
                           w_branch_c[l].astype(BF16), rows=rows)
        xs = matmul_residual(cfg, m, w_out[l].astype(BF16), xs, mod_l, k=2, rows=rows)
        qh, h2 = norm_matmul(cfg, xs, norm_ffn[l], mod_l, peer_wq[l].astype(BF16),
                             k0=3, rows=rows, tm=tm, tn=_pick(peer_wq.shape[2], (512, 256)), emit_h=True)
        tabs = peer_tables(cfg, qh, peer_keys[l].astype(BF16), rows=rows)
        xs = peer_dense(cfg, h2.T, peer_u[l].astype(BF16), peer_v[l].astype(BF16).T, tabs, xs, mod_l, rows=rows)

    return final_norm(cfg, xs, final_w).reshape(cfg.batch, cfg.seq, d)


def kernel(x, c, ctx, c_ctx, w_mod, b_mod, norm_mix, norm_ffn, w_in, hg_lb_logits, hg_norm, pool_w, pool_scale,
           mla_q_norm, mla_w_uq, mla_kv_norm, mla_w_ukv, w_branch_a, w_branch_b, w_branch_c, w_out,
           peer_wq, peer_keys, peer_u, peer_v, final_norm):
    batch, seq, d = x.shape
    cfg = Cfg(d=d, batch=batch, seq=seq, ctx=ctx.shape[1], grid_w=64, depth=w_mod.shape[0],
              hg_heads=hg_lb_logits.shape[2] // HEAD_DIM,
              mla_heads=mla_w_ukv.shape[2] // (2 * HEAD_DIM), q_lora=mla_q_norm.shape[1],
              kv_lora=mla_kv_norm.shape[1], peer_heads=peer_keys.shape[2])
    return _forward(cfg, x, c, ctx, c_ctx, w_mod, b_mod, norm_mix, norm_ffn, w_in, hg_lb_logits, hg_norm,
                    pool_w, pool_scale, mla_q_norm, mla_w_uq, mla_kv_norm, mla_w_ukv,
                    w_branch_a, w_branch_b, w_branch_c, w_out, peer_wq, peer_keys, peer_u, peer_v, final_norm)
```

```python
import functools
import math
from typing import NamedTuple

import numpy as np
import jax
import jax.numpy as jnp
from jax import lax
from jax.experimental import pallas as pl
from jax.experimental.pallas import tpu as pltpu

F32 = jnp.float32
BF16 = jnp.bfloat16

RMS_EPS = 1e-6
ROPE_THETA = 10000.0
HEAD_DIM = 128
ROPE_DIM = 64
QK_DIM = 256
LOG2_E = float(math.log2(math.e))
QK_LOG2_SCALE = float((HEAD_DIM + ROPE_DIM) ** -0.5) * LOG2_E
POOL_WINDOWS = (2, 4, 8, 16)
POOL_GROUP = 256
POOL_WIDTH = POOL_GROUP * len(POOL_WINDOWS)
POOL_HALO = 8
PEER_NKEYS = 128
PEER_TOPK = 16
HG_CHUNK = 128
HG_LEVELS = 7
UNRANKED = 99.0
MOD_ROWS = 8
VMEM_LIMIT = 56 * 1024 * 1024


class Cfg(NamedTuple):
    d: int
    batch: int
    seq: int
    ctx: int
    grid_w: int
    depth: int
    hg_heads: int
    mla_heads: int
    q_lora: int
    kv_lora: int
    peer_heads: int

    @property
    def nl(self):
        return self.batch * self.seq

    @property
    def nc(self):
        return self.batch * self.ctx

    @property
    def nt(self):
        return self.nl + self.nc

    @property
    def hgw(self):
        return self.hg_heads * HEAD_DIM

    @property
    def off_gate(self):
        return 0

    @property
    def off_hg(self):
        return 3 * self.d

    @property
    def off_pool(self):
        return self.off_hg + 5 * self.hgw

    @property
    def off_cq(self):
        return self.off_pool + POOL_WIDTH

    @property
    def off_ckv(self):
        return self.off_cq + self.q_lora

    @property
    def off_slab(self):
        return self.off_ckv + self.kv_lora

    @property
    def in_cols(self):
        return self.off_slab + HEAD_DIM


def _params(sem):
    return pltpu.CompilerParams(dimension_semantics=sem, vmem_limit_bytes=VMEM_LIMIT)


def _dot(a, b):
    return jnp.dot(a, b, preferred_element_type=F32)


def _dot_nt(a, b):
    return lax.dot_general(a, b, (((1,), (1,)), ((), ())), preferred_element_type=F32)


def _sigmoid(x):
    return 1.0 / (1.0 + jnp.exp(-x))


def _pick(n, prefs):
    for p in prefs:
        if n % p == 0:
            return p
    raise ValueError(f"no tile for {n} in {prefs}")


def _mod_row_map(cfg, tm):
    n_lat = cfg.nl // tm
    per_batch = cfg.seq // tm
    return lambda i: jnp.where(i < n_lat, i // per_batch, cfg.batch)


def _mod_kernel(c_ref, w_ref, b_ref, o_ref):
    c = c_ref[...]
    s = (c * _sigmoid(c)).astype(BF16)
    o_ref[...] = _dot(s, w_ref[...].astype(BF16)) + b_ref[...]


def adaln_tables(cfg, c_all, w_mod, b_mod):
    d6 = 6 * cfg.d
    tn = _pick(d6, (1024, 768, 512, 256))
    out = pl.pallas_call(
        _mod_kernel,
        grid=(cfg.depth, d6 // tn),
        in_specs=[
            pl.BlockSpec((MOD_ROWS, cfg.d), lambda l, j: (0, 0)),
            pl.BlockSpec((None, cfg.d, tn), lambda l, j: (l, 0, j)),
            pl.BlockSpec((None, 1, tn), lambda l, j: (l, 0, j)),
        ],
        out_specs=pl.BlockSpec((None, MOD_ROWS, tn), lambda l, j: (l, 0, j)),
        out_shape=jax.ShapeDtypeStruct((cfg.depth, MOD_ROWS, d6), F32),
        compiler_params=_params(("arbitrary", "arbitrary")),
        name="adaln_tables",
    )(c_all, w_mod, b_mod.reshape(cfg.depth, 1, d6))
    return out.reshape(cfg.depth, MOD_ROWS, 6, cfg.d)


def _norm_matmul_kernel(x_ref, gain_ref, mod_ref, w_ref, *rest, k0, emit_h):
    if emit_h:
        o_ref, hout_ref, h_ref = rest
    else:
        o_ref, h_ref = rest

    @pl.when(pl.program_id(1) == 0)
    def _():
        x = x_ref[...]
        y = x * lax.rsqrt(jnp.mean(x * x, axis=-1, keepdims=True) + RMS_EPS) * gain_ref[...]
        h = (y * (1.0 + mod_ref[k0 + 1:k0 + 2, :]) + mod_ref[k0:k0 + 1, :]).astype(BF16)
        h_ref[...] = h
        if emit_h:
            hout_ref[...] = h

    o_ref[...] = _dot(h_ref[...], w_ref[...])


def norm_matmul(cfg, x, gain, mod_l, w, *, layer, k0, rows, tm, tn, emit_h):
    n = w.shape[2]
    row_of = _mod_row_map(cfg, tm)
    out_shape = [jax.ShapeDtypeStruct((rows, n), F32)]
    out_specs = [pl.BlockSpec((tm, tn), lambda i, j: (i, j))]
    if emit_h:
        out_shape.append(jax.ShapeDtypeStruct((rows, cfg.d), BF16))
        out_specs.append(pl.BlockSpec((tm, cfg.d), lambda i, j: (i, 0)))
    res = pl.pallas_call(
        functools.partial(_norm_matmul_kernel, k0=k0, emit_h=emit_h),
        grid=(rows // tm, n // tn),
        in_specs=[
            pl.BlockSpec((tm, cfg.d), lambda i, j: (i, 0)),
            pl.BlockSpec((1, cfg.d), lambda i, j: (0, 0)),
            pl.BlockSpec((None, 6, cfg.d), lambda i, j: (row_of(i), 0, 0)),
            pl.BlockSpec((None, cfg.d, tn), lambda i, j: (layer, 0, j)),
        ],
        out_specs=out_specs,
        out_shape=out_shape,
        scratch_shapes=[pltpu.VMEM((tm, cfg.d), BF16)],
        compiler_params=_params(("arbitrary", "arbitrary")),
        name="norm_matmul_h" if emit_h else "norm_matmul",
    )(x, gain.reshape(1, cfg.d), mod_l, w)
    return res if emit_h else res[0]


def _hgrn_consts():
    c = HG_CHUNK
    out = []
    for rev in (False, True):
        p = np.arange(c) if not rev else c - 1 - np.arange(c)
        pt, pu = p[:, None], p[None, :]
        g = np.zeros((HG_LEVELS + 1, c, c), np.float32)
        up = np.zeros((HG_LEVELS, c, c), np.float32)
        g[0] = pu <= pt
        for l in range(HG_LEVELS):
            m = 1 << l
            blk = p >> (l + 1)
            upper = ((p >> l) & 1) == 1
            mid = (blk * 2 * m + m)[:, None]
            same = blk[:, None] == blk[None, :]
            g_up = same & (pu >= mid) & (pu <= pt)
            g_lo = same & (pu > pt) & (pu < mid)
            g[1 + l] = np.where(upper[:, None], g_up, g_lo)
            up[l] = np.broadcast_to(upper[:, None], (c, c))
        x = pt ^ pu
        lv = np.where(pu < pt, np.floor(np.log2(np.maximum(x, 1))), -1.0).astype(np.float32)
        out.append((jnp.asarray(g.reshape(-1, c), BF16), jnp.asarray(up, F32), jnp.asarray(lv, F32)))
    return out


def _hgrn_chunk(q_raw, v, f_raw, log_lb, log_1mlb, one_m_lb, g_ref, up_ref, lv_ref, st_ref, end_row):
    c = HG_CHUNK
    q = q_raw * _sigmoid(q_raw)
    e = jnp.exp(-jnp.abs(f_raw))
    one_pe = 1.0 + e
    log_sig = jnp.minimum(f_raw, 0.0) - jnp.log(one_pe)
    t = log_1mlb + log_sig
    log_f = jnp.maximum(log_lb, t) + jnp.log(1.0 + jnp.exp(-jnp.abs(log_lb - t)))
    k = one_m_lb * jnp.where(f_raw >= 0.0, e, 1.0) / one_pe
    hi = log_f.astype(BF16)
    lo = (log_f - hi.astype(F32)).astype(BF16)
    a2 = _dot(g_ref[...], jnp.concatenate([hi, lo], axis=1))
    a = a2[:, :HEAD_DIM] + a2[:, HEAD_DIM:]
    b = a[0:c]
    lv = lv_ref[...]
    scores = jnp.zeros((c, c), F32)
    for l in range(HG_LEVELS):
        e_l = jnp.exp(a[(1 + l) * c:(2 + l) * c])
        x = (jnp.where(up_ref[l] > 0.5, q, k) * e_l).astype(BF16)
        scores = jnp.where(lv == float(l), _dot_nt(x, x), scores)
    b_end = b[end_row:end_row + 1, :]
    qb = (q * jnp.exp(b)).astype(BF16)
    kd = (k * jnp.exp(b_end - b)).astype(BF16)
    st = st_ref[...]
    vb = v.astype(BF16)
    o = (_dot(scores.astype(BF16), vb) + _dot_nt(qb, st.astype(BF16))
         + jnp.sum(q * k, axis=-1, keepdims=True) * v)
    st_ref[...] = st * jnp.exp(b_end) + _dot(v.T.astype(BF16), kd)
    return o


def _hgrn_kernel(ql, ffl, fbl, il, gl, qc, ffc, fbc, ic, gc, logit_ref, nw_ref,
                 gf_ref, upf_ref, lvf_ref, gb_ref, upb_ref, lvb_ref,
                 yl_ref, yc_ref, ofl, obl, ofc, obc, stf, stb, *, layer, seq, ctx):
    c = HG_CHUNK
    depth = logit_ref.shape[0]
    lg = [logit_ref[dd] for dd in range(depth)]
    mx = functools.reduce(jnp.maximum, lg)
    ex = [jnp.exp(v - mx) for v in lg]
    tot = functools.reduce(jnp.add, ex)
    cum = [ex[0] / tot]
    for dd in range(1, layer + 1):
        cum.append(cum[-1] + ex[dd] / tot)
    lb = cum[layer] - cum[0]
    log_lb = jnp.log(lb)
    log_1mlb = jnp.log1p(-lb)
    one_m_lb = 1.0 - lb

    stf[...] = jnp.zeros_like(stf)
    stb[...] = jnp.zeros_like(stb)

    def segment(q_ref, ff_ref, fb_ref, i_ref, of_ref, ob_ref, n):
        def body(j, carry):
            rf = pl.multiple_of(j * c, c)
            rb = pl.multiple_of((n - 1 - j) * c, c)
            of_ref[pl.ds(rf, c), :] = _hgrn_chunk(
                q_ref[pl.ds(rf, c), :], i_ref[pl.ds(rf, c), :], ff_ref[pl.ds(rf, c), :],
                log_lb[0:1], log_1mlb[0:1], one_m_lb[0:1], gf_ref, upf_ref, lvf_ref, stf, c - 1)
            ob_ref[pl.ds(rb, c), :] = _hgrn_chunk(
                q_ref[pl.ds(rb, c), :], i_ref[pl.ds(rb, c), :], fb_ref[pl.ds(rb, c), :],
                log_lb[1:2], log_1mlb[1:2], one_m_lb[1:2], gb_ref, upb_ref, lvb_ref, stb, 0)
            return carry
        lax.fori_loop(0, n, body, 0, unroll=4 if n % 4 == 0 else 2)

    segment(qc, ffc, fbc, ic, ofc, obc, ctx // c)
    segment(ql, ffl, fbl, il, ofl, obl, seq // c)

    nw = nw_ref[...]

    def readout(of_ref, ob_ref, g_ref, y_ref, n):
        def body(j, carry):
            r = pl.multiple_of(j * c, c)
            o = of_ref[pl.ds(r, c), :] + ob_ref[pl.ds(r, c), :]
            y = o * lax.rsqrt(jnp.mean(o * o, axis=-1, keepdims=True) + RMS_EPS) * nw
            g = g_ref[pl.ds(r, c), :]
            y_ref[pl.ds(r, c), :] = (y * (g * _sigmoid(g))).astype(y_ref.dtype)
            return carry
        lax.fori_loop(0, n, body, 0)

    readout(ofc, obc, gc, yc_ref, ctx // c)
    readout(ofl, obl, gl, yl_ref, seq // c)


def hgrn_mixer(cfg, p, logits, norm_w, layer, out_rows):
    hd = HEAD_DIM
    nh = cfg.hg_heads
    col0 = cfg.off_hg // hd
    ctx_blk0 = cfg.nl // cfg.ctx
    (gf, upf, lvf), (gb, upb, lvb) = _hgrn_consts()

    def lat_spec(part):
        return pl.BlockSpec((cfg.seq, hd), lambda b, h, part=part: (b, col0 + part * nh + h))

    def ctx_spec(part):
        return pl.BlockSpec((cfg.ctx, hd), lambda b, h, part=part: (ctx_blk0 + b, col0 + part * nh + h))

    parts = (0, 1, 2, 3, 4)

    def const(arr):
        return pl.BlockSpec(arr.shape, lambda b, h, nd=arr.ndim: (0,) * nd)

    y_lat, y_ctx = pl.pallas_call(
        functools.partial(_hgrn_kernel, layer=layer, seq=cfg.seq, ctx=cfg.ctx),
        grid=(cfg.batch, nh),
        in_specs=[lat_spec(k) for k in parts] + [ctx_spec(k) for k in parts] + [
            pl.BlockSpec((cfg.depth, 2, hd), lambda b, h: (0, 0, h)),
            pl.BlockSpec((1, hd), lambda b, h: (0, 0)),
            const(gf), const(upf), const(lvf), const(gb), const(upb), const(lvb),
        ],
        out_specs=[
            pl.BlockSpec((cfg.seq, hd), lambda b, h: (b, h)),
            pl.BlockSpec((cfg.ctx, hd), lambda b, h: (b, h)),
        ],
        out_shape=[
            jax.ShapeDtypeStruct((out_rows, cfg.hgw), BF16),
            jax.ShapeDtypeStruct((cfg.nc, cfg.hgw), BF16),
        ],
        scratch_shapes=[
            pltpu.VMEM((cfg.seq, hd), F32), pltpu.VMEM((cfg.seq, hd), F32),
            pltpu.VMEM((cfg.ctx, hd), F32), pltpu.VMEM((cfg.ctx, hd), F32),
            pltpu.VMEM((hd, hd), F32), pltpu.VMEM((hd, hd), F32),
        ],
        compiler_params=_params(("arbitrary", "arbitrary")),
        name="hgrn_mixer",
    )(*([p] * 10), logits, norm_w.reshape(1, hd), gf, upf, lvf, gb, upb, lvb)
    return y_lat, y_ctx


def _pool_kernel(prev_ref, cur_ref, next_ref, w_ref, scale_ref, o_ref, buf_ref, *, seq_len, tile, n_tiles):
    i = pl.program_id(1)
    h = POOL_HALO
    cur = cur_ref[...]
    buf_ref[0:h, :] = jnp.where(i > 0, prev_ref[...], 0.0)
    buf_ref[h:h + tile, :] = cur
    buf_ref[h + tile:2 * h + tile, :] = jnp.where(i < n_tiles - 1, next_ref[...], 0.0)
    pos = i * tile + lax.broadcasted_iota(jnp.int32, (tile, POOL_GROUP), 0)
    for gi, win in enumerate(POOL_WINDOWS):
        half = win // 2
        cols = slice(gi * POOL_GROUP, (gi + 1) * POOL_GROUP)
        acc = buf_ref[h - half:h - half + tile, cols]
        for dlt in range(-half + 1, half):
            acc = acc + buf_ref[h + dlt:h + dlt + tile, cols]
        cnt = (jnp.minimum(pos + half, seq_len) - jnp.maximum(pos - half, 0)).astype(F32)
        pooled = acc / cnt - cur[:, cols]
        mixed = _dot(pooled.astype(BF16), w_ref[gi])
        o_ref[:, cols] = (mixed * scale_ref[:, cols]).astype(o_ref.dtype)


def pool_mixer(cfg, p, w_pool, scale, *, n_seq, seq_len, row0, out_rows=None):
    out_rows = n_seq * seq_len if out_rows is None else out_rows
    tile = _pick(seq_len, (512, 256, 128))
    n_tiles = seq_len // tile
    blk0 = row0 // tile
    col = cfg.off_pool // POOL_WIDTH
    per8 = tile // POOL_HALO
    last8 = cfg.nt // POOL_HALO - 1

    def cur_map(s, i):
        return (blk0 + s * n_tiles + i, col)

    def prev_map(s, i):
        return (jnp.maximum((blk0 + s * n_tiles + i) * per8 - 1, 0), col)

    def next_map(s, i):
        return (jnp.minimum((blk0 + s * n_tiles + i + 1) * per8, last8), col)

    return pl.pallas_call(
        functools.partial(_pool_kernel, seq_len=seq_len, tile=tile, n_tiles=n_tiles),
        grid=(n_seq, n_tiles),
        in_specs=[
            pl.BlockSpec((POOL_HALO, POOL_WIDTH), prev_map),
            pl.BlockSpec((tile, POOL_WIDTH), cur_map),
            pl.BlockSpec((POOL_HALO, POOL_WIDTH), next_map),
            pl.BlockSpec((len(POOL_WINDOWS), POOL_GROUP, POOL_GROUP), lambda s, i: (0, 0, 0)),
            pl.BlockSpec((1, POOL_WIDTH), lambda s, i: (0, 0)),
        ],
        out_specs=pl.BlockSpec((tile, POOL_WIDTH), lambda s, i: (s * n_tiles + i, 0)),
        out_shape=jax.ShapeDtypeStruct((out_rows, POOL_WIDTH), BF16),
        scratch_shapes=[pltpu.VMEM((tile + 2 * POOL_HALO, POOL_WIDTH), F32)],
        compiler_params=_params(("arbitrary", "arbitrary")),
        name="pool_mixer",
    )(p, p, p, w_pool, scale.reshape(1, POOL_WIDTH))


def _rope_slab(slab, cs):
    t = slab * cs
    r = t + pltpu.roll(t, ROPE_DIM, axis=1)
    lane = lax.broadcasted_iota(jnp.int32, r.shape, 1)
    return jnp.where(lane < ROPE_DIM, r, 0.0)


def _rms_bf16(x, w):
    return (x * lax.rsqrt(jnp.mean(x * x, axis=-1, keepdims=True) + RMS_EPS) * w).astype(BF16)


def _mla_q_kernel(cq_ref, nw_ref, cs_ref, w_ref, o_ref):
    n = _rms_bf16(cq_ref[...], nw_ref[...])
    cs = cs_ref[...]
    for h in range(w_ref.shape[0]):
        y = _dot(n, w_ref[h])
        rope = _rope_slab(y[:, HEAD_DIM:], cs)
        o_ref[h] = (jnp.concatenate([y[:, :HEAD_DIM], rope], axis=1) * QK_LOG2_SCALE).astype(o_ref.dtype)


def mla_queries(cfg, p, norm_w, w_q, cs):
    tm = _pick(cfg.nc, (512, 256))
    nh = cfg.mla_heads
    col = cfg.off_cq // cfg.q_lora
    return pl.pallas_call(
        _mla_q_kernel,
        grid=(cfg.nt // tm,),
        in_specs=[
            pl.BlockSpec((tm, cfg.q_lora), lambda i: (i, col)),
            pl.BlockSpec((1, cfg.q_lora), lambda i: (0, 0)),
            pl.BlockSpec((tm, HEAD_DIM), lambda i: (i, 0)),
            pl.BlockSpec((nh, cfg.q_lora, QK_DIM), lambda i: (0, 0, 0)),
        ],
        out_specs=pl.BlockSpec((nh, tm, QK_DIM), lambda i: (0, i, 0)),
        out_shape=jax.ShapeDtypeStruct((nh, cfg.nt, QK_DIM), BF16),
        compiler_params=_params(("arbitrary",)),
        name="mla_queries",
    )(p, norm_w.reshape(1, cfg.q_lora), cs, w_q)


def _mla_kv_kernel(ckv_ref, slab_ref, nw_ref, cs_ref, w_ref, k_ref, v_ref):
    n = _rms_bf16(ckv_ref[...], nw_ref[...])
    k_rope = _rope_slab(slab_ref[...], cs_ref[...])
    lane = lax.broadcasted_iota(jnp.int32, k_rope.shape, 1)
    ones_col = jnp.where(lane == 0, 1.0, 0.0)
    for h in range(w_ref.shape[0]):
        y = _dot(n, w_ref[h])
        k_ref[h] = jnp.concatenate([y[:, :HEAD_DIM], k_rope], axis=1).astype(k_ref.dtype)
        v_ref[h] = jnp.concatenate([y[:, HEAD_DIM:], ones_col], axis=1).astype(v_ref.dtype)


def mla_keys_values(cfg, p, norm_w, w_kv, cs):
    tm = _pick(cfg.nc, (512, 256))
    nh = cfg.mla_heads
    col = cfg.off_ckv // cfg.kv_lora
    slab_col = cfg.off_slab // HEAD_DIM
    return pl.pallas_call(
        _mla_kv_kernel,
        grid=(cfg.nt // tm,),
        in_specs=[
            pl.BlockSpec((tm, cfg.kv_lora), lambda i: (i, col)),
            pl.BlockSpec((tm, HEAD_DIM), lambda i: (i, slab_col)),
            pl.BlockSpec((1, cfg.kv_lora), lambda i: (0, 0)),
            pl.BlockSpec((tm, HEAD_DIM), lambda i: (i, 0)),
            pl.BlockSpec((nh, cfg.kv_lora, 2 * HEAD_DIM), lambda i: (0, 0, 0)),
        ],
        out_specs=[
            pl.BlockSpec((nh, tm, QK_DIM), lambda i: (0, i, 0)),
            pl.BlockSpec((nh, tm, 2 * HEAD_DIM), lambda i: (0, i, 0)),
        ],
        out_shape=[
            jax.ShapeDtypeStruct((nh, cfg.nt, QK_DIM), BF16),
            jax.ShapeDtypeStruct((nh, cfg.nt, 2 * HEAD_DIM), BF16),
        ],
        compiler_params=_params(("arbitrary",)),
        name="mla_keys_values",
    )(p, p, norm_w.reshape(1, cfg.kv_lora), cs, w_kv)


ATTN_SUB = 256
ATTN_KT = 512
ATTN_SCORE_ROWS = 1024


def _softmax_pv(s_list, v_list):
    m = functools.reduce(jnp.maximum, [jnp.max(s, axis=-1, keepdims=True) for s in s_list])
    o = None
    for s, v in zip(s_list, v_list):
        part = _dot(jnp.exp2(s - m).astype(BF16), v)
        o = part if o is None else o + part
    return o[:, :HEAD_DIM] / o[:, HEAD_DIM:HEAD_DIM + 1]


def _attn_lat_kernel(q_ref, kl_ref, kc_ref, vl_ref, vc_ref, o_ref, s_ref):
    n_keys = kl_ref.shape[0]
    n_slots = s_ref.shape[0] // ATTN_SUB
    for r in range(0, q_ref.shape[0], ATTN_SUB):
        rows = slice(r, r + ATTN_SUB)
        slot = (r // ATTN_SUB) % n_slots
        srows = slice(slot * ATTN_SUB, (slot + 1) * ATTN_SUB)
        q = q_ref[rows, :]
        sc = _dot_nt(q, kc_ref[...])
        m = jnp.max(sc, axis=-1, keepdims=True)
        for c in range(0, n_keys, ATTN_KT):
            s = _dot_nt(q, kl_ref[c:c + ATTN_KT, :])
            s_ref[srows, c:c + ATTN_KT] = s
            m = jnp.maximum(m, jnp.max(s, axis=-1, keepdims=True))
        o = _dot(jnp.exp2(sc - m).astype(BF16), vc_ref[...])
        for c in range(0, n_keys, ATTN_KT):
            p = jnp.exp2((s_ref[srows, c:c + ATTN_KT] - m).astype(BF16))
            o = o + _dot(p, vl_ref[c:c + ATTN_KT, :])
        o_ref[rows, :] = (o[:, :HEAD_DIM] / o[:, HEAD_DIM:HEAD_DIM + 1]).astype(o_ref.dtype)


def _attn_ctx_kernel(q_ref, kc_ref, vc_ref, o_ref):
    o_ref[...] = _softmax_pv([_dot_nt(q_ref[...], kc_ref[...])], [vc_ref[...]]).astype(o_ref.dtype)


def mla_attention(cfg, q, k, v, *, with_ctx, out_rows):
    nh = cfg.mla_heads
    vw = 2 * HEAD_DIM
    tq = _pick(cfg.seq, (2048, 1024, 512, 256))
    nq = cfg.seq // tq
    s_rows = min(tq, ATTN_SCORE_ROWS)
    cb0 = cfg.nl // cfg.ctx
    y_lat = pl.pallas_call(
        _attn_lat_kernel,
        grid=(cfg.batch, nh, nq),
        in_specs=[
            pl.BlockSpec((None, tq, QK_DIM), lambda b, h, i: (h, b * nq + i, 0)),
            pl.BlockSpec((None, cfg.seq, QK_DIM), lambda b, h, i: (h, b, 0)),
            pl.BlockSpec((None, cfg.ctx, QK_DIM), lambda b, h, i: (h, cb0 + b, 0)),
            pl.BlockSpec((None, cfg.seq, vw), lambda b, h, i: (h, b, 0)),
            pl.BlockSpec((None, cfg.ctx, vw), lambda b, h, i: (h, cb0 + b, 0)),
        ],
        out_specs=pl.BlockSpec((tq, HEAD_DIM), lambda b, h, i: (b * nq + i, h)),
        out_shape=jax.ShapeDtypeStruct((out_rows, nh * HEAD_DIM), BF16),
        scratch_shapes=[pltpu.VMEM((s_rows, cfg.seq), F32)],
        compiler_params=_params(("arbitrary", "arbitrary", "arbitrary")),
        name="mla_attention",
    )(q, k, k, v, v)
    if not with_ctx:
        return y_lat, None
    y_ctx = pl.pallas_call(
        _attn_ctx_kernel,
        grid=(cfg.batch, nh),
        in_specs=[
            pl.BlockSpec((None, cfg.ctx, QK_DIM), lambda b, h: (h, cb0 + b, 0)),
            pl.BlockSpec((None, cfg.ctx, QK_DIM), lambda b, h: (h, cb0 + b, 0)),
            pl.BlockSpec((None, cfg.ctx, vw), lambda b, h: (h, cb0 + b, 0)),
        ],
        out_specs=pl.BlockSpec((cfg.ctx, HEAD_DIM), lambda b, h: (b, h)),
        out_shape=jax.ShapeDtypeStruct((cfg.nc, nh * HEAD_DIM), BF16),
        compiler_params=_params(("arbitrary", "arbitrary")),
        name="mla_attention_ctx",
    )(q, k, v)
    return y_lat, y_ctx


def _merge_kernel(ya_ref, yb_ref, yc_ref, ga_ref, gb_ref, gc_ref, wa_ref, wb_ref, wc_ref, o_ref):
    m = (_sigmoid(ga_ref[...]) * _dot(ya_ref[...], wa_ref[...])
         + _sigmoid(gb_ref[...]) * _dot(yb_ref[...], wb_ref[...])
         + _sigmoid(gc_ref[...]) * _dot(yc_ref[...], wc_ref[...]))
    o_ref[...] = m.astype(o_ref.dtype)


def merge_branches(cfg, ya, yb, yc, p, wa, wb, wc, *, rows):
    tm = _pick(cfg.nc, (512, 256))
    tn = _pick(cfg.d, (512, 256))
    gcols = cfg.d // tn

    def gate_spec(k):
        return pl.BlockSpec((tm, tn), lambda i, j, k=k: (i, k * gcols + j))

    return pl.pallas_call(
        _merge_kernel,
        grid=(rows // tm, cfg.d // tn),
        in_specs=[
            pl.BlockSpec((tm, ya.shape[1]), lambda i, j: (i, 0)),
            pl.BlockSpec((tm, yb.shape[1]), lambda i, j: (i, 0)),
            pl.BlockSpec((tm, yc.shape[1]), lambda i, j: (i, 0)),
            gate_spec(0), gate_spec(1), gate_spec(2),
            pl.BlockSpec((wa.shape[0], tn), lambda i, j: (0, j)),
            pl.BlockSpec((wb.shape[0], tn), lambda i, j: (0, j)),
            pl.BlockSpec((wc.shape[0], tn), lambda i, j: (0, j)),
        ],
        out_specs=pl.BlockSpec((tm, tn), lambda i, j: (i, j)),
        out_shape=jax.ShapeDtypeStruct((rows, cfg.d), BF16),
        compiler_params=_params(("arbitrary", "arbitrary")),
        name="merge_branches",
    )(ya, yb, yc, p, p, p, wa, wb, wc)


def _matmul_resid_kernel(a_ref, w_ref, x_ref, mod_ref, o_ref, *, k):
    o_ref[...] = x_ref[...] + mod_ref[k:k + 1, :] * _dot(a_ref[...], w_ref[...])


def matmul_residual(cfg, a, w, x, mod_l, *, k, rows):
    tm = _pick(cfg.nc, (1024, 512, 256))
    tn = _pick(cfg.d, (512, 256))
    row_of = _mod_row_map(cfg, tm)
    return pl.pallas_call(
        functools.partial(_matmul_resid_kernel, k=k),
        grid=(rows // tm, cfg.d // tn),
        in_specs=[
            pl.BlockSpec((tm, a.shape[1]), lambda i, j: (i, 0)),
            pl.BlockSpec((a.shape[1], tn), lambda i, j: (0, j)),
            pl.BlockSpec((tm, tn), lambda i, j: (i, j)),
            pl.BlockSpec((None, 6, tn), lambda i, j: (row_of(i), 0, j)),
        ],
        out_specs=pl.BlockSpec((tm, tn), lambda i, j: (i, j)),
        out_shape=jax.ShapeDtypeStruct((rows, cfg.d), F32),
        compiler_params=_params(("arbitrary", "arbitrary")),
        name="matmul_residual",
    )(a, w, x, mod_l)


def _top16_pass(s_ref, top_ref, rank_ref, break_ties):
    s = s_ref[...]
    n = s.shape[0]
    row = lax.broadcasted_iota(jnp.int32, s.shape, 0).astype(F32)
    rank = jnp.full(s.shape, UNRANKED, F32)
    work = s
    for r in range(PEER_TOPK):
        m = jnp.max(work, axis=0, keepdims=True)
        sel = work == m
        if break_ties:
            sel = row == jnp.min(jnp.where(sel, row, float(n)), axis=0, keepdims=True)
        rank = jnp.where(sel, float(r), rank)
        work = jnp.where(sel, -jnp.inf, work)
        top_ref[r:r + 1, :] = m
    rank_ref[...] = rank


def _peer_tables_kernel(qh_ref, keys_ref, b_ref, e2_ref, c_ref, e1_ref, cnt_ref, *scratch, tile):
    lanes = HEAD_DIM
    n_arr = 2 * (tile // lanes)
    s_refs, top_refs, rank_refs = (scratch[i * n_arr:(i + 1) * n_arr] for i in range(3))
    k1 = keys_ref[0]
    k2 = keys_ref[1]
    ranked = jnp.zeros((1, lanes), F32)
    for part in range(tile // lanes):
        qh = qh_ref[part * lanes:(part + 1) * lanes, :].astype(BF16)
        s_refs[2 * part][...] = _dot_nt(k1, qh[:, :lanes])
        s_refs[2 * part + 1][...] = _dot_nt(k2, qh[:, lanes:])
        for idx in (2 * part, 2 * part + 1):
            _top16_pass(s_refs[idx], top_refs[idx], rank_refs[idx], break_ties=False)
            n_ranked = jnp.sum(jnp.where(rank_refs[idx][...] < UNRANKED, 1.0, 0.0), axis=0, keepdims=True)
            ranked = jnp.maximum(ranked, n_ranked)

    @pl.when(jnp.max(ranked) > float(PEER_TOPK))
    def _():
        for idx in range(n_arr):
            _top16_pass(s_refs[idx], top_refs[idx], rank_refs[idx], break_ties=True)

    for part in range(tile // lanes):
        cols = slice(part * lanes, (part + 1) * lanes)
        s1 = s_refs[2 * part][...]
        s2 = s_refs[2 * part + 1][...]
        rank1 = rank_refs[2 * part][...]
        rank2 = rank_refs[2 * part + 1][...]
        t1 = top_refs[2 * part][...]
        t2 = top_refs[2 * part + 1][...]
        pieces = [t1[0:1] + t2]
        pos = [lax.broadcasted_iota(jnp.int32, (PEER_TOPK, lanes), 0).astype(F32)]
        for a in range(1, 8):
            pieces.append(t1[a:a + 1] + t2[0:8])
            pos.append(lax.broadcasted_iota(jnp.int32, (8, lanes), 0).astype(F32) + float(a * PEER_TOPK))
        pieces.append(t1[8:16] + t2[0:1])
        pos.append((lax.broadcasted_iota(jnp.int32, (8, lanes), 0).astype(F32) + 8.0) * float(PEER_TOPK))
        cand = jnp.concatenate(pieces, axis=0)
        cpos = jnp.concatenate(pos, axis=0)
        a_row = lax.broadcasted_iota(jnp.int32, (PEER_TOPK, lanes), 0).astype(F32)
        cnt = jnp.zeros((PEER_TOPK, lanes), F32)
        z = jnp.zeros((1, lanes), F32)
        best0 = None
        for r in range(PEER_TOPK):
            m = jnp.max(cand, axis=0, keepdims=True)
            first = jnp.min(jnp.where(cand == m, cpos, 1e9), axis=0, keepdims=True)
            cand = jnp.where(cpos == first, -jnp.inf, cand)
            cnt = cnt + jnp.where(a_row == jnp.floor(first * (1.0 / PEER_TOPK)), 1.0, 0.0)
            if r == 0:
                best0 = m
            z = z + jnp.exp(m - best0)
        cnt_ref[...] = cnt
        c_tab = jnp.zeros((PEER_NKEYS, lanes), F32)
        for a in range(PEER_TOPK):
            c_tab = jnp.where(rank1 == float(a), cnt_ref[a:a + 1, :], c_tab)
        b_ref[:, cols] = rank2.astype(b_ref.dtype)
        c_ref[:, cols] = c_tab
        e1_ref[:, cols] = jnp.exp(s1 - t1[0:1]) / z
        e2_ref[:, cols] = jnp.exp(s2 - t2[0:1]).astype(e2_ref.dtype)


def peer_tables(cfg, qh, keys, *, rows):
    tile = 256
    n_arr = 2 * (tile // HEAD_DIM)
    nh = cfg.peer_heads
    shapes = [jax.ShapeDtypeStruct((nh, PEER_NKEYS, rows), dt) for dt in (BF16, BF16, F32, F32)]
    out_spec = pl.BlockSpec((None, PEER_NKEYS, tile), lambda i, h: (h, 0, i))
    return pl.pallas_call(
        functools.partial(_peer_tables_kernel, tile=tile),
        grid=(rows // tile, nh),
        in_specs=[
            pl.BlockSpec((tile, 2 * HEAD_DIM), lambda i, h: (i, h)),
            pl.BlockSpec((2, None, PEER_NKEYS, HEAD_DIM), lambda i, h: (0, h, 0, 0)),
        ],
        out_specs=[out_spec] * 4,
        out_shape=shapes,
        scratch_shapes=([pltpu.VMEM((PEER_TOPK, HEAD_DIM), F32)]
                        + [pltpu.VMEM((PEER_NKEYS, HEAD_DIM), F32) for _ in range(n_arr)]
                        + [pltpu.VMEM((PEER_TOPK, HEAD_DIM), F32) for _ in range(n_arr)]
                        + [pltpu.VMEM((PEER_NKEYS, HEAD_DIM), F32) for _ in range(n_arr)]),
        compiler_params=_params(("arbitrary", "arbitrary")),
        name="peer_tables",
    )(qh, keys)


PEER_ROWS = 8
PEER_EB = PEER_ROWS * PEER_NKEYS
PEER_DOT_ROWS = 1


def _peer_dense_kernel(ht_ref, u_ref, vt_ref, b_ref, e2_ref, c_ref, e1_ref, x_ref, mod_ref, o_ref,
                       acc_ref, w_ref, *, heads):
    e = pl.program_id(1)

    @pl.when(e == 0)
    def _():
        acc_ref[...] = jnp.zeros_like(acc_ref)

    zero = jnp.zeros((), BF16)
    ht = ht_ref[...]
    for r0 in range(0, PEER_ROWS, PEER_DOT_ROWS):
        blk = slice(r0 * PEER_NKEYS, (r0 + PEER_DOT_ROWS) * PEER_NKEYS)
        act = _dot(u_ref[blk, :], ht)
        gelu = (0.5 * act * (1.0 + lax.erf(act * float(math.sqrt(0.5))))).astype(BF16)
        for q in range(PEER_DOT_ROWS):
            r = r0 + q
            g = jnp.zeros((PEER_NKEYS, act.shape[1]), BF16)
            for h in range(heads):
                crow = jnp.broadcast_to(c_ref[h, r:r + 1, :], g.shape).astype(BF16)
                erow = jnp.broadcast_to(e1_ref[h, r:r + 1, :], g.shape).astype(BF16)
                g = g + jnp.where(b_ref[h] < crow, e2_ref[h] * erow, zero)
            w_ref[r * PEER_NKEYS:(r + 1) * PEER_NKEYS, :] = g * gelu[q * PEER_NKEYS:(q + 1) * PEER_NKEYS]
    acc_ref[...] += _dot(vt_ref[...], w_ref[...])

    @pl.when(e == pl.num_programs(1) - 1)
    def _():
        o_ref[...] = x_ref[...] + mod_ref[5:6, :] * acc_ref[...].T


def peer_dense(cfg, ht, u, vt, tabs, x, mod_l, *, layer, rows):
    tile = _pick(cfg.nc, (512, 256))
    eb = PEER_EB
    n_exp = u.shape[1]
    nh = cfg.peer_heads
    row_of = _mod_row_map(cfg, tile)
    tab_spec = pl.BlockSpec((nh, PEER_NKEYS, tile), lambda i, e: (0, 0, i))
    row_spec = pl.BlockSpec((nh, PEER_ROWS, tile), lambda i, e: (0, e, i))
    return pl.pallas_call(
        functools.partial(_peer_dense_kernel, heads=nh),
        grid=(rows // tile, n_exp // eb),
        in_specs=[
            pl.BlockSpec((cfg.d, tile), lambda i, e: (0, i)),
            pl.BlockSpec((None, eb, cfg.d), lambda i, e: (layer, e, 0)),
            pl.BlockSpec((None, cfg.d, eb), lambda i, e: (layer, 0, e)),
            tab_spec, tab_spec, row_spec, row_spec,
            pl.BlockSpec((tile, cfg.d), lambda i, e: (i, 0)),
            pl.BlockSpec((None, 6, cfg.d), lambda i, e: (row_of(i), 0, 0)),
        ],
        out_specs=pl.BlockSpec((tile, cfg.d), lambda i, e: (i, 0)),
        out_shape=jax.ShapeDtypeStruct((rows, cfg.d), F32),
        scratch_shapes=[pltpu.VMEM((cfg.d, tile), F32), pltpu.VMEM((eb, tile), BF16)],
        compiler_params=_params(("arbitrary", "arbitrary")),
        name="peer_dense",
    )(ht, u, vt, *tabs, x, mod_l)


def _final_norm_kernel(x_ref, w_ref, o_ref):
    x = x_ref[...]
    o_ref[...] = x * lax.rsqrt(jnp.mean(x * x, axis=-1, keepdims=True) + RMS_EPS) * w_ref[...]


def final_norm(cfg, x, w):
    tm = _pick(cfg.nl, (1024, 512, 256))
    return pl.pallas_call(
        _final_norm_kernel,
        grid=(cfg.nl // tm,),
        in_specs=[pl.BlockSpec((tm, cfg.d), lambda i: (i, 0)), pl.BlockSpec((1, cfg.d), lambda i: (0, 0))],
        out_specs=pl.BlockSpec((tm, cfg.d), lambda i: (i, 0)),
        out_shape=jax.ShapeDtypeStruct((cfg.nl, cfg.d), F32),
        compiler_params=_params(("arbitrary",)),
        name="final_norm",
    )(x, w.reshape(1, cfg.d))


def _rot_cols(w):
    q = ROPE_DIM // 4
    return jnp.concatenate([-w[..., q:2 * q], w[..., 0:q], -w[..., 3 * q:4 * q], w[..., 2 * q:3 * q]], axis=-1)


def _in_proj_weight(cfg, w_in, n_cols):
    d = cfg.d
    hg_end = 5 * cfg.hgw
    pool_end = hg_end + POOL_WIDTH
    cq_end = pool_end + cfg.q_lora
    ckv_end = cq_end + cfg.kv_lora
    rope_end = ckv_end + ROPE_DIM
    w_in = w_in.astype(BF16)
    k_rope = w_in[..., ckv_end:rope_end]
    parts = [w_in[..., rope_end:rope_end + 3 * d], w_in[..., :ckv_end], k_rope, _rot_cols(k_rope)]
    parts.append(jnp.zeros(w_in.shape[:-1] + (n_cols - cfg.in_cols,), BF16))
    return jnp.concatenate(parts, axis=-1)


def _mla_q_weight(cfg, w_uq):
    w = w_uq.reshape(cfg.q_lora, cfg.mla_heads, HEAD_DIM + ROPE_DIM)
    rope = w[..., HEAD_DIM:]
    w = jnp.concatenate([w[..., :HEAD_DIM], rope, _rot_cols(rope)], axis=-1)
    return jnp.transpose(w, (1, 0, 2)).astype(BF16)


def _mla_kv_weight(cfg, w_ukv):
    w = w_ukv.reshape(cfg.kv_lora, cfg.mla_heads, 2 * HEAD_DIM)
    return jnp.transpose(w, (1, 0, 2)).astype(BF16)


def _rope_table(cfg):
    rows = cfg.seq // cfg.grid_w
    r, col = jnp.meshgrid(jnp.arange(rows), jnp.arange(cfg.grid_w), indexing="ij")
    n_freq = ROPE_DIM // 4
    freqs = ROPE_THETA ** (-jnp.arange(n_freq, dtype=F32) / n_freq)
    ang_r = r.reshape(-1)[:, None] * freqs
    ang_c = col.reshape(-1)[:, None] * freqs
    cos = jnp.concatenate([jnp.cos(ang_r)] * 2 + [jnp.cos(ang_c)] * 2, axis=1)
    sin = jnp.concatenate([jnp.sin(ang_r)] * 2 + [jnp.sin(ang_c)] * 2, axis=1)
    lat = jnp.tile(jnp.concatenate([cos, sin], axis=1).astype(F32), (cfg.batch, 1))
    ctx = jnp.concatenate([jnp.ones((cfg.nc, ROPE_DIM), F32), jnp.zeros((cfg.nc, ROPE_DIM), F32)], axis=1)
    return jnp.concatenate([lat, ctx], axis=0)


def _forward(cfg, x, c, ctx, c_ctx, w_mod, b_mod, norm_mix, norm_ffn, w_in, hg_lb_logits, hg_norm,
             pool_w, pool_scale, mla_q_norm, mla_w_uq, mla_kv_norm, mla_w_ukv,
             w_branch_a, w_branch_b, w_branch_c, w_out, peer_wq, peer_keys, peer_u, peer_v, final_w):
    d = cfg.d
    assert cfg.seq % HG_CHUNK == 0 and cfg.ctx % HG_CHUNK == 0 and cfg.nl % cfg.ctx == 0
    assert cfg.off_pool % POOL_WIDTH == 0 and cfg.off_cq % cfg.q_lora == 0
    assert cfg.off_ckv % cfg.kv_lora == 0 and cfg.batch < MOD_ROWS
    tn_in = 768 if d % 256 == 0 and cfg.in_cols > 8192 else 256
    n_cols = -(-cfg.in_cols // tn_in) * tn_in
    tm = _pick(cfg.nc, (1024, 512, 256))

    xs = jnp.concatenate([x.reshape(cfg.nl, d), ctx.reshape(cfg.nc, d)], axis=0)
    c_all = jnp.concatenate([c, c_ctx[None], jnp.zeros((MOD_ROWS - cfg.batch - 1, d), F32)], axis=0)
    mod = adaln_tables(cfg, c_all, w_mod, b_mod)
    cs = _rope_table(cfg)
    w_in_all = _in_proj_weight(cfg, w_in, n_cols)
    wq_all = peer_wq.astype(BF16)
    u_all = peer_u.astype(BF16)
    vt_all = jnp.swapaxes(peer_v.astype(BF16), 1, 2)

    for l in range(cfg.depth):
        last = l == cfg.depth - 1
        rows = cfg.nl if last else cfg.nt
        mod_l = mod[l]
        p = norm_matmul(cfg, xs, norm_mix[l], mod_l, w_in_all, layer=l,
                        k0=0, rows=cfg.nt, tm=tm, tn=tn_in, emit_h=False)
        ya, hg_ctx = hgrn_mixer(cfg, p, hg_lb_logits, hg_norm[l], l, rows)
        pw = pool_w[l].astype(BF16)
        yb = pool_mixer(cfg, p, pw, pool_scale[l], n_seq=cfg.batch, seq_len=cfg.seq, row0=0, out_rows=rows)
        q = mla_queries(cfg, p, mla_q_norm[l], _mla_q_weight(cfg, mla_w_uq[l]), cs)
        k, v = mla_keys_values(cfg, p, mla_kv_norm[l], _mla_kv_weight(cfg, mla_w_ukv[l]), cs)
        yc, att_ctx = mla_attention(cfg, q, k, v, with_ctx=not last, out_rows=rows)
        if not last:
            pool_ctx = pool_mixer(cfg, p, pw, pool_scale[l], n_seq=cfg.batch, seq_len=cfg.ctx, row0=cfg.nl)
            ya = lax.dynamic_update_slice(ya, hg_ctx, (cfg.nl, 0))
            yb = lax.dynamic_update_slice(yb, pool_ctx, (cfg.nl, 0))
            yc = lax.dynamic_update_slice(yc, att_ctx, (cfg.nl, 0))
        m = merge_branches(cfg, ya, yb, yc, p, w_branch_a[l].astype(BF16), w_branch_b[l].astype(BF16),
                           w_branch_c[l].astype(BF16), rows=rows)
        xs = matmul_residual(cfg, m, w_out[l].astype(BF16), xs, mod_l, k=2, rows=rows)
        qh, h2 = norm_matmul(cfg, xs, norm_ffn[l], mod_l, wq_all, layer=l,
                             k0=3, rows=rows, tm=tm, tn=_pick(peer_wq.shape[2], (512, 256)), emit_h=True)
        tabs = peer_tables(cfg, qh, peer_keys[l].astype(BF16), rows=rows)
        xs = peer_dense(cfg, h2.T, u_all, vt_all, tabs, xs, mod_l, layer=l, rows=rows)

    return final_norm(cfg, xs, final_w).reshape(cfg.batch, cfg.seq, d)


def kernel(x, c, ctx, c_ctx, w_mod, b_mod, norm_mix, norm_ffn, w_in, hg_lb_logits, hg_norm, pool_w, pool_scale,
           mla_q_norm, mla_w_uq, mla_kv_norm, mla_w_ukv, w_branch_a, w_branch_b, w_branch_c, w_out,
           peer_wq, peer_keys, peer_u, peer_v, final_norm):
    batch, seq, d = x.shape
    cfg = Cfg(d=d, batch=batch, seq=seq, ctx=ctx.shape[1], grid_w=64, depth=w_mod.shape[0],
              hg_heads=hg_lb_logits.shape[2] // HEAD_DIM,
              mla_heads=mla_w_ukv.shape[2] // (2 * HEAD_DIM), q_lora=mla_q_norm.shape[1],
              kv_lora=mla_kv_norm.shape[1], peer_heads=peer_keys.shape[2])
    return _forward(cfg, x, c, ctx, c_ctx, w_mod, b_mod, norm_mix, norm_ffn, w_in, hg_lb_logits, hg_norm,
                    pool_w, pool_scale, mla_q_norm, mla_w_uq, mla_kv_norm, mla_w_ukv,
                    w_branch_a, w_branch_b, w_branch_c, w_out, peer_wq, peer_keys, peer_u, peer_v, final_norm)
```

```python
import functools
import math
from typing import NamedTuple

import numpy as np
import jax
import jax.numpy as jnp
from jax import lax
from jax.experimental import pallas as pl
from jax.experimental.pallas import tpu as pltpu

F32 = jnp.float32
BF16 = jnp.bfloat16

RMS_EPS = 1e-6
ROPE_THETA = 10000.0
HEAD_DIM = 128
ROPE_DIM = 64
QK_DIM = 256
LOG2_E = float(math.log2(math.e))
QK_LOG2_SCALE = float((HEAD_DIM + ROPE_DIM) ** -0.5) * LOG2_E
POOL_WINDOWS = (2, 4, 8, 16)
POOL_GROUP = 256
POOL_WIDTH = POOL_GROUP * len(POOL_WINDOWS)
POOL_HALO = 8
PEER_NKEYS = 128
PEER_TOPK = 16
HG_CHUNK = 128
HG_LEVELS = 7
UNRANKED = 99.0
MOD_ROWS = 8
VMEM_LIMIT = 56 * 1024 * 1024


class Cfg(NamedTuple):
    d: int
    batch: int
    seq: int
    ctx: int
    grid_w: int
    depth: int
    hg_heads: int
    mla_heads: int
    q_lora: int
    kv_lora: int
    peer_heads: int

    @property
    def nl(self):
        return self.batch * self.seq

    @property
    def nc(self):
        return self.batch * self.ctx

    @property
    def nt(self):
        return self.nl + self.nc

    @property
    def hgw(self):
        return self.hg_heads * HEAD_DIM

    @property
    def off_gate(self):
        return 0

    @property
    def off_hg(self):
        return 3 * self.d

    @property
    def off_pool(self):
        return self.off_hg + 5 * self.hgw

    @property
    def off_cq(self):
        return self.off_pool + POOL_WIDTH

    @property
    def off_ckv(self):
        return self.off_cq + self.q_lora

    @property
    def off_slab(self):
        return self.off_ckv + self.kv_lora

    @property
    def in_cols(self):
        return self.off_slab + HEAD_DIM


def _params(sem):
    return pltpu.CompilerParams(dimension_semantics=sem, vmem_limit_bytes=VMEM_LIMIT)


def _dot(a, b):
    return jnp.dot(a, b, preferred_element_type=F32)


def _dot_nt(a, b):
    return lax.dot_general(a, b, (((1,), (1,)), ((), ())), preferred_element_type=F32)


def _sigmoid(x):
    return 1.0 / (1.0 + jnp.exp(-x))


def _pick(n, prefs):
    for p in prefs:
        if n % p == 0:
            return p
    raise ValueError(f"no tile for {n} in {prefs}")


def _mod_row_map(cfg, tm):
    n_lat = cfg.nl // tm
    per_batch = cfg.seq // tm
    return lambda i: jnp.where(i < n_lat, i // per_batch, cfg.batch)


def _mod_kernel(c_ref, w_ref, b_ref, o_ref):
    c = c_ref[...]
    s = (c * _sigmoid(c)).astype(BF16)
    o_ref[...] = _dot(s, w_ref[...].astype(BF16)) + b_ref[...]


def adaln_tables(cfg, c_all, w_mod, b_mod):
    d6 = 6 * cfg.d
    tn = _pick(d6, (1024, 768, 512, 256))
    out = pl.pallas_call(
        _mod_kernel,
        grid=(cfg.depth, d6 // tn),
        in_specs=[
            pl.BlockSpec((MOD_ROWS, cfg.d), lambda l, j: (0, 0)),
            pl.BlockSpec((None, cfg.d, tn), lambda l, j: (l, 0, j)),
            pl.BlockSpec((None, 1, tn), lambda l, j: (l, 0, j)),
        ],
        out_specs=pl.BlockSpec((None, MOD_ROWS, tn), lambda l, j: (l, 0, j)),
        out_shape=jax.ShapeDtypeStruct((cfg.depth, MOD_ROWS, d6), F32),
        compiler_params=_params(("arbitrary", "arbitrary")),
        name="adaln_tables",
    )(c_all, w_mod, b_mod.reshape(cfg.depth, 1, d6))
    return out.reshape(cfg.depth, MOD_ROWS, 6, cfg.d)


def _norm_matmul_kernel(x_ref, gain_ref, mod_ref, w_ref, *rest, k0, emit_h):
    if emit_h:
        o_ref, hout_ref, h_ref = rest
    else:
        o_ref, h_ref = rest

    @pl.when(pl.program_id(1) == 0)
    def _():
        x = x_ref[...]
        y = x * lax.rsqrt(jnp.mean(x * x, axis=-1, keepdims=True) + RMS_EPS) * gain_ref[...]
        h = (y * (1.0 + mod_ref[k0 + 1:k0 + 2, :]) + mod_ref[k0:k0 + 1, :]).astype(BF16)
        h_ref[...] = h
        if emit_h:
            hout_ref[...] = h.T

    o_ref[...] = _dot(h_ref[...], w_ref[...])


def norm_matmul(cfg, x, gain, mod_l, w, *, layer, k0, rows, tm, tn, emit_h):
    n = w.shape[2]
    row_of = _mod_row_map(cfg, tm)
    out_shape = [jax.ShapeDtypeStruct((rows, n), F32)]
    out_specs = [pl.BlockSpec((tm, tn), lambda i, j: (i, j))]
    if emit_h:
        out_shape.append(jax.ShapeDtypeStruct((cfg.d, rows), BF16))
        out_specs.append(pl.BlockSpec((cfg.d, tm), lambda i, j: (0, i)))
    res = pl.pallas_call(
        functools.partial(_norm_matmul_kernel, k0=k0, emit_h=emit_h),
        grid=(rows // tm, n // tn),
        in_specs=[
            pl.BlockSpec((tm, cfg.d), lambda i, j: (i, 0)),
            pl.BlockSpec((1, cfg.d), lambda i, j: (0, 0)),
            pl.BlockSpec((None, 6, cfg.d), lambda i, j: (row_of(i), 0, 0)),
            pl.BlockSpec((None, cfg.d, tn), lambda i, j: (layer, 0, j)),
        ],
        out_specs=out_specs,
        out_shape=out_shape,
        scratch_shapes=[pltpu.VMEM((tm, cfg.d), BF16)],
        compiler_params=_params(("arbitrary", "arbitrary")),
        name="norm_matmul_h" if emit_h else "norm_matmul",
    )(x, gain.reshape(1, cfg.d), mod_l, w)
    return res if emit_h else res[0]


def _hgrn_consts():
    c = HG_CHUNK
    out = []
    for rev in (False, True):
        p = np.arange(c) if not rev else c - 1 - np.arange(c)
        pt, pu = p[:, None], p[None, :]
        g = np.zeros((HG_LEVELS + 1, c, c), np.float32)
        up = np.zeros((HG_LEVELS, c, c), np.float32)
        g[0] = pu <= pt
        for l in range(HG_LEVELS):
            m = 1 << l
            blk = p >> (l + 1)
            upper = ((p >> l) & 1) == 1
            mid = (blk * 2 * m + m)[:, None]
            same = blk[:, None] == blk[None, :]
            g_up = same & (pu >= mid) & (pu <= pt)
            g_lo = same & (pu > pt) & (pu < mid)
            g[1 + l] = np.where(upper[:, None], g_up, g_lo)
            up[l] = np.broadcast_to(upper[:, None], (c, c))
        x = pt ^ pu
        lv = np.where(pu < pt, np.floor(np.log2(np.maximum(x, 1))), -1.0).astype(np.float32)
        out.append((jnp.asarray(g.reshape(-1, c), BF16), jnp.asarray(up, F32), jnp.asarray(lv, F32)))
    return out


def _hgrn_chunk(q_raw, v, f_raw, log_lb, log_1mlb, one_m_lb, g_ref, up_ref, lv_ref, st_ref, end_row):
    c = HG_CHUNK
    q = q_raw * _sigmoid(q_raw)
    e = jnp.exp(-jnp.abs(f_raw))
    one_pe = 1.0 + e
    log_sig = jnp.minimum(f_raw, 0.0) - jnp.log(one_pe)
    t = log_1mlb + log_sig
    log_f = jnp.maximum(log_lb, t) + jnp.log(1.0 + jnp.exp(-jnp.abs(log_lb - t)))
    k = one_m_lb * jnp.where(f_raw >= 0.0, e, 1.0) / one_pe
    hi = log_f.astype(BF16)
    lo = (log_f - hi.astype(F32)).astype(BF16)
    a2 = _dot(g_ref[...], jnp.concatenate([hi, lo], axis=1))
    a = a2[:, :HEAD_DIM] + a2[:, HEAD_DIM:]
    b = a[0:c]
    lv = lv_ref[...]
    scores = jnp.zeros((c, c), F32)
    for l in range(HG_LEVELS):
        e_l = jnp.exp(a[(1 + l) * c:(2 + l) * c])
        x = (jnp.where(up_ref[l] > 0.5, q, k) * e_l).astype(BF16)
        scores = jnp.where(lv == float(l), _dot_nt(x, x), scores)
    b_end = b[end_row:end_row + 1, :]
    qb = (q * jnp.exp(b)).astype(BF16)
    kd = (k * jnp.exp(b_end - b)).astype(BF16)
    st = st_ref[...]
    vb = v.astype(BF16)
    o = (_dot(scores.astype(BF16), vb) + _dot_nt(qb, st.astype(BF16))
         + jnp.sum(q * k, axis=-1, keepdims=True) * v)
    st_ref[...] = st * jnp.exp(b_end) + _dot(v.T.astype(BF16), kd)
    return o


def _hgrn_kernel(ql, ffl, fbl, il, gl, qc, ffc, fbc, ic, gc, logit_ref, nw_ref,
                 gf_ref, upf_ref, lvf_ref, gb_ref, upb_ref, lvb_ref,
                 yl_ref, yc_ref, ofl, obl, ofc, obc, stf, stb, *, layer, seq, ctx):
    c = HG_CHUNK
    depth = logit_ref.shape[0]
    lg = [logit_ref[dd] for dd in range(depth)]
    mx = functools.reduce(jnp.maximum, lg)
    ex = [jnp.exp(v - mx) for v in lg]
    tot = functools.reduce(jnp.add, ex)
    cum = [ex[0] / tot]
    for dd in range(1, layer + 1):
        cum.append(cum[-1] + ex[dd] / tot)
    lb = cum[layer] - cum[0]
    log_lb = jnp.log(lb)
    log_1mlb = jnp.log1p(-lb)
    one_m_lb = 1.0 - lb

    stf[...] = jnp.zeros_like(stf)
    stb[...] = jnp.zeros_like(stb)

    def segment(q_ref, ff_ref, fb_ref, i_ref, of_ref, ob_ref, n):
        def body(j, carry):
            rf = pl.multiple_of(j * c, c)
            rb = pl.multiple_of((n - 1 - j) * c, c)
            of_ref[pl.ds(rf, c), :] = _hgrn_chunk(
                q_ref[pl.ds(rf, c), :], i_ref[pl.ds(rf, c), :], ff_ref[pl.ds(rf, c), :],
                log_lb[0:1], log_1mlb[0:1], one_m_lb[0:1], gf_ref, upf_ref, lvf_ref, stf, c - 1)
            ob_ref[pl.ds(rb, c), :] = _hgrn_chunk(
                q_ref[pl.ds(rb, c), :], i_ref[pl.ds(rb, c), :], fb_ref[pl.ds(rb, c), :],
                log_lb[1:2], log_1mlb[1:2], one_m_lb[1:2], gb_ref, upb_ref, lvb_ref, stb, 0)
            return carry
        lax.fori_loop(0, n, body, 0, unroll=4 if n % 4 == 0 else 2)

    segment(qc, ffc, fbc, ic, ofc, obc, ctx // c)
    segment(ql, ffl, fbl, il, ofl, obl, seq // c)

    nw = nw_ref[...]

    def readout(of_ref, ob_ref, g_ref, y_ref, n):
        def body(j, carry):
            r = pl.multiple_of(j * c, c)
            o = of_ref[pl.ds(r, c), :] + ob_ref[pl.ds(r, c), :]
            y = o * lax.rsqrt(jnp.mean(o * o, axis=-1, keepdims=True) + RMS_EPS) * nw
            g = g_ref[pl.ds(r, c), :]
            y_ref[pl.ds(r, c), :] = (y * (g * _sigmoid(g))).astype(y_ref.dtype)
            return carry
        lax.fori_loop(0, n, body, 0)

    readout(ofc, obc, gc, yc_ref, ctx // c)
    readout(ofl, obl, gl, yl_ref, seq // c)


def hgrn_mixer(cfg, p, logits, norm_w, layer, out_rows):
    hd = HEAD_DIM
    nh = cfg.hg_heads
    col0 = cfg.off_hg // hd
    ctx_blk0 = cfg.nl // cfg.ctx
    (gf, upf, lvf), (gb, upb, lvb) = _hgrn_consts()

    def lat_spec(part):
        return pl.BlockSpec((cfg.seq, hd), lambda b, h, part=part: (b, col0 + part * nh + h))

    def ctx_spec(part):
        return pl.BlockSpec((cfg.ctx, hd), lambda b, h, part=part: (ctx_blk0 + b, col0 + part * nh + h))

    parts = (0, 1, 2, 3, 4)

    def const(arr):
        return pl.BlockSpec(arr.shape, lambda b, h, nd=arr.ndim: (0,) * nd)

    y_lat, y_ctx = pl.pallas_call(
        functools.partial(_hgrn_kernel, layer=layer, seq=cfg.seq, ctx=cfg.ctx),
        grid=(cfg.batch, nh),
        in_specs=[lat_spec(k) for k in parts] + [ctx_spec(k) for k in parts] + [
            pl.BlockSpec((cfg.depth, 2, hd), lambda b, h: (0, 0, h)),
            pl.BlockSpec((1, hd), lambda b, h: (0, 0)),
            const(gf), const(upf), const(lvf), const(gb), const(upb), const(lvb),
        ],
        out_specs=[
            pl.BlockSpec((cfg.seq, hd), lambda b, h: (b, h)),
            pl.BlockSpec((cfg.ctx, hd), lambda b, h: (b, h)),
        ],
        out_shape=[
            jax.ShapeDtypeStruct((out_rows, cfg.hgw), BF16),
            jax.ShapeDtypeStruct((cfg.nc, cfg.hgw), BF16),
        ],
        scratch_shapes=[
            pltpu.VMEM((cfg.seq, hd), F32), pltpu.VMEM((cfg.seq, hd), F32),
            pltpu.VMEM((cfg.ctx, hd), F32), pltpu.VMEM((cfg.ctx, hd), F32),
            pltpu.VMEM((hd, hd), F32), pltpu.VMEM((hd, hd), F32),
        ],
        compiler_params=_params(("arbitrary", "arbitrary")),
        name="hgrn_mixer",
    )(*([p] * 10), logits, norm_w.reshape(1, hd), gf, upf, lvf, gb, upb, lvb)
    return y_lat, y_ctx


def _pool_kernel(prev_ref, cur_ref, next_ref, w_ref, scale_ref, o_ref, buf_ref, *, seq_len, tile, n_tiles):
    i = pl.program_id(1)
    h = POOL_HALO
    cur = cur_ref[...]
    buf_ref[0:h, :] = jnp.where(i > 0, prev_ref[...], 0.0)
    buf_ref[h:h + tile, :] = cur
    buf_ref[h + tile:2 * h + tile, :] = jnp.where(i < n_tiles - 1, next_ref[...], 0.0)
    pos = i * tile + lax.broadcasted_iota(jnp.int32, (tile, POOL_GROUP), 0)
    for gi, win in enumerate(POOL_WINDOWS):
        half = win // 2
        cols = slice(gi * POOL_GROUP, (gi + 1) * POOL_GROUP)
        acc = buf_ref[h - half:h - half + tile, cols]
        for dlt in range(-half + 1, half):
            acc = acc + buf_ref[h + dlt:h + dlt + tile, cols]
        cnt = (jnp.minimum(pos + half, seq_len) - jnp.maximum(pos - half, 0)).astype(F32)
        pooled = acc / cnt - cur[:, cols]
        mixed = _dot(pooled.astype(BF16), w_ref[gi])
        o_ref[:, cols] = (mixed * scale_ref[:, cols]).astype(o_ref.dtype)


def pool_mixer(cfg, p, w_pool, scale, *, n_seq, seq_len, row0, out_rows=None):
    out_rows = n_seq * seq_len if out_rows is None else out_rows
    tile = _pick(seq_len, (512, 256, 128))
    n_tiles = seq_len // tile
    blk0 = row0 // tile
    col = cfg.off_pool // POOL_WIDTH
    per8 = tile // POOL_HALO
    last8 = cfg.nt // POOL_HALO - 1

    def cur_map(s, i):
        return (blk0 + s * n_tiles + i, col)

    def prev_map(s, i):
        return (jnp.maximum((blk0 + s * n_tiles + i) * per8 - 1, 0), col)

    def next_map(s, i):
        return (jnp.minimum((blk0 + s * n_tiles + i + 1) * per8, last8), col)

    return pl.pallas_call(
        functools.partial(_pool_kernel, seq_len=seq_len, tile=tile, n_tiles=n_tiles),
        grid=(n_seq, n_tiles),
        in_specs=[
            pl.BlockSpec((POOL_HALO, POOL_WIDTH), prev_map),
            pl.BlockSpec((tile, POOL_WIDTH), cur_map),
            pl.BlockSpec((POOL_HALO, POOL_WIDTH), next_map),
            pl.BlockSpec((len(POOL_WINDOWS), POOL_GROUP, POOL_GROUP), lambda s, i: (0, 0, 0)),
            pl.BlockSpec((1, POOL_WIDTH), lambda s, i: (0, 0)),
        ],
        out_specs=pl.BlockSpec((tile, POOL_WIDTH), lambda s, i: (s * n_tiles + i, 0)),
        out_shape=jax.ShapeDtypeStruct((out_rows, POOL_WIDTH), BF16),
        scratch_shapes=[pltpu.VMEM((tile + 2 * POOL_HALO, POOL_WIDTH), F32)],
        compiler_params=_params(("arbitrary", "arbitrary")),
        name="pool_mixer",
    )(p, p, p, w_pool, scale.reshape(1, POOL_WIDTH))


def _rope_slab(slab, cs):
    t = slab * cs
    r = t + pltpu.roll(t, ROPE_DIM, axis=1)
    lane = lax.broadcasted_iota(jnp.int32, r.shape, 1)
    return jnp.where(lane < ROPE_DIM, r, 0.0)


def _rms_bf16(x, w):
    return (x * lax.rsqrt(jnp.mean(x * x, axis=-1, keepdims=True) + RMS_EPS) * w).astype(BF16)


def _mla_q_kernel(cq_ref, nw_ref, cs_ref, w_ref, o_ref):
    n = _rms_bf16(cq_ref[...], nw_ref[...])
    cs = cs_ref[...]
    for h in range(w_ref.shape[0]):
        y = _dot(n, w_ref[h])
        rope = _rope_slab(y[:, HEAD_DIM:], cs)
        o_ref[h] = (jnp.concatenate([y[:, :HEAD_DIM], rope], axis=1) * QK_LOG2_SCALE).astype(o_ref.dtype)


def mla_queries(cfg, p, norm_w, w_q, cs):
    tm = _pick(cfg.nc, (512, 256))
    nh = cfg.mla_heads
    col = cfg.off_cq // cfg.q_lora
    return pl.pallas_call(
        _mla_q_kernel,
        grid=(cfg.nt // tm,),
        in_specs=[
            pl.BlockSpec((tm, cfg.q_lora), lambda i: (i, col)),
            pl.BlockSpec((1, cfg.q_lora), lambda i: (0, 0)),
            pl.BlockSpec((tm, HEAD_DIM), lambda i: (i, 0)),
            pl.BlockSpec((nh, cfg.q_lora, QK_DIM), lambda i: (0, 0, 0)),
        ],
        out_specs=pl.BlockSpec((nh, tm, QK_DIM), lambda i: (0, i, 0)),
        out_shape=jax.ShapeDtypeStruct((nh, cfg.nt, QK_DIM), BF16),
        compiler_params=_params(("arbitrary",)),
        name="mla_queries",
    )(p, norm_w.reshape(1, cfg.q_lora), cs, w_q)


def _mla_kv_kernel(ckv_ref, slab_ref, nw_ref, cs_ref, w_ref, k_ref, v_ref):
    n = _rms_bf16(ckv_ref[...], nw_ref[...])
    k_rope = _rope_slab(slab_ref[...], cs_ref[...])
    lane = lax.broadcasted_iota(jnp.int32, k_rope.shape, 1)
    ones_col = jnp.where(lane == 0, 1.0, 0.0)
    for h in range(w_ref.shape[0]):
        y = _dot(n, w_ref[h])
        k_ref[h] = jnp.concatenate([y[:, :HEAD_DIM], k_rope], axis=1).astype(k_ref.dtype)
        v_ref[h] = jnp.concatenate([y[:, HEAD_DIM:], ones_col], axis=1).astype(v_ref.dtype)


def mla_keys_values(cfg, p, norm_w, w_kv, cs):
    tm = _pick(cfg.nc, (512, 256))
    nh = cfg.mla_heads
    col = cfg.off_ckv // cfg.kv_lora
    slab_col = cfg.off_slab // HEAD_DIM
    return pl.pallas_call(
        _mla_kv_kernel,
        grid=(cfg.nt // tm,),
        in_specs=[
            pl.BlockSpec((tm, cfg.kv_lora), lambda i: (i, col)),
            pl.BlockSpec((tm, HEAD_DIM), lambda i: (i, slab_col)),
            pl.BlockSpec((1, cfg.kv_lora), lambda i: (0, 0)),
            pl.BlockSpec((tm, HEAD_DIM), lambda i: (i, 0)),
            pl.BlockSpec((nh, cfg.kv_lora, 2 * HEAD_DIM), lambda i: (0, 0, 0)),
        ],
        out_specs=[
            pl.BlockSpec((nh, tm, QK_DIM), lambda i: (0, i, 0)),
            pl.BlockSpec((nh, tm, 2 * HEAD_DIM), lambda i: (0, i, 0)),
        ],
        out_shape=[
            jax.ShapeDtypeStruct((nh, cfg.nt, QK_DIM), BF16),
            jax.ShapeDtypeStruct((nh, cfg.nt, 2 * HEAD_DIM), BF16),
        ],
        compiler_params=_params(("arbitrary",)),
        name="mla_keys_values",
    )(p, p, norm_w.reshape(1, cfg.kv_lora), cs, w_kv)


ATTN_SUB = 256
ATTN_KT = 512
ATTN_SCORE_ROWS = 1024


def _softmax_pv(s_list, v_list):
    m = functools.reduce(jnp.maximum, [jnp.max(s, axis=-1, keepdims=True) for s in s_list])
    o = None
    for s, v in zip(s_list, v_list):
        part = _dot(jnp.exp2(s - m).astype(BF16), v)
        o = part if o is None else o + part
    return o[:, :HEAD_DIM] / o[:, HEAD_DIM:HEAD_DIM + 1]


def _attn_lat_kernel(q_ref, kl_ref, kc_ref, vl_ref, vc_ref, o_ref, s_ref):
    n_keys = kl_ref.shape[0]
    n_slots = s_ref.shape[0] // ATTN_SUB
    for r in range(0, q_ref.shape[0], ATTN_SUB):
        rows = slice(r, r + ATTN_SUB)
        slot = (r // ATTN_SUB) % n_slots
        srows = slice(slot * ATTN_SUB, (slot + 1) * ATTN_SUB)
        q = q_ref[rows, :]
        sc = _dot_nt(q, kc_ref[...])
        m = jnp.max(sc, axis=-1, keepdims=True)
        for c in range(0, n_keys, ATTN_KT):
            s = _dot_nt(q, kl_ref[c:c + ATTN_KT, :])
            s_ref[srows, c:c + ATTN_KT] = s
            m = jnp.maximum(m, jnp.max(s, axis=-1, keepdims=True))
        o = _dot(jnp.exp2(sc - m).astype(BF16), vc_ref[...])
        for c in range(0, n_keys, ATTN_KT):
            p = jnp.exp2((s_ref[srows, c:c + ATTN_KT] - m).astype(BF16))
            o = o + _dot(p, vl_ref[c:c + ATTN_KT, :])
        o_ref[rows, :] = (o[:, :HEAD_DIM] / o[:, HEAD_DIM:HEAD_DIM + 1]).astype(o_ref.dtype)


def _attn_ctx_kernel(q_ref, kc_ref, vc_ref, o_ref):
    o_ref[...] = _softmax_pv([_dot_nt(q_ref[...], kc_ref[...])], [vc_ref[...]]).astype(o_ref.dtype)


def mla_attention(cfg, q, k, v, *, with_ctx, out_rows):
    nh = cfg.mla_heads
    vw = 2 * HEAD_DIM
    tq = _pick(cfg.seq, (2048, 1024, 512, 256))
    nq = cfg.seq // tq
    s_rows = min(tq, ATTN_SCORE_ROWS)
    cb0 = cfg.nl // cfg.ctx
    y_lat = pl.pallas_call(
        _attn_lat_kernel,
        grid=(cfg.batch, nh, nq),
        in_specs=[
            pl.BlockSpec((None, tq, QK_DIM), lambda b, h, i: (h, b * nq + i, 0)),
            pl.BlockSpec((None, cfg.seq, QK_DIM), lambda b, h, i: (h, b, 0)),
            pl.BlockSpec((None, cfg.ctx, QK_DIM), lambda b, h, i: (h, cb0 + b, 0)),
            pl.BlockSpec((None, cfg.seq, vw), lambda b, h, i: (h, b, 0)),
            pl.BlockSpec((None, cfg.ctx, vw), lambda b, h, i: (h, cb0 + b, 0)),
        ],
        out_specs=pl.BlockSpec((tq, HEAD_DIM), lambda b, h, i: (b * nq + i, h)),
        out_shape=jax.ShapeDtypeStruct((out_rows, nh * HEAD_DIM), BF16),
        scratch_shapes=[pltpu.VMEM((s_rows, cfg.seq), F32)],
        compiler_params=_params(("arbitrary", "arbitrary", "arbitrary")),
        name="mla_attention",
    )(q, k, k, v, v)
    if not with_ctx:
        return y_lat, None
    y_ctx = pl.pallas_call(
        _attn_ctx_kernel,
        grid=(cfg.batch, nh),
        in_specs=[
            pl.BlockSpec((None, cfg.ctx, QK_DIM), lambda b, h: (h, cb0 + b, 0)),
            pl.BlockSpec((None, cfg.ctx, QK_DIM), lambda b, h: (h, cb0 + b, 0)),
            pl.BlockSpec((None, cfg.ctx, vw), lambda b, h: (h, cb0 + b, 0)),
        ],
        out_specs=pl.BlockSpec((cfg.ctx, HEAD_DIM), lambda b, h: (b, h)),
        out_shape=jax.ShapeDtypeStruct((cfg.nc, nh * HEAD_DIM), BF16),
        compiler_params=_params(("arbitrary", "arbitrary")),
        name="mla_attention_ctx",
    )(q, k, v)
    return y_lat, y_ctx


def _merge_kernel(ya_ref, yb_ref, yc_ref, ga_ref, gb_ref, gc_ref, wa_ref, wb_ref, wc_ref, o_ref):
    m = (_sigmoid(ga_ref[...]) * _dot(ya_ref[...], wa_ref[...])
         + _sigmoid(gb_ref[...]) * _dot(yb_ref[...], wb_ref[...])
         + _sigmoid(gc_ref[...]) * _dot(yc_ref[...], wc_ref[...]))
    o_ref[...] = m.astype(o_ref.dtype)


def merge_branches(cfg, ya, yb, yc, p, wa, wb, wc, *, rows):
    tm = _pick(cfg.nc, (512, 256))
    tn = _pick(cfg.d, (512, 256))
    gcols = cfg.d // tn

    def gate_spec(k):
        return pl.BlockSpec((tm, tn), lambda i, j, k=k: (i, k * gcols + j))

    return pl.pallas_call(
        _merge_kernel,
        grid=(rows // tm, cfg.d // tn),
        in_specs=[
            pl.BlockSpec((tm, ya.shape[1]), lambda i, j: (i, 0)),
            pl.BlockSpec((tm, yb.shape[1]), lambda i, j: (i, 0)),
            pl.BlockSpec((tm, yc.shape[1]), lambda i, j: (i, 0)),
            gate_spec(0), gate_spec(1), gate_spec(2),
            pl.BlockSpec((wa.shape[0], tn), lambda i, j: (0, j)),
            pl.BlockSpec((wb.shape[0], tn), lambda i, j: (0, j)),
            pl.BlockSpec((wc.shape[0], tn), lambda i, j: (0, j)),
        ],
        out_specs=pl.BlockSpec((tm, tn), lambda i, j: (i, j)),
        out_shape=jax.ShapeDtypeStruct((rows, cfg.d), BF16),
        compiler_params=_params(("arbitrary", "arbitrary")),
        name="merge_branches",
    )(ya, yb, yc, p, p, p, wa, wb, wc)


def _matmul_resid_kernel(a_ref, w_ref, x_ref, mod_ref, o_ref, *, k):
    o_ref[...] = x_ref[...] + mod_ref[k:k + 1, :] * _dot(a_ref[...], w_ref[...])


def matmul_residual(cfg, a, w, x, mod_l, *, k, rows):
    tm = _pick(cfg.nc, (1024, 512, 256))
    tn = _pick(cfg.d, (512, 256))
    row_of = _mod_row_map(cfg, tm)
    return pl.pallas_call(
        functools.partial(_matmul_resid_kernel, k=k),
        grid=(rows // tm, cfg.d // tn),
        in_specs=[
            pl.BlockSpec((tm, a.shape[1]), lambda i, j: (i, 0)),
            pl.BlockSpec((a.shape[1], tn), lambda i, j: (0, j)),
            pl.BlockSpec((tm, tn), lambda i, j: (i, j)),
            pl.BlockSpec((None, 6, tn), lambda i, j: (row_of(i), 0, j)),
        ],
        out_specs=pl.BlockSpec((tm, tn), lambda i, j: (i, j)),
        out_shape=jax.ShapeDtypeStruct((rows, cfg.d), F32),
        compiler_params=_params(("arbitrary", "arbitrary")),
        name="matmul_residual",
    )(a, w, x, mod_l)


def _top16_pass(s_ref, top_ref, rank_ref, break_ties):
    s = s_ref[...]
    n = s.shape[0]
    row = lax.broadcasted_iota(jnp.int32, s.shape, 0).astype(F32)
    rank = jnp.full(s.shape, UNRANKED, F32)
    work = s
    for r in range(PEER_TOPK):
        m = jnp.max(work, axis=0, keepdims=True)
        sel = work == m
        if break_ties:
            sel = row == jnp.min(jnp.where(sel, row, float(n)), axis=0, keepdims=True)
        rank = jnp.where(sel, float(r), rank)
        work = jnp.where(sel, -jnp.inf, work)
        top_ref[r:r + 1, :] = m
    rank_ref[...] = rank


def _peer_tables_kernel(qh_ref, keys_ref, b_ref, e2_ref, c_ref, e1_ref, cnt_ref, *scratch, tile):
    lanes = HEAD_DIM
    n_arr = 2 * (tile // lanes)
    s_refs, top_refs, rank_refs = (scratch[i * n_arr:(i + 1) * n_arr] for i in range(3))
    k1 = keys_ref[0]
    k2 = keys_ref[1]
    ranked = jnp.zeros((1, lanes), F32)
    for part in range(tile // lanes):
        qh = qh_ref[part * lanes:(part + 1) * lanes, :].astype(BF16)
        s_refs[2 * part][...] = _dot_nt(k1, qh[:, :lanes])
        s_refs[2 * part + 1][...] = _dot_nt(k2, qh[:, lanes:])
        for idx in (2 * part, 2 * part + 1):
            _top16_pass(s_refs[idx], top_refs[idx], rank_refs[idx], break_ties=False)
            n_ranked = jnp.sum(jnp.where(rank_refs[idx][...] < UNRANKED, 1.0, 0.0), axis=0, keepdims=True)
            ranked = jnp.maximum(ranked, n_ranked)

    @pl.when(jnp.max(ranked) > float(PEER_TOPK))
    def _():
        for idx in range(n_arr):
            _top16_pass(s_refs[idx], top_refs[idx], rank_refs[idx], break_ties=True)

    for part in range(tile // lanes):
        cols = slice(part * lanes, (part + 1) * lanes)
        s1 = s_refs[2 * part][...]
        s2 = s_refs[2 * part + 1][...]
        rank1 = rank_refs[2 * part][...]
        rank2 = rank_refs[2 * part + 1][...]
        t1 = top_refs[2 * part][...]
        t2 = top_refs[2 * part + 1][...]
        pieces = [t1[0:1] + t2]
        pos = [lax.broadcasted_iota(jnp.int32, (PEER_TOPK, lanes), 0).astype(F32)]
        for a in range(1, 8):
            pieces.append(t1[a:a + 1] + t2[0:8])
            pos.append(lax.broadcasted_iota(jnp.int32, (8, lanes), 0).astype(F32) + float(a * PEER_TOPK))
        pieces.append(t1[8:16] + t2[0:1])
        pos.append((lax.broadcasted_iota(jnp.int32, (8, lanes), 0).astype(F32) + 8.0) * float(PEER_TOPK))
        cand = jnp.concatenate(pieces, axis=0)
        cpos = jnp.concatenate(pos, axis=0)
        a_row = lax.broadcasted_iota(jnp.int32, (PEER_TOPK, lanes), 0).astype(F32)
        cnt = jnp.zeros((PEER_TOPK, lanes), F32)
        z = jnp.zeros((1, lanes), F32)
        best0 = None
        for r in range(PEER_TOPK):
            m = jnp.max(cand, axis=0, keepdims=True)
            first = jnp.min(jnp.where(cand == m, cpos, 1e9), axis=0, keepdims=True)
            cand = jnp.where(cpos == first, -jnp.inf, cand)
            cnt = cnt + jnp.where(a_row == jnp.floor(first * (1.0 / PEER_TOPK)), 1.0, 0.0)
            if r == 0:
                best0 = m
            z = z + jnp.exp(m - best0)
        cnt_ref[...] = cnt
        c_tab = jnp.zeros((PEER_NKEYS, lanes), F32)
        for a in range(PEER_TOPK):
            c_tab = jnp.where(rank1 == float(a), cnt_ref[a:a + 1, :], c_tab)
        b_ref[:, cols] = rank2.astype(b_ref.dtype)
        c_ref[:, cols] = c_tab
        e1_ref[:, cols] = jnp.exp(s1 - t1[0:1]) / z
        e2_ref[:, cols] = jnp.exp(s2 - t2[0:1]).astype(e2_ref.dtype)


def peer_tables(cfg, qh, keys, *, rows):
    tile = 256
    n_arr = 2 * (tile // HEAD_DIM)
    nh = cfg.peer_heads
    shapes = [jax.ShapeDtypeStruct((nh, PEER_NKEYS, rows), dt) for dt in (BF16, BF16, F32, F32)]
    out_spec = pl.BlockSpec((None, PEER_NKEYS, tile), lambda i, h: (h, 0, i))
    return pl.pallas_call(
        functools.partial(_peer_tables_kernel, tile=tile),
        grid=(rows // tile, nh),
        in_specs=[
            pl.BlockSpec((tile, 2 * HEAD_DIM), lambda i, h: (i, h)),
            pl.BlockSpec((2, None, PEER_NKEYS, HEAD_DIM), lambda i, h: (0, h, 0, 0)),
        ],
        out_specs=[out_spec] * 4,
        out_shape=shapes,
        scratch_shapes=([pltpu.VMEM((PEER_TOPK, HEAD_DIM), F32)]
                        + [pltpu.VMEM((PEER_NKEYS, HEAD_DIM), F32) for _ in range(n_arr)]
                        + [pltpu.VMEM((PEER_TOPK, HEAD_DIM), F32) for _ in range(n_arr)]
                        + [pltpu.VMEM((PEER_NKEYS, HEAD_DIM), F32) for _ in range(n_arr)]),
        compiler_params=_params(("arbitrary", "arbitrary")),
        name="peer_tables",
    )(qh, keys)


PEER_ROWS = 8
PEER_EB = PEER_ROWS * PEER_NKEYS
PEER_DOT_ROWS = 1


def _peer_dense_kernel(ht_ref, u_ref, vt_ref, b_ref, e2_ref, c_ref, e1_ref, x_ref, mod_ref, o_ref,
                       acc_ref, w_ref, *, heads):
    e = pl.program_id(1)

    @pl.when(e == 0)
    def _():
        acc_ref[...] = jnp.zeros_like(acc_ref)

    zero = jnp.zeros((), BF16)
    ht = ht_ref[...]
    for r0 in range(0, PEER_ROWS, PEER_DOT_ROWS):
        blk = slice(r0 * PEER_NKEYS, (r0 + PEER_DOT_ROWS) * PEER_NKEYS)
        act = _dot(u_ref[blk, :], ht)
        gelu = (0.5 * act * (1.0 + lax.erf(act * float(math.sqrt(0.5))))).astype(BF16)
        for q in range(PEER_DOT_ROWS):
            r = r0 + q
            g = jnp.zeros((PEER_NKEYS, act.shape[1]), BF16)
            for h in range(heads):
                crow = jnp.broadcast_to(c_ref[h, r:r + 1, :], g.shape).astype(BF16)
                erow = jnp.broadcast_to(e1_ref[h, r:r + 1, :], g.shape).astype(BF16)
                g = g + jnp.where(b_ref[h] < crow, e2_ref[h] * erow, zero)
            w_ref[r * PEER_NKEYS:(r + 1) * PEER_NKEYS, :] = g * gelu[q * PEER_NKEYS:(q + 1) * PEER_NKEYS]
    acc_ref[...] += _dot(vt_ref[...], w_ref[...])

    @pl.when(e == pl.num_programs(1) - 1)
    def _():
        o_ref[...] = x_ref[...] + mod_ref[5:6, :] * acc_ref[...].T


def peer_dense(cfg, ht, u, vt, tabs, x, mod_l, *, layer, rows):
    tile = _pick(cfg.nc, (512, 256))
    eb = PEER_EB
    n_exp = u.shape[1]
    nh = cfg.peer_heads
    row_of = _mod_row_map(cfg, tile)
    tab_spec = pl.BlockSpec((nh, PEER_NKEYS, tile), lambda i, e: (0, 0, i))
    row_spec = pl.BlockSpec((nh, PEER_ROWS, tile), lambda i, e: (0, e, i))
    return pl.pallas_call(
        functools.partial(_peer_dense_kernel, heads=nh),
        grid=(rows // tile, n_exp // eb),
        in_specs=[
            pl.BlockSpec((cfg.d, tile), lambda i, e: (0, i)),
            pl.BlockSpec((None, eb, cfg.d), lambda i, e: (layer, e, 0)),
            pl.BlockSpec((None, cfg.d, eb), lambda i, e: (layer, 0, e)),
            tab_spec, tab_spec, row_spec, row_spec,
            pl.BlockSpec((tile, cfg.d), lambda i, e: (i, 0)),
            pl.BlockSpec((None, 6, cfg.d), lambda i, e: (row_of(i), 0, 0)),
        ],
        out_specs=pl.BlockSpec((tile, cfg.d), lambda i, e: (i, 0)),
        out_shape=jax.ShapeDtypeStruct((rows, cfg.d), F32),
        scratch_shapes=[pltpu.VMEM((cfg.d, tile), F32), pltpu.VMEM((eb, tile), BF16)],
        compiler_params=_params(("arbitrary", "arbitrary")),
        name="peer_dense",
    )(ht, u, vt, *tabs, x, mod_l)


def _final_norm_kernel(x_ref, w_ref, o_ref):
    x = x_ref[...]
    o_ref[...] = x * lax.rsqrt(jnp.mean(x * x, axis=-1, keepdims=True) + RMS_EPS) * w_ref[...]


def final_norm(cfg, x, w):
    tm = _pick(cfg.nl, (1024, 512, 256))
    return pl.pallas_call(
        _final_norm_kernel,
        grid=(cfg.nl // tm,),
        in_specs=[pl.BlockSpec((tm, cfg.d), lambda i: (i, 0)), pl.BlockSpec((1, cfg.d), lambda i: (0, 0))],
        out_specs=pl.BlockSpec((tm, cfg.d), lambda i: (i, 0)),
        out_shape=jax.ShapeDtypeStruct((cfg.nl, cfg.d), F32),
        compiler_params=_params(("arbitrary",)),
        name="final_norm",
    )(x, w.reshape(1, cfg.d))


def _rot_cols(w):
    q = ROPE_DIM // 4
    return jnp.concatenate([-w[..., q:2 * q], w[..., 0:q], -w[..., 3 * q:4 * q], w[..., 2 * q:3 * q]], axis=-1)


def _in_proj_weight(cfg, w_in, n_cols):
    d = cfg.d
    hg_end = 5 * cfg.hgw
    pool_end = hg_end + POOL_WIDTH
    cq_end = pool_end + cfg.q_lora
    ckv_end = cq_end + cfg.kv_lora
    rope_end = ckv_end + ROPE_DIM
    w_in = w_in.astype(BF16)
    k_rope = w_in[..., ckv_end:rope_end]
    parts = [w_in[..., rope_end:rope_end + 3 * d], w_in[..., :ckv_end], k_rope, _rot_cols(k_rope)]
    parts.append(jnp.zeros(w_in.shape[:-1] + (n_cols - cfg.in_cols,), BF16))
    return jnp.concatenate(parts, axis=-1)


def _mla_q_weight(cfg, w_uq):
    w = w_uq.reshape(cfg.q_lora, cfg.mla_heads, HEAD_DIM + ROPE_DIM)
    rope = w[..., HEAD_DIM:]
    w = jnp.concatenate([w[..., :HEAD_DIM], rope, _rot_cols(rope)], axis=-1)
    return jnp.transpose(w, (1, 0, 2)).astype(BF16)


def _mla_kv_weight(cfg, w_ukv):
    w = w_ukv.reshape(cfg.kv_lora, cfg.mla_heads, 2 * HEAD_DIM)
    return jnp.transpose(w, (1, 0, 2)).astype(BF16)


def _rope_table(cfg):
    rows = cfg.seq // cfg.grid_w
    r, col = jnp.meshgrid(jnp.arange(rows), jnp.arange(cfg.grid_w), indexing="ij")
    n_freq = ROPE_DIM // 4
    freqs = ROPE_THETA ** (-jnp.arange(n_freq, dtype=F32) / n_freq)
    ang_r = r.reshape(-1)[:, None] * freqs
    ang_c = col.reshape(-1)[:, None] * freqs
    cos = jnp.concatenate([jnp.cos(ang_r)] * 2 + [jnp.cos(ang_c)] * 2, axis=1)
    sin = jnp.concatenate([jnp.sin(ang_r)] * 2 + [jnp.sin(ang_c)] * 2, axis=1)
    lat = jnp.tile(jnp.concatenate([cos, sin], axis=1).astype(F32), (cfg.batch, 1))
    ctx = jnp.concatenate([jnp.ones((cfg.nc, ROPE_DIM), F32), jnp.zeros((cfg.nc, ROPE_DIM), F32)], axis=1)
    return jnp.concatenate([lat, ctx], axis=0)


def _forward(cfg, x, c, ctx, c_ctx, w_mod, b_mod, norm_mix, norm_ffn, w_in, hg_lb_logits, hg_norm,
             pool_w, pool_scale, mla_q_norm, mla_w_uq, mla_kv_norm, mla_w_ukv,
             w_branch_a, w_branch_b, w_branch_c, w_out, peer_wq, peer_keys, peer_u, peer_v, final_w):
    d = cfg.d
    assert cfg.seq % HG_CHUNK == 0 and cfg.ctx % HG_CHUNK == 0 and cfg.nl % cfg.ctx == 0
    assert cfg.off_pool % POOL_WIDTH == 0 and cfg.off_cq % cfg.q_lora == 0
    assert cfg.off_ckv % cfg.kv_lora == 0 and cfg.batch < MOD_ROWS
    tn_in = 768 if d % 256 == 0 and cfg.in_cols > 8192 else 256
    n_cols = -(-cfg.in_cols // tn_in) * tn_in
    tm = _pick(cfg.nc, (1024, 512, 256))

    xs = jnp.concatenate([x.reshape(cfg.nl, d), ctx.reshape(cfg.nc, d)], axis=0)
    c_all = jnp.concatenate([c, c_ctx[None], jnp.zeros((MOD_ROWS - cfg.batch - 1, d), F32)], axis=0)
    mod = adaln_tables(cfg, c_all, w_mod, b_mod)
    cs = _rope_table(cfg)
    w_in_all = _in_proj_weight(cfg, w_in, n_cols)
    wq_all = peer_wq.astype(BF16)
    u_all = peer_u.astype(BF16)
    vt_all = jnp.swapaxes(peer_v.astype(BF16), 1, 2)

    for l in range(cfg.depth):
        last = l == cfg.depth - 1
        rows = cfg.nl if last else cfg.nt
        mod_l = mod[l]
        p = norm_matmul(cfg, xs, norm_mix[l], mod_l, w_in_all, layer=l,
                        k0=0, rows=cfg.nt, tm=tm, tn=tn_in, emit_h=False)
        ya, hg_ctx = hgrn_mixer(cfg, p, hg_lb_logits, hg_norm[l], l, rows)
        pw = pool_w[l].astype(BF16)
        yb = pool_mixer(cfg, p, pw, pool_scale[l], n_seq=cfg.batch, seq_len=cfg.seq, row0=0, out_rows=rows)
        q = mla_queries(cfg, p, mla_q_norm[l], _mla_q_weight(cfg, mla_w_uq[l]), cs)
        k, v = mla_keys_values(cfg, p, mla_kv_norm[l], _mla_kv_weight(cfg, mla_w_ukv[l]), cs)
        yc, att_ctx = mla_attention(cfg, q, k, v, with_ctx=not last, out_rows=rows)
        if not last:
            pool_ctx = pool_mixer(cfg, p, pw, pool_scale[l], n_seq=cfg.batch, seq_len=cfg.ctx, row0=cfg.nl)
            ya = lax.dynamic_update_slice(ya, hg_ctx, (cfg.nl, 0))
            yb = lax.dynamic_update_slice(yb, pool_ctx, (cfg.nl, 0))
            yc = lax.dynamic_update_slice(yc, att_ctx, (cfg.nl, 0))
        m = merge_branches(cfg, ya, yb, yc, p, w_branch_a[l].astype(BF16), w_branch_b[l].astype(BF16),
                           w_branch_c[l].astype(BF16), rows=rows)
        xs = matmul_residual(cfg, m, w_out[l].astype(BF16), xs, mod_l, k=2, rows=rows)
        qh, h2t = norm_matmul(cfg, xs, norm_ffn[l], mod_l, wq_all, layer=l,
                             k0=3, rows=rows, tm=tm, tn=_pick(peer_wq.shape[2], (512, 256)), emit_h=True)
        tabs = peer_tables(cfg, qh, peer_keys[l].astype(BF16), rows=rows)
        xs = peer_dense(cfg, h2t, u_all, vt_all, tabs, xs, mod_l, layer=l, rows=rows)

    return final_norm(cfg, xs, final_w).reshape(cfg.batch, cfg.seq, d)


def kernel(x, c, ctx, c_ctx, w_mod, b_mod, norm_mix, norm_ffn, w_in, hg_lb_logits, hg_norm, pool_w, pool_scale,
           mla_q_norm, mla_w_uq, mla_kv_norm, mla_w_ukv, w_branch_a, w_branch_b, w_branch_c, w_out,
           peer_wq, peer_keys, peer_u, peer_v, final_norm):
    batch, seq, d = x.shape
    cfg = Cfg(d=d, batch=batch, seq=seq, ctx=ctx.shape[1], grid_w=64, depth=w_mod.shape[0],
              hg_heads=hg_lb_logits.shape[2] // HEAD_DIM,
              mla_heads=mla_w_ukv.shape[2] // (2 * HEAD_DIM), q_lora=mla_q_norm.shape[1],
              kv_lora=mla_kv_norm.shape[1], peer_heads=peer_keys.shape[2])
    return _forward(cfg, x, c, ctx, c_ctx, w_mod, b_mod, norm_mix, norm_ffn, w_in, hg_lb_logits, hg_norm,
                    pool_w, pool_scale, mla_q_norm, mla_w_uq, mla_kv_norm, mla_w_ukv,
                    w_branch_a, w_branch_b, w_branch_c, w_out, peer_wq, peer_keys, peer_u, peer_v, final_norm)
```

```python
import functools
import math
from typing import NamedTuple

import numpy as np
import jax
import jax.numpy as jnp
from jax import lax
from jax.experimental import pallas as pl
from jax.experimental.pallas import tpu as pltpu

F32 = jnp.float32
BF16 = jnp.bfloat16

RMS_EPS = 1e-6
ROPE_THETA = 10000.0
HEAD_DIM = 128
ROPE_DIM = 64
QK_DIM = 256
LOG2_E = float(math.log2(math.e))
QK_LOG2_SCALE = float((HEAD_DIM + ROPE_DIM) ** -0.5) * LOG2_E
POOL_WINDOWS = (2, 4, 8, 16)
POOL_GROUP = 256
POOL_WIDTH = POOL_GROUP * len(POOL_WINDOWS)
POOL_HALO = 8
PEER_NKEYS = 128
PEER_TOPK = 16
HG_CHUNK = 128
HG_LEVELS = 7
UNRANKED = 99.0
MOD_ROWS = 8
VMEM_LIMIT = 56 * 1024 * 1024


class Cfg(NamedTuple):
    d: int
    batch: int
    seq: int
    ctx: int
    grid_w: int
    depth: int
    hg_heads: int
    mla_heads: int
    q_lora: int
    kv_lora: int
    peer_heads: int

    @property
    def nl(self):
        return self.batch * self.seq

    @property
    def nc(self):
        return self.batch * self.ctx

    @property
    def nt(self):
        return self.nl + self.nc

    @property
    def hgw(self):
        return self.hg_heads * HEAD_DIM

    @property
    def off_gate(self):
        return 0

    @property
    def off_hg(self):
        return 3 * self.d

    @property
    def off_pool(self):
        return self.off_hg + 5 * self.hgw

    @property
    def off_cq(self):
        return self.off_pool + POOL_WIDTH

    @property
    def off_ckv(self):
        return self.off_cq + self.q_lora

    @property
    def off_slab(self):
        return self.off_ckv + self.kv_lora

    @property
    def in_cols(self):
        return self.off_slab + HEAD_DIM


def _params(sem):
    return pltpu.CompilerParams(dimension_semantics=sem, vmem_limit_bytes=VMEM_LIMIT)


def _dot(a, b):
    return jnp.dot(a, b, preferred_element_type=F32)


def _dot_nt(a, b):
    return lax.dot_general(a, b, (((1,), (1,)), ((), ())), preferred_element_type=F32)


def _sigmoid(x):
    return 1.0 / (1.0 + jnp.exp(-x))


def _pick(n, prefs):
    for p in prefs:
        if n % p == 0:
            return p
    raise ValueError(f"no tile for {n} in {prefs}")


def _mod_row_map(cfg, tm):
    n_lat = cfg.nl // tm
    per_batch = cfg.seq // tm
    return lambda i: jnp.where(i < n_lat, i // per_batch, cfg.batch)


def _mod_kernel(c_ref, w_ref, b_ref, o_ref):
    c = c_ref[...]
    s = (c * _sigmoid(c)).astype(BF16)
    o_ref[...] = _dot(s, w_ref[...].astype(BF16)) + b_ref[...]


def adaln_tables(cfg, c_all, w_mod, b_mod):
    d6 = 6 * cfg.d
    tn = _pick(d6, (1024, 768, 512, 256))
    out = pl.pallas_call(
        _mod_kernel,
        grid=(cfg.depth, d6 // tn),
        in_specs=[
            pl.BlockSpec((MOD_ROWS, cfg.d), lambda l, j: (0, 0)),
            pl.BlockSpec((None, cfg.d, tn), lambda l, j: (l, 0, j)),
            pl.BlockSpec((None, 1, tn), lambda l, j: (l, 0, j)),
        ],
        out_specs=pl.BlockSpec((None, MOD_ROWS, tn), lambda l, j: (l, 0, j)),
        out_shape=jax.ShapeDtypeStruct((cfg.depth, MOD_ROWS, d6), F32),
        compiler_params=_params(("arbitrary", "arbitrary")),
        name="adaln_tables",
    )(c_all, w_mod, b_mod.reshape(cfg.depth, 1, d6))
    return out.reshape(cfg.depth, MOD_ROWS, 6, cfg.d)


def _norm_matmul_kernel(x_ref, gain_ref, mod_ref, w_ref, *rest, k0, emit_h):
    if emit_h:
        o_ref, hout_ref, h_ref = rest
    else:
        o_ref, h_ref = rest

    @pl.when(pl.program_id(1) == 0)
    def _():
        x = x_ref[...]
        y = x * lax.rsqrt(jnp.mean(x * x, axis=-1, keepdims=True) + RMS_EPS) * gain_ref[...]
        h = (y * (1.0 + mod_ref[k0 + 1:k0 + 2, :]) + mod_ref[k0:k0 + 1, :]).astype(BF16)
        h_ref[...] = h
        if emit_h:
            hout_ref[...] = h.T

    o_ref[...] = _dot(h_ref[...], w_ref[...])


def norm_matmul(cfg, x, gain, mod_l, w, *, layer, k0, rows, tm, tn, emit_h):
    n = w.shape[2]
    row_of = _mod_row_map(cfg, tm)
    out_shape = [jax.ShapeDtypeStruct((rows, n), F32)]
    out_specs = [pl.BlockSpec((tm, tn), lambda i, j: (i, j))]
    if emit_h:
        out_shape.append(jax.ShapeDtypeStruct((cfg.d, rows), BF16))
        out_specs.append(pl.BlockSpec((cfg.d, tm), lambda i, j: (0, i)))
    res = pl.pallas_call(
        functools.partial(_norm_matmul_kernel, k0=k0, emit_h=emit_h),
        grid=(rows // tm, n // tn),
        in_specs=[
            pl.BlockSpec((tm, cfg.d), lambda i, j: (i, 0)),
            pl.BlockSpec((1, cfg.d), lambda i, j: (0, 0)),
            pl.BlockSpec((None, 6, cfg.d), lambda i, j: (row_of(i), 0, 0)),
            pl.BlockSpec((None, cfg.d, tn), lambda i, j: (layer, 0, j)),
        ],
        out_specs=out_specs,
        out_shape=out_shape,
        scratch_shapes=[pltpu.VMEM((tm, cfg.d), BF16)],
        compiler_params=_params(("arbitrary", "arbitrary")),
        name="norm_matmul_h" if emit_h else "norm_matmul",
    )(x, gain.reshape(1, cfg.d), mod_l, w)
    return res if emit_h else res[0]


def _hgrn_consts():
    c = HG_CHUNK
    out = []
    for rev in (False, True):
        p = np.arange(c) if not rev else c - 1 - np.arange(c)
        pt, pu = p[:, None], p[None, :]
        g = np.zeros((HG_LEVELS + 1, c, c), np.float32)
        up = np.zeros((HG_LEVELS, c, c), np.float32)
        g[0] = pu <= pt
        for l in range(HG_LEVELS):
            m = 1 << l
            blk = p >> (l + 1)
            upper = ((p >> l) & 1) == 1
            mid = (blk * 2 * m + m)[:, None]
            same = blk[:, None] == blk[None, :]
            g_up = same & (pu >= mid) & (pu <= pt)
            g_lo = same & (pu > pt) & (pu < mid)
            g[1 + l] = np.where(upper[:, None], g_up, g_lo)
            up[l] = np.broadcast_to(upper[:, None], (c, c))
        x = pt ^ pu
        lv = np.where(pu < pt, np.floor(np.log2(np.maximum(x, 1))), -1.0).astype(np.float32)
        out.append((jnp.asarray(g.reshape(-1, c), BF16), jnp.asarray(up, F32), jnp.asarray(lv, F32)))
    return out


def _hgrn_chunk(q_raw, v, f_raw, log_lb, log_1mlb, one_m_lb, g_ref, up_ref, lv_ref, st_ref, end_row):
    c = HG_CHUNK
    q = q_raw * _sigmoid(q_raw)
    e = jnp.exp(-jnp.abs(f_raw))
    one_pe = 1.0 + e
    log_sig = jnp.minimum(f_raw, 0.0) - jnp.log(one_pe)
    t = log_1mlb + log_sig
    log_f = jnp.maximum(log_lb, t) + jnp.log(1.0 + jnp.exp(-jnp.abs(log_lb - t)))
    k = one_m_lb * jnp.where(f_raw >= 0.0, e, 1.0) / one_pe
    hi = log_f.astype(BF16)
    lo = (log_f - hi.astype(F32)).astype(BF16)
    a2 = _dot(g_ref[...], jnp.concatenate([hi, lo], axis=1))
    a = a2[:, :HEAD_DIM] + a2[:, HEAD_DIM:]
    b = a[0:c]
    lv = lv_ref[...]
    scores = jnp.zeros((c, c), F32)
    for l in range(HG_LEVELS):
        e_l = jnp.exp(a[(1 + l) * c:(2 + l) * c])
        x = (jnp.where(up_ref[l] > 0.5, q, k) * e_l).astype(BF16)
        scores = jnp.where(lv == float(l), _dot_nt(x, x), scores)
    b_end = b[end_row:end_row + 1, :]
    qb = (q * jnp.exp(b)).astype(BF16)
    kd = (k * jnp.exp(b_end - b)).astype(BF16)
    st = st_ref[...]
    vb = v.astype(BF16)
    o = (_dot(scores.astype(BF16), vb) + _dot_nt(qb, st.astype(BF16))
         + jnp.sum(q * k, axis=-1, keepdims=True) * v)
    st_ref[...] = st * jnp.exp(b_end) + _dot(v.T.astype(BF16), kd)
    return o


def _hgrn_kernel(ql, ffl, fbl, il, gl, qc, ffc, fbc, ic, gc, logit_ref, nw_ref,
                 gf_ref, upf_ref, lvf_ref, gb_ref, upb_ref, lvb_ref,
                 yl_ref, yc_ref, ofl, obl, ofc, obc, stf, stb, *, layer, seq, ctx):
    c = HG_CHUNK
    depth = logit_ref.shape[0]
    lg = [logit_ref[dd] for dd in range(depth)]
    mx = functools.reduce(jnp.maximum, lg)
    ex = [jnp.exp(v - mx) for v in lg]
    tot = functools.reduce(jnp.add, ex)
    cum = [ex[0] / tot]
    for dd in range(1, layer + 1):
        cum.append(cum[-1] + ex[dd] / tot)
    lb = cum[layer] - cum[0]
    log_lb = jnp.log(lb)
    log_1mlb = jnp.log1p(-lb)
    one_m_lb = 1.0 - lb

    stf[...] = jnp.zeros_like(stf)
    stb[...] = jnp.zeros_like(stb)

    def segment(q_ref, ff_ref, fb_ref, i_ref, of_ref, ob_ref, n):
        def body(j, carry):
            rf = pl.multiple_of(j * c, c)
            rb = pl.multiple_of((n - 1 - j) * c, c)
            of_ref[pl.ds(rf, c), :] = _hgrn_chunk(
                q_ref[pl.ds(rf, c), :], i_ref[pl.ds(rf, c), :], ff_ref[pl.ds(rf, c), :],
                log_lb[0:1], log_1mlb[0:1], one_m_lb[0:1], gf_ref, upf_ref, lvf_ref, stf, c - 1)
            ob_ref[pl.ds(rb, c), :] = _hgrn_chunk(
                q_ref[pl.ds(rb, c), :], i_ref[pl.ds(rb, c), :], fb_ref[pl.ds(rb, c), :],
                log_lb[1:2], log_1mlb[1:2], one_m_lb[1:2], gb_ref, upb_ref, lvb_ref, stb, 0)
            return carry
        lax.fori_loop(0, n, body, 0, unroll=4 if n % 4 == 0 else 2)

    segment(qc, ffc, fbc, ic, ofc, obc, ctx // c)
    segment(ql, ffl, fbl, il, ofl, obl, seq // c)

    nw = nw_ref[...]

    def readout(of_ref, ob_ref, g_ref, y_ref, n):
        def body(j, carry):
            r = pl.multiple_of(j * c, c)
            o = of_ref[pl.ds(r, c), :] + ob_ref[pl.ds(r, c), :]
            y = o * lax.rsqrt(jnp.mean(o * o, axis=-1, keepdims=True) + RMS_EPS) * nw
            g = g_ref[pl.ds(r, c), :]
            y_ref[pl.ds(r, c), :] = (y * (g * _sigmoid(g))).astype(y_ref.dtype)
            return carry
        lax.fori_loop(0, n, body, 0)

    readout(ofc, obc, gc, yc_ref, ctx // c)
    readout(ofl, obl, gl, yl_ref, seq // c)


def hgrn_mixer(cfg, p, logits, norm_w, layer, out_rows):
    hd = HEAD_DIM
    nh = cfg.hg_heads
    col0 = cfg.off_hg // hd
    ctx_blk0 = cfg.nl // cfg.ctx
    (gf, upf, lvf), (gb, upb, lvb) = _hgrn_consts()

    def lat_spec(part):
        return pl.BlockSpec((cfg.seq, hd), lambda b, h, part=part: (b, col0 + part * nh + h))

    def ctx_spec(part):
        return pl.BlockSpec((cfg.ctx, hd), lambda b, h, part=part: (ctx_blk0 + b, col0 + part * nh + h))

    parts = (0, 1, 2, 3, 4)

    def const(arr):
        return pl.BlockSpec(arr.shape, lambda b, h, nd=arr.ndim: (0,) * nd)

    y_lat, y_ctx = pl.pallas_call(
        functools.partial(_hgrn_kernel, layer=layer, seq=cfg.seq, ctx=cfg.ctx),
        grid=(cfg.batch, nh),
        in_specs=[lat_spec(k) for k in parts] + [ctx_spec(k) for k in parts] + [
            pl.BlockSpec((cfg.depth, 2, hd), lambda b, h: (0, 0, h)),
            pl.BlockSpec((1, hd), lambda b, h: (0, 0)),
            const(gf), const(upf), const(lvf), const(gb), const(upb), const(lvb),
        ],
        out_specs=[
            pl.BlockSpec((cfg.seq, hd), lambda b, h: (b, h)),
            pl.BlockSpec((cfg.ctx, hd), lambda b, h: (b, h)),
        ],
        out_shape=[
            jax.ShapeDtypeStruct((out_rows, cfg.hgw), BF16),
            jax.ShapeDtypeStruct((cfg.nc, cfg.hgw), BF16),
        ],
        scratch_shapes=[
            pltpu.VMEM((cfg.seq, hd), F32), pltpu.VMEM((cfg.seq, hd), F32),
            pltpu.VMEM((cfg.ctx, hd), F32), pltpu.VMEM((cfg.ctx, hd), F32),
            pltpu.VMEM((hd, hd), F32), pltpu.VMEM((hd, hd), F32),
        ],
        compiler_params=_params(("arbitrary", "arbitrary")),
        name="hgrn_mixer",
    )(*([p] * 10), logits, norm_w.reshape(1, hd), gf, upf, lvf, gb, upb, lvb)
    return y_lat, y_ctx


def _pool_kernel(prev_ref, cur_ref, next_ref, w_ref, scale_ref, o_ref, buf_ref, *, seq_len, tile, n_tiles):
    i = pl.program_id(1)
    h = POOL_HALO
    cur = cur_ref[...]
    buf_ref[0:h, :] = jnp.where(i > 0, prev_ref[...], 0.0)
    buf_ref[h:h + tile, :] = cur
    buf_ref[h + tile:2 * h + tile, :] = jnp.where(i < n_tiles - 1, next_ref[...], 0.0)
    pos = i * tile + lax.broadcasted_iota(jnp.int32, (tile, POOL_GROUP), 0)
    for gi, win in enumerate(POOL_WINDOWS):
        half = win // 2
        cols = slice(gi * POOL_GROUP, (gi + 1) * POOL_GROUP)
        acc = buf_ref[h - half:h - half + tile, cols]
        for dlt in range(-half + 1, half):
            acc = acc + buf_ref[h + dlt:h + dlt + tile, cols]
        cnt = (jnp.minimum(pos + half, seq_len) - jnp.maximum(pos - half, 0)).astype(F32)
        pooled = acc / cnt - cur[:, cols]
        mixed = _dot(pooled.astype(BF16), w_ref[gi])
        o_ref[:, cols] = (mixed * scale_ref[:, cols]).astype(o_ref.dtype)


def pool_mixer(cfg, p, w_pool, scale, *, n_seq, seq_len, row0, out_rows=None):
    out_rows = n_seq * seq_len if out_rows is None else out_rows
    tile = _pick(seq_len, (512, 256, 128))
    n_tiles = seq_len // tile
    blk0 = row0 // tile
    col = cfg.off_pool // POOL_WIDTH
    per8 = tile // POOL_HALO
    last8 = cfg.nt // POOL_HALO - 1

    def cur_map(s, i):
        return (blk0 + s * n_tiles + i, col)

    def prev_map(s, i):
        return (jnp.maximum((blk0 + s * n_tiles + i) * per8 - 1, 0), col)

    def next_map(s, i):
        return (jnp.minimum((blk0 + s * n_tiles + i + 1) * per8, last8), col)

    return pl.pallas_call(
        functools.partial(_pool_kernel, seq_len=seq_len, tile=tile, n_tiles=n_tiles),
        grid=(n_seq, n_tiles),
        in_specs=[
            pl.BlockSpec((POOL_HALO, POOL_WIDTH), prev_map),
            pl.BlockSpec((tile, POOL_WIDTH), cur_map),
            pl.BlockSpec((POOL_HALO, POOL_WIDTH), next_map),
            pl.BlockSpec((len(POOL_WINDOWS), POOL_GROUP, POOL_GROUP), lambda s, i: (0, 0, 0)),
            pl.BlockSpec((1, POOL_WIDTH), lambda s, i: (0, 0)),
        ],
        out_specs=pl.BlockSpec((tile, POOL_WIDTH), lambda s, i: (s * n_tiles + i, 0)),
        out_shape=jax.ShapeDtypeStruct((out_rows, POOL_WIDTH), BF16),
        scratch_shapes=[pltpu.VMEM((tile + 2 * POOL_HALO, POOL_WIDTH), F32)],
        compiler_params=_params(("arbitrary", "arbitrary")),
        name="pool_mixer",
    )(p, p, p, w_pool, scale.reshape(1, POOL_WIDTH))


def _rope_slab(slab, cs):
    t = slab * cs
    r = t + pltpu.roll(t, ROPE_DIM, axis=1)
    lane = lax.broadcasted_iota(jnp.int32, r.shape, 1)
    return jnp.where(lane < ROPE_DIM, r, 0.0)


def _rms_bf16(x, w):
    return (x * lax.rsqrt(jnp.mean(x * x, axis=-1, keepdims=True) + RMS_EPS) * w).astype(BF16)


def _mla_q_kernel(cq_ref, nw_ref, cs_ref, w_ref, o_ref):
    n = _rms_bf16(cq_ref[...], nw_ref[...])
    cs = cs_ref[...]
    for h in range(w_ref.shape[0]):
        y = _dot(n, w_ref[h])
        rope = _rope_slab(y[:, HEAD_DIM:], cs)
        o_ref[h] = (jnp.concatenate([y[:, :HEAD_DIM], rope], axis=1) * QK_LOG2_SCALE).astype(o_ref.dtype)


def mla_queries(cfg, p, norm_w, w_q, cs):
    tm = _pick(cfg.nc, (512, 256))
    nh = cfg.mla_heads
    col = cfg.off_cq // cfg.q_lora
    return pl.pallas_call(
        _mla_q_kernel,
        grid=(cfg.nt // tm,),
        in_specs=[
            pl.BlockSpec((tm, cfg.q_lora), lambda i: (i, col)),
            pl.BlockSpec((1, cfg.q_lora), lambda i: (0, 0)),
            pl.BlockSpec((tm, HEAD_DIM), lambda i: (i, 0)),
            pl.BlockSpec((nh, cfg.q_lora, QK_DIM), lambda i: (0, 0, 0)),
        ],
        out_specs=pl.BlockSpec((nh, tm, QK_DIM), lambda i: (0, i, 0)),
        out_shape=jax.ShapeDtypeStruct((nh, cfg.nt, QK_DIM), BF16),
        compiler_params=_params(("arbitrary",)),
        name="mla_queries",
    )(p, norm_w.reshape(1, cfg.q_lora), cs, w_q)


def _mla_kv_kernel(ckv_ref, slab_ref, nw_ref, cs_ref, w_ref, k_ref, v_ref):
    n = _rms_bf16(ckv_ref[...], nw_ref[...])
    k_rope = _rope_slab(slab_ref[...], cs_ref[...])
    lane = lax.broadcasted_iota(jnp.int32, k_rope.shape, 1)
    ones_col = jnp.where(lane == 0, 1.0, 0.0)
    for h in range(w_ref.shape[0]):
        y = _dot(n, w_ref[h])
        k_ref[h] = jnp.concatenate([y[:, :HEAD_DIM], k_rope], axis=1).astype(k_ref.dtype)
        v_ref[h] = jnp.concatenate([y[:, HEAD_DIM:], ones_col], axis=1).astype(v_ref.dtype)


def mla_keys_values(cfg, p, norm_w, w_kv, cs):
    tm = _pick(cfg.nc, (512, 256))
    nh = cfg.mla_heads
    col = cfg.off_ckv // cfg.kv_lora
    slab_col = cfg.off_slab // HEAD_DIM
    return pl.pallas_call(
        _mla_kv_kernel,
        grid=(cfg.nt // tm,),
        in_specs=[
            pl.BlockSpec((tm, cfg.kv_lora), lambda i: (i, col)),
            pl.BlockSpec((tm, HEAD_DIM), lambda i: (i, slab_col)),
            pl.BlockSpec((1, cfg.kv_lora), lambda i: (0, 0)),
            pl.BlockSpec((tm, HEAD_DIM), lambda i: (i, 0)),
            pl.BlockSpec((nh, cfg.kv_lora, 2 * HEAD_DIM), lambda i: (0, 0, 0)),
        ],
        out_specs=[
            pl.BlockSpec((nh, tm, QK_DIM), lambda i: (0, i, 0)),
            pl.BlockSpec((nh, tm, 2 * HEAD_DIM), lambda i: (0, i, 0)),
        ],
        out_shape=[
            jax.ShapeDtypeStruct((nh, cfg.nt, QK_DIM), BF16),
            jax.ShapeDtypeStruct((nh, cfg.nt, 2 * HEAD_DIM), BF16),
        ],
        compiler_params=_params(("arbitrary",)),
        name="mla_keys_values",
    )(p, p, norm_w.reshape(1, cfg.kv_lora), cs, w_kv)


ATTN_SUB = 256
ATTN_KT = 512
ATTN_SCORE_ROWS = 1024


def _softmax_pv(s_list, v_list):
    m = functools.reduce(jnp.maximum, [jnp.max(s, axis=-1, keepdims=True) for s in s_list])
    o = None
    for s, v in zip(s_list, v_list):
        part = _dot(jnp.exp2(s - m).astype(BF16), v)
        o = part if o is None else o + part
    return o[:, :HEAD_DIM] / o[:, HEAD_DIM:HEAD_DIM + 1]


def _attn_lat_kernel(q_ref, kl_ref, kc_ref, vl_ref, vc_ref, o_ref, s_ref):
    n_keys = kl_ref.shape[0]
    n_slots = s_ref.shape[0] // ATTN_SUB
    for r in range(0, q_ref.shape[0], ATTN_SUB):
        rows = slice(r, r + ATTN_SUB)
        slot = (r // ATTN_SUB) % n_slots
        srows = slice(slot * ATTN_SUB, (slot + 1) * ATTN_SUB)
        q = q_ref[rows, :]
        sc = _dot_nt(q, kc_ref[...])
        m = jnp.max(sc, axis=-1, keepdims=True)
        for c in range(0, n_keys, ATTN_KT):
            s = _dot_nt(q, kl_ref[c:c + ATTN_KT, :])
            s_ref[srows, c:c + ATTN_KT] = s
            m = jnp.maximum(m, jnp.max(s, axis=-1, keepdims=True))
        o = _dot(jnp.exp2(sc - m).astype(BF16), vc_ref[...])
        for c in range(0, n_keys, ATTN_KT):
            p = jnp.exp2((s_ref[srows, c:c + ATTN_KT] - m).astype(BF16))
            o = o + _dot(p, vl_ref[c:c + ATTN_KT, :])
        o_ref[rows, :] = (o[:, :HEAD_DIM] / o[:, HEAD_DIM:HEAD_DIM + 1]).astype(o_ref.dtype)


def _attn_ctx_kernel(q_ref, kc_ref, vc_ref, o_ref):
    o_ref[...] = _softmax_pv([_dot_nt(q_ref[...], kc_ref[...])], [vc_ref[...]]).astype(o_ref.dtype)


def mla_attention(cfg, q, k, v, *, with_ctx, out_rows):
    nh = cfg.mla_heads
    vw = 2 * HEAD_DIM
    tq = _pick(cfg.seq, (2048, 1024, 512, 256))
    nq = cfg.seq // tq
    s_rows = min(tq, ATTN_SCORE_ROWS)
    cb0 = cfg.nl // cfg.ctx
    y_lat = pl.pallas_call(
        _attn_lat_kernel,
        grid=(cfg.batch, nh, nq),
        in_specs=[
            pl.BlockSpec((None, tq, QK_DIM), lambda b, h, i: (h, b * nq + i, 0)),
            pl.BlockSpec((None, cfg.seq, QK_DIM), lambda b, h, i: (h, b, 0)),
            pl.BlockSpec((None, cfg.ctx, QK_DIM), lambda b, h, i: (h, cb0 + b, 0)),
            pl.BlockSpec((None, cfg.seq, vw), lambda b, h, i: (h, b, 0)),
            pl.BlockSpec((None, cfg.ctx, vw), lambda b, h, i: (h, cb0 + b, 0)),
        ],
        out_specs=pl.BlockSpec((tq, HEAD_DIM), lambda b, h, i: (b * nq + i, h)),
        out_shape=jax.ShapeDtypeStruct((out_rows, nh * HEAD_DIM), BF16),
        scratch_shapes=[pltpu.VMEM((s_rows, cfg.seq), F32)],
        compiler_params=_params(("arbitrary", "arbitrary", "arbitrary")),
        name="mla_attention",
    )(q, k, k, v, v)
    if not with_ctx:
        return y_lat, None
    y_ctx = pl.pallas_call(
        _attn_ctx_kernel,
        grid=(cfg.batch, nh),
        in_specs=[
            pl.BlockSpec((None, cfg.ctx, QK_DIM), lambda b, h: (h, cb0 + b, 0)),
            pl.BlockSpec((None, cfg.ctx, QK_DIM), lambda b, h: (h, cb0 + b, 0)),
            pl.BlockSpec((None, cfg.ctx, vw), lambda b, h: (h, cb0 + b, 0)),
        ],
        out_specs=pl.BlockSpec((cfg.ctx, HEAD_DIM), lambda b, h: (b, h)),
        out_shape=jax.ShapeDtypeStruct((cfg.nc, nh * HEAD_DIM), BF16),
        compiler_params=_params(("arbitrary", "arbitrary")),
        name="mla_attention_ctx",
    )(q, k, v)
    return y_lat, y_ctx


def _merge_kernel(ya_ref, yb_ref, yc_ref, ga_ref, gb_ref, gc_ref, wa_ref, wb_ref, wc_ref, o_ref):
    m = (_sigmoid(ga_ref[...]) * _dot(ya_ref[...], wa_ref[...])
         + _sigmoid(gb_ref[...]) * _dot(yb_ref[...], wb_ref[...])
         + _sigmoid(gc_ref[...]) * _dot(yc_ref[...], wc_ref[...]))
    o_ref[...] = m.astype(o_ref.dtype)


def merge_branches(cfg, ya, yb, yc, p, wa, wb, wc, *, rows):
    tm = _pick(cfg.nc, (512, 256))
    tn = _pick(cfg.d, (512, 256))
    gcols = cfg.d // tn

    def gate_spec(k):
        return pl.BlockSpec((tm, tn), lambda i, j, k=k: (i, k * gcols + j))

    return pl.pallas_call(
        _merge_kernel,
        grid=(rows // tm, cfg.d // tn),
        in_specs=[
            pl.BlockSpec((tm, ya.shape[1]), lambda i, j: (i, 0)),
            pl.BlockSpec((tm, yb.shape[1]), lambda i, j: (i, 0)),
            pl.BlockSpec((tm, yc.shape[1]), lambda i, j: (i, 0)),
            gate_spec(0), gate_spec(1), gate_spec(2),
            pl.BlockSpec((wa.shape[0], tn), lambda i, j: (0, j)),
            pl.BlockSpec((wb.shape[0], tn), lambda i, j: (0, j)),
            pl.BlockSpec((wc.shape[0], tn), lambda i, j: (0, j)),
        ],
        out_specs=pl.BlockSpec((tm, tn), lambda i, j: (i, j)),
        out_shape=jax.ShapeDtypeStruct((rows, cfg.d), BF16),
        compiler_params=_params(("arbitrary", "arbitrary")),
        name="merge_branches",
    )(ya, yb, yc, p, p, p, wa, wb, wc)


def _matmul_resid_kernel(a_ref, w_ref, x_ref, mod_ref, o_ref, *, k):
    o_ref[...] = x_ref[...] + mod_ref[k:k + 1, :] * _dot(a_ref[...], w_ref[...])


def matmul_residual(cfg, a, w, x, mod_l, *, k, rows):
    tm = _pick(cfg.nc, (1024, 512, 256))
    tn = _pick(cfg.d, (512, 256))
    row_of = _mod_row_map(cfg, tm)
    return pl.pallas_call(
        functools.partial(_matmul_resid_kernel, k=k),
        grid=(rows // tm, cfg.d // tn),
        in_specs=[
            pl.BlockSpec((tm, a.shape[1]), lambda i, j: (i, 0)),
            pl.BlockSpec((a.shape[1], tn), lambda i, j: (0, j)),
            pl.BlockSpec((tm, tn), lambda i, j: (i, j)),
            pl.BlockSpec((None, 6, tn), lambda i, j: (row_of(i), 0, j)),
        ],
        out_specs=pl.BlockSpec((tm, tn), lambda i, j: (i, j)),
        out_shape=jax.ShapeDtypeStruct((rows, cfg.d), F32),
        compiler_params=_params(("arbitrary", "arbitrary")),
        name="matmul_residual",
    )(a, w, x, mod_l)


def _top16_pass(s_ref, top_ref, rank_ref, break_ties):
    s = s_ref[...]
    n = s.shape[0]
    row = lax.broadcasted_iota(jnp.int32, s.shape, 0).astype(F32)
    rank = jnp.full(s.shape, UNRANKED, F32)
    work = s
    for r in range(PEER_TOPK):
        m = jnp.max(work, axis=0, keepdims=True)
        sel = work == m
        if break_ties:
            sel = row == jnp.min(jnp.where(sel, row, float(n)), axis=0, keepdims=True)
        rank = jnp.where(sel, float(r), rank)
        work = jnp.where(sel, -jnp.inf, work)
        top_ref[r:r + 1, :] = m
    rank_ref[...] = rank


def _peer_tables_kernel(qh_ref, keys_ref, b_ref, e2_ref, c_ref, e1_ref, cnt_ref, *scratch, tile):
    lanes = HEAD_DIM
    n_arr = 2 * (tile // lanes)
    s_refs, top_refs, rank_refs = (scratch[i * n_arr:(i + 1) * n_arr] for i in range(3))
    k1 = keys_ref[0]
    k2 = keys_ref[1]
    ranked = jnp.zeros((1, lanes), F32)
    for part in range(tile // lanes):
        qh = qh_ref[part * lanes:(part + 1) * lanes, :].astype(BF16)
        s_refs[2 * part][...] = _dot_nt(k1, qh[:, :lanes])
        s_refs[2 * part + 1][...] = _dot_nt(k2, qh[:, lanes:])
        for idx in (2 * part, 2 * part + 1):
            _top16_pass(s_refs[idx], top_refs[idx], rank_refs[idx], break_ties=False)
            n_ranked = jnp.sum(jnp.where(rank_refs[idx][...] < UNRANKED, 1.0, 0.0), axis=0, keepdims=True)
            ranked = jnp.maximum(ranked, n_ranked)

    @pl.when(jnp.max(ranked) > float(PEER_TOPK))
    def _():
        for idx in range(n_arr):
            _top16_pass(s_refs[idx], top_refs[idx], rank_refs[idx], break_ties=True)

    for part in range(tile // lanes):
        cols = slice(part * lanes, (part + 1) * lanes)
        s1 = s_refs[2 * part][...]
        s2 = s_refs[2 * part + 1][...]
        rank1 = rank_refs[2 * part][...]
        rank2 = rank_refs[2 * part + 1][...]
        t1 = top_refs[2 * part][...]
        t2 = top_refs[2 * part + 1][...]
        pieces = [t1[0:1] + t2]
        pos = [lax.broadcasted_iota(jnp.int32, (PEER_TOPK, lanes), 0).astype(F32)]
        for a in range(1, 8):
            pieces.append(t1[a:a + 1] + t2[0:8])
            pos.append(lax.broadcasted_iota(jnp.int32, (8, lanes), 0).astype(F32) + float(a * PEER_TOPK))
        pieces.append(t1[8:16] + t2[0:1])
        pos.append((lax.broadcasted_iota(jnp.int32, (8, lanes), 0).astype(F32) + 8.0) * float(PEER_TOPK))
        cand = jnp.concatenate(pieces, axis=0)
        cpos = jnp.concatenate(pos, axis=0)
        a_row = lax.broadcasted_iota(jnp.int32, (PEER_TOPK, lanes), 0).astype(F32)
        cnt = jnp.zeros((PEER_TOPK, lanes), F32)
        z = jnp.zeros((1, lanes), F32)
        best0 = None
        for r in range(PEER_TOPK):
            m = jnp.max(cand, axis=0, keepdims=True)
            first = jnp.min(jnp.where(cand == m, cpos, 1e9), axis=0, keepdims=True)
            cand = jnp.where(cpos == first, -jnp.inf, cand)
            cnt = cnt + jnp.where(a_row == jnp.floor(first * (1.0 / PEER_TOPK)), 1.0, 0.0)
            if r == 0:
                best0 = m
            z = z + jnp.exp(m - best0)
        cnt_ref[...] = cnt
        c_tab = jnp.zeros((PEER_NKEYS, lanes), F32)
        for a in range(PEER_TOPK):
            c_tab = jnp.where(rank1 == float(a), cnt_ref[a:a + 1, :], c_tab)
        b_ref[:, cols] = rank2.astype(b_ref.dtype)
        c_ref[:, cols] = c_tab
        e1_ref[:, cols] = jnp.exp(s1 - t1[0:1]) / z
        e2_ref[:, cols] = jnp.exp(s2 - t2[0:1]).astype(e2_ref.dtype)


def peer_tables(cfg, qh, keys, *, rows):
    tile = _pick(cfg.nc, (512, 256))
    n_arr = 2 * (tile // HEAD_DIM)
    nh = cfg.peer_heads
    shapes = [jax.ShapeDtypeStruct((nh, PEER_NKEYS, rows), dt) for dt in (BF16, BF16, F32, F32)]
    out_spec = pl.BlockSpec((None, PEER_NKEYS, tile), lambda i, h: (h, 0, i))
    return pl.pallas_call(
        functools.partial(_peer_tables_kernel, tile=tile),
        grid=(rows // tile, nh),
        in_specs=[
            pl.BlockSpec((tile, 2 * HEAD_DIM), lambda i, h: (i, h)),
            pl.BlockSpec((2, None, PEER_NKEYS, HEAD_DIM), lambda i, h: (0, h, 0, 0)),
        ],
        out_specs=[out_spec] * 4,
        out_shape=shapes,
        scratch_shapes=([pltpu.VMEM((PEER_TOPK, HEAD_DIM), F32)]
                        + [pltpu.VMEM((PEER_NKEYS, HEAD_DIM), F32) for _ in range(n_arr)]
                        + [pltpu.VMEM((PEER_TOPK, HEAD_DIM), F32) for _ in range(n_arr)]
                        + [pltpu.VMEM((PEER_NKEYS, HEAD_DIM), F32) for _ in range(n_arr)]),
        compiler_params=_params(("arbitrary", "arbitrary")),
        name="peer_tables",
    )(qh, keys)


PEER_ROWS = 8
PEER_EB = PEER_ROWS * PEER_NKEYS
PEER_DOT_ROWS = 1


def _peer_dense_kernel(ht_ref, u_ref, vt_ref, b_ref, e2_ref, c_ref, e1_ref, x_ref, mod_ref, o_ref,
                       acc_ref, w_ref, *, heads):
    e = pl.program_id(1)

    @pl.when(e == 0)
    def _():
        acc_ref[...] = jnp.zeros_like(acc_ref)

    zero = jnp.zeros((), BF16)
    ht = ht_ref[...]
    for r0 in range(0, PEER_ROWS, PEER_DOT_ROWS):
        blk = slice(r0 * PEER_NKEYS, (r0 + PEER_DOT_ROWS) * PEER_NKEYS)
        act = _dot(u_ref[blk, :], ht)
        gelu = (0.5 * act * (1.0 + lax.erf(act * float(math.sqrt(0.5))))).astype(BF16)
        for q in range(PEER_DOT_ROWS):
            r = r0 + q
            g = jnp.zeros((PEER_NKEYS, act.shape[1]), BF16)
            for h in range(heads):
                crow = jnp.broadcast_to(c_ref[h, r:r + 1, :], g.shape).astype(BF16)
                erow = jnp.broadcast_to(e1_ref[h, r:r + 1, :], g.shape).astype(BF16)
                g = g + jnp.where(b_ref[h] < crow, e2_ref[h] * erow, zero)
            w_ref[r * PEER_NKEYS:(r + 1) * PEER_NKEYS, :] = g * gelu[q * PEER_NKEYS:(q + 1) * PEER_NKEYS]
    acc_ref[...] += _dot(vt_ref[...], w_ref[...])

    @pl.when(e == pl.num_programs(1) - 1)
    def _():
        o_ref[...] = x_ref[...] + mod_ref[5:6, :] * acc_ref[...].T


def peer_dense(cfg, ht, u, vt, tabs, x, mod_l, *, layer, rows):
    tile = _pick(cfg.nc, (512, 256))
    eb = PEER_EB
    n_exp = u.shape[1]
    nh = cfg.peer_heads
    row_of = _mod_row_map(cfg, tile)
    tab_spec = pl.BlockSpec((nh, PEER_NKEYS, tile), lambda i, e: (0, 0, i))
    row_spec = pl.BlockSpec((nh, PEER_ROWS, tile), lambda i, e: (0, e, i))
    return pl.pallas_call(
        functools.partial(_peer_dense_kernel, heads=nh),
        grid=(rows // tile, n_exp // eb),
        in_specs=[
            pl.BlockSpec((cfg.d, tile), lambda i, e: (0, i)),
            pl.BlockSpec((None, eb, cfg.d), lambda i, e: (layer, e, 0)),
            pl.BlockSpec((None, cfg.d, eb), lambda i, e: (layer, 0, e)),
            tab_spec, tab_spec, row_spec, row_spec,
            pl.BlockSpec((tile, cfg.d), lambda i, e: (i, 0)),
            pl.BlockSpec((None, 6, cfg.d), lambda i, e: (row_of(i), 0, 0)),
        ],
        out_specs=pl.BlockSpec((tile, cfg.d), lambda i, e: (i, 0)),
        out_shape=jax.ShapeDtypeStruct((rows, cfg.d), F32),
        scratch_shapes=[pltpu.VMEM((cfg.d, tile), F32), pltpu.VMEM((eb, tile), BF16)],
        compiler_params=_params(("arbitrary", "arbitrary")),
        name="peer_dense",
    )(ht, u, vt, *tabs, x, mod_l)


def _final_norm_kernel(x_ref, w_ref, o_ref):
    x = x_ref[...]
    o_ref[...] = x * lax.rsqrt(jnp.mean(x * x, axis=-1, keepdims=True) + RMS_EPS) * w_ref[...]


def final_norm(cfg, x, w):
    tm = _pick(cfg.nl, (1024, 512, 256))
    return pl.pallas_call(
        _final_norm_kernel,
        grid=(cfg.nl // tm,),
        in_specs=[pl.BlockSpec((tm, cfg.d), lambda i: (i, 0)), pl.BlockSpec((1, cfg.d), lambda i: (0, 0))],
        out_specs=pl.BlockSpec((tm, cfg.d), lambda i: (i, 0)),
        out_shape=jax.ShapeDtypeStruct((cfg.nl, cfg.d), F32),
        compiler_params=_params(("arbitrary",)),
        name="final_norm",
    )(x, w.reshape(1, cfg.d))


def _rot_cols(w):
    q = ROPE_DIM // 4
    return jnp.concatenate([-w[..., q:2 * q], w[..., 0:q], -w[..., 3 * q:4 * q], w[..., 2 * q:3 * q]], axis=-1)


def _in_proj_weight(cfg, w_in, n_cols):
    d = cfg.d
    hg_end = 5 * cfg.hgw
    pool_end = hg_end + POOL_WIDTH
    cq_end = pool_end + cfg.q_lora
    ckv_end = cq_end + cfg.kv_lora
    rope_end = ckv_end + ROPE_DIM
    w_in = w_in.astype(BF16)
    k_rope = w_in[..., ckv_end:rope_end]
    parts = [w_in[..., rope_end:rope_end + 3 * d], w_in[..., :ckv_end], k_rope, _rot_cols(k_rope)]
    parts.append(jnp.zeros(w_in.shape[:-1] + (n_cols - cfg.in_cols,), BF16))
    return jnp.concatenate(parts, axis=-1)


def _mla_q_weight(cfg, w_uq):
    w = w_uq.reshape(cfg.q_lora, cfg.mla_heads, HEAD_DIM + ROPE_DIM)
    rope = w[..., HEAD_DIM:]
    w = jnp.concatenate([w[..., :HEAD_DIM], rope, _rot_cols(rope)], axis=-1)
    return jnp.transpose(w, (1, 0, 2)).astype(BF16)


def _mla_kv_weight(cfg, w_ukv):
    w = w_ukv.reshape(cfg.kv_lora, cfg.mla_heads, 2 * HEAD_DIM)
    return jnp.transpose(w, (1, 0, 2)).astype(BF16)


def _rope_table(cfg):
    rows = cfg.seq // cfg.grid_w
    r, col = jnp.meshgrid(jnp.arange(rows), jnp.arange(cfg.grid_w), indexing="ij")
    n_freq = ROPE_DIM // 4
    freqs = ROPE_THETA ** (-jnp.arange(n_freq, dtype=F32) / n_freq)
    ang_r = r.reshape(-1)[:, None] * freqs
    ang_c = col.reshape(-1)[:, None] * freqs
    cos = jnp.concatenate([jnp.cos(ang_r)] * 2 + [jnp.cos(ang_c)] * 2, axis=1)
    sin = jnp.concatenate([jnp.sin(ang_r)] * 2 + [jnp.sin(ang_c)] * 2, axis=1)
    lat = jnp.tile(jnp.concatenate([cos, sin], axis=1).astype(F32), (cfg.batch, 1))
    ctx = jnp.concatenate([jnp.ones((cfg.nc, ROPE_DIM), F32), jnp.zeros((cfg.nc, ROPE_DIM), F32)], axis=1)
    return jnp.concatenate([lat, ctx], axis=0)


def _forward(cfg, x, c, ctx, c_ctx, w_mod, b_mod, norm_mix, norm_ffn, w_in, hg_lb_logits, hg_norm,
             pool_w, pool_scale, mla_q_norm, mla_w_uq, mla_kv_norm, mla_w_ukv,
             w_branch_a, w_branch_b, w_branch_c, w_out, peer_wq, peer_keys, peer_u, peer_v, final_w):
    d = cfg.d
    assert cfg.seq % HG_CHUNK == 0 and cfg.ctx % HG_CHUNK == 0 and cfg.nl % cfg.ctx == 0
    assert cfg.off_pool % POOL_WIDTH == 0 and cfg.off_cq % cfg.q_lora == 0
    assert cfg.off_ckv % cfg.kv_lora == 0 and cfg.batch < MOD_ROWS
    tn_in = 1536 if d % 256 == 0 and cfg.in_cols > 8192 else 256
    n_cols = -(-cfg.in_cols // tn_in) * tn_in
    tm = _pick(cfg.nc, (1024, 512, 256))

    xs = jnp.concatenate([x.reshape(cfg.nl, d), ctx.reshape(cfg.nc, d)], axis=0)
    c_all = jnp.concatenate([c, c_ctx[None], jnp.zeros((MOD_ROWS - cfg.batch - 1, d), F32)], axis=0)
    mod = adaln_tables(cfg, c_all, w_mod, b_mod)
    cs = _rope_table(cfg)
    w_in_all = _in_proj_weight(cfg, w_in, n_cols)
    wq_all = peer_wq.astype(BF16)
    u_all = peer_u.astype(BF16)
    vt_all = jnp.swapaxes(peer_v.astype(BF16), 1, 2)

    for l in range(cfg.depth):
        last = l == cfg.depth - 1
        rows = cfg.nl if last else cfg.nt
        mod_l = mod[l]
        p = norm_matmul(cfg, xs, norm_mix[l], mod_l, w_in_all, layer=l,
                        k0=0, rows=cfg.nt, tm=tm, tn=tn_in, emit_h=False)
        ya, hg_ctx = hgrn_mixer(cfg, p, hg_lb_logits, hg_norm[l], l, rows)
        pw = pool_w[l].astype(BF16)
        yb = pool_mixer(cfg, p, pw, pool_scale[l], n_seq=cfg.batch, seq_len=cfg.seq, row0=0, out_rows=rows)
        q = mla_queries(cfg, p, mla_q_norm[l], _mla_q_weight(cfg, mla_w_uq[l]), cs)
        k, v = mla_keys_values(cfg, p, mla_kv_norm[l], _mla_kv_weight(cfg, mla_w_ukv[l]), cs)
        yc, att_ctx = mla_attention(cfg, q, k, v, with_ctx=not last, out_rows=rows)
        if not last:
            pool_ctx = pool_mixer(cfg, p, pw, pool_scale[l], n_seq=cfg.batch, seq_len=cfg.ctx, row0=cfg.nl)
            ya = lax.dynamic_update_slice(ya, hg_ctx, (cfg.nl, 0))
            yb = lax.dynamic_update_slice(yb, pool_ctx, (cfg.nl, 0))
            yc = lax.dynamic_update_slice(yc, att_ctx, (cfg.nl, 0))
        m = merge_branches(cfg, ya, yb, yc, p, w_branch_a[l].astype(BF16), w_branch_b[l].astype(BF16),
                           w_branch_c[l].astype(BF16), rows=rows)
        xs = matmul_residual(cfg, m, w_out[l].astype(BF16), xs, mod_l, k=2, rows=rows)
        qh, h2t = norm_matmul(cfg, xs, norm_ffn[l], mod_l, wq_all, layer=l,
                             k0=3, rows=rows, tm=tm, tn=_pick(peer_wq.shape[2], (512, 256)), emit_h=True)
        tabs = peer_tables(cfg, qh, peer_keys[l].astype(BF16), rows=rows)
        xs = peer_dense(cfg, h2t, u_all, vt_all, tabs, xs, mod_l, layer=l, rows=rows)

    return final_norm(cfg, xs, final_w).reshape(cfg.batch, cfg.seq, d)


def kernel(x, c, ctx, c_ctx, w_mod, b_mod, norm_mix, norm_ffn, w_in, hg_lb_logits, hg_norm, pool_w, pool_scale,
           mla_q_norm, mla_w_uq, mla_kv_norm, mla_w_ukv, w_branch_a, w_branch_b, w_branch_c, w_out,
           peer_wq, peer_keys, peer_u, peer_v, final_norm):
    batch, seq, d = x.shape
    cfg = Cfg(d=d, batch=batch, seq=seq, ctx=ctx.shape[1], grid_w=64, depth=w_mod.shape[0],
              hg_heads=hg_lb_logits.shape[2] // HEAD_DIM,
              mla_heads=mla_w_ukv.shape[2] // (2 * HEAD_DIM), q_lora=mla_q_norm.shape[1],
              kv_lora=mla_kv_norm.shape[1], peer_heads=peer_keys.shape[2])
    return _forward(cfg, x, c, ctx, c_ctx, w_mod, b_mod, norm_mix, norm_ffn, w_in, hg_lb_logits, hg_norm,
                    pool_w, pool_scale, mla_q_norm, mla_w_uq, mla_kv_norm, mla_w_ukv,
                    w_branch_a, w_branch_b, w_branch_c, w_out, peer_wq, peer_keys, peer_u, peer_v, final_norm)
```

```python
import functools
import math
from typing import NamedTuple

import numpy as np
import jax
import jax.numpy as jnp
from jax import lax
from jax.experimental import pallas as pl
from jax.experimental.pallas import tpu as pltpu

F32 = jnp.float32
BF16 = jnp.bfloat16

RMS_EPS = 1e-6
ROPE_THETA = 10000.0
HEAD_DIM = 128
ROPE_DIM = 64
QK_DIM = 256
LOG2_E = float(math.log2(math.e))
QK_LOG2_SCALE = float((HEAD_DIM + ROPE_DIM) ** -0.5) * LOG2_E
POOL_WINDOWS = (2, 4, 8, 16)
POOL_GROUP = 256
POOL_WIDTH = POOL_GROUP * len(POOL_WINDOWS)
POOL_HALO = 8
PEER_NKEYS = 128
PEER_TOPK = 16
HG_CHUNK = 128
HG_LEVELS = 7
UNRANKED = 99.0
MOD_ROWS = 8
VMEM_LIMIT = 56 * 1024 * 1024


class Cfg(NamedTuple):
    d: int
    batch: int
    seq: int
    ctx: int
    grid_w: int
    depth: int
    hg_heads: int
    mla_heads: int
    q_lora: int
    kv_lora: int
    peer_heads: int

    @property
    def nl(self):
        return self.batch * self.seq

    @property
    def nc(self):
        return self.batch * self.ctx

    @property
    def nt(self):
        return self.nl + self.nc

    @property
    def hgw(self):
        return self.hg_heads * HEAD_DIM

    @property
    def off_gate(self):
        return 0

    @property
    def off_hg(self):
        return 3 * self.d

    @property
    def off_pool(self):
        return self.off_hg + 5 * self.hgw

    @property
    def off_cq(self):
        return self.off_pool + POOL_WIDTH

    @property
    def off_ckv(self):
        return self.off_cq + self.q_lora

    @property
    def off_slab(self):
        return self.off_ckv + self.kv_lora

    @property
    def in_cols(self):
        return self.off_slab + HEAD_DIM


def _params(sem):
    return pltpu.CompilerParams(dimension_semantics=sem, vmem_limit_bytes=VMEM_LIMIT)


def _dot(a, b):
    return jnp.dot(a, b, preferred_element_type=F32)


def _dot_nt(a, b):
    return lax.dot_general(a, b, (((1,), (1,)), ((), ())), preferred_element_type=F32)


def _sigmoid(x):
    return 1.0 / (1.0 + jnp.exp(-x))


def _pick(n, prefs):
    for p in prefs:
        if n % p == 0:
            return p
    raise ValueError(f"no tile for {n} in {prefs}")


def _mod_row_map(cfg, tm):
    n_lat = cfg.nl // tm
    per_batch = cfg.seq // tm
    return lambda i: jnp.where(i < n_lat, i // per_batch, cfg.batch)


def _mod_kernel(c_ref, w_ref, b_ref, o_ref):
    @pl.when(pl.program_id(1) == 0)
    def _():
        o_ref[...] = jnp.broadcast_to(b_ref[...], o_ref.shape)

    c = c_ref[...]
    s = (c * _sigmoid(c)).astype(BF16)
    o_ref[...] += _dot(s, w_ref[...].astype(BF16))


def adaln_tables(cfg, c_all, w_mod, b_mod):
    d6 = 6 * cfg.d
    tk = _pick(cfg.d, (256, 128))
    out = pl.pallas_call(
        _mod_kernel,
        grid=(cfg.depth, cfg.d // tk),
        in_specs=[
            pl.BlockSpec((MOD_ROWS, tk), lambda l, k: (0, k)),
            pl.BlockSpec((None, tk, d6), lambda l, k: (l, k, 0)),
            pl.BlockSpec((None, 1, d6), lambda l, k: (l, 0, 0)),
        ],
        out_specs=pl.BlockSpec((None, MOD_ROWS, d6), lambda l, k: (l, 0, 0)),
        out_shape=jax.ShapeDtypeStruct((cfg.depth, MOD_ROWS, d6), F32),
        compiler_params=_params(("arbitrary", "arbitrary")),
        name="adaln_tables",
    )(c_all, w_mod, b_mod.reshape(cfg.depth, 1, d6))
    return out.reshape(cfg.depth, MOD_ROWS, 6, cfg.d)


def _norm_matmul_kernel(x_ref, gain_ref, mod_ref, w_ref, *rest, k0, emit_h):
    if emit_h:
        o_ref, hout_ref, h_ref = rest
    else:
        o_ref, h_ref = rest

    @pl.when(pl.program_id(1) == 0)
    def _():
        x = x_ref[...]
        y = x * lax.rsqrt(jnp.mean(x * x, axis=-1, keepdims=True) + RMS_EPS) * gain_ref[...]
        h = (y * (1.0 + mod_ref[k0 + 1:k0 + 2, :]) + mod_ref[k0:k0 + 1, :]).astype(BF16)
        h_ref[...] = h
        if emit_h:
            hout_ref[...] = h.T

    o_ref[...] = _dot(h_ref[...], w_ref[...])


def norm_matmul(cfg, x, gain, mod_l, w, *, layer, k0, rows, tm, tn, emit_h):
    n = w.shape[2]
    row_of = _mod_row_map(cfg, tm)
    out_shape = [jax.ShapeDtypeStruct((rows, n), F32)]
    out_specs = [pl.BlockSpec((tm, tn), lambda i, j: (i, j))]
    if emit_h:
        out_shape.append(jax.ShapeDtypeStruct((cfg.d, rows), BF16))
        out_specs.append(pl.BlockSpec((cfg.d, tm), lambda i, j: (0, i)))
    res = pl.pallas_call(
        functools.partial(_norm_matmul_kernel, k0=k0, emit_h=emit_h),
        grid=(rows // tm, n // tn),
        in_specs=[
            pl.BlockSpec((tm, cfg.d), lambda i, j: (i, 0)),
            pl.BlockSpec((1, cfg.d), lambda i, j: (0, 0)),
            pl.BlockSpec((None, 6, cfg.d), lambda i, j: (row_of(i), 0, 0)),
            pl.BlockSpec((None, cfg.d, tn), lambda i, j: (layer, 0, j)),
        ],
        out_specs=out_specs,
        out_shape=out_shape,
        scratch_shapes=[pltpu.VMEM((tm, cfg.d), BF16)],
        compiler_params=_params(("arbitrary", "arbitrary")),
        name="norm_matmul_h" if emit_h else "norm_matmul",
    )(x, gain.reshape(1, cfg.d), mod_l, w)
    return res if emit_h else res[0]


def _hgrn_consts():
    c = HG_CHUNK
    out = []
    for rev in (False, True):
        p = np.arange(c) if not rev else c - 1 - np.arange(c)
        pt, pu = p[:, None], p[None, :]
        g = np.zeros((HG_LEVELS + 1, c, c), np.float32)
        up = np.zeros((HG_LEVELS, c, c), np.float32)
        g[0] = pu <= pt
        for l in range(HG_LEVELS):
            m = 1 << l
            blk = p >> (l + 1)
            upper = ((p >> l) & 1) == 1
            mid = (blk * 2 * m + m)[:, None]
            same = blk[:, None] == blk[None, :]
            g_up = same & (pu >= mid) & (pu <= pt)
            g_lo = same & (pu > pt) & (pu < mid)
            g[1 + l] = np.where(upper[:, None], g_up, g_lo)
            up[l] = np.broadcast_to(upper[:, None], (c, c))
        x = pt ^ pu
        lv = np.where(pu < pt, np.floor(np.log2(np.maximum(x, 1))), -1.0).astype(np.float32)
        out.append((jnp.asarray(g.reshape(-1, c), BF16), jnp.asarray(up, F32), jnp.asarray(lv, F32)))
    return out


def _hgrn_chunk(q_raw, v, f_raw, log_lb, log_1mlb, one_m_lb, g_ref, up_ref, lv_ref, st_ref, end_row):
    c = HG_CHUNK
    q = q_raw * _sigmoid(q_raw)
    e = jnp.exp(-jnp.abs(f_raw))
    one_pe = 1.0 + e
    log_sig = jnp.minimum(f_raw, 0.0) - jnp.log(one_pe)
    t = log_1mlb + log_sig
    log_f = jnp.maximum(log_lb, t) + jnp.log(1.0 + jnp.exp(-jnp.abs(log_lb - t)))
    k = one_m_lb * jnp.where(f_raw >= 0.0, e, 1.0) / one_pe
    hi = log_f.astype(BF16)
    lo = (log_f - hi.astype(F32)).astype(BF16)
    a2 = _dot(g_ref[...], jnp.concatenate([hi, lo], axis=1))
    a = a2[:, :HEAD_DIM] + a2[:, HEAD_DIM:]
    b = a[0:c]
    lv = lv_ref[...]
    scores = jnp.zeros((c, c), F32)
    for l in range(HG_LEVELS):
        e_l = jnp.exp(a[(1 + l) * c:(2 + l) * c])
        x = (jnp.where(up_ref[l] > 0.5, q, k) * e_l).astype(BF16)
        scores = jnp.where(lv == float(l), _dot_nt(x, x), scores)
    b_end = b[end_row:end_row + 1, :]
    qb = (q * jnp.exp(b)).astype(BF16)
    kd = (k * jnp.exp(b_end - b)).astype(BF16)
    st = st_ref[...]
    vb = v.astype(BF16)
    o = (_dot(scores.astype(BF16), vb) + _dot_nt(qb, st.astype(BF16))
         + jnp.sum(q * k, axis=-1, keepdims=True) * v)
    st_ref[...] = st * jnp.exp(b_end) + _dot(v.T.astype(BF16), kd)
    return o


def _hgrn_kernel(ql, ffl, fbl, il, gl, qc, ffc, fbc, ic, gc, logit_ref, nw_ref,
                 gf_ref, upf_ref, lvf_ref, gb_ref, upb_ref, lvb_ref,
                 yl_ref, yc_ref, ofl, obl, ofc, obc, stf, stb, *, layer, seq, ctx):
    c = HG_CHUNK
    depth = logit_ref.shape[0]
    lg = [logit_ref[dd] for dd in range(depth)]
    mx = functools.reduce(jnp.maximum, lg)
    ex = [jnp.exp(v - mx) for v in lg]
    tot = functools.reduce(jnp.add, ex)
    cum = [ex[0] / tot]
    for dd in range(1, layer + 1):
        cum.append(cum[-1] + ex[dd] / tot)
    lb = cum[layer] - cum[0]
    log_lb = jnp.log(lb)
    log_1mlb = jnp.log1p(-lb)
    one_m_lb = 1.0 - lb

    stf[...] = jnp.zeros_like(stf)
    stb[...] = jnp.zeros_like(stb)

    def segment(q_ref, ff_ref, fb_ref, i_ref, of_ref, ob_ref, n):
        def body(j, carry):
            rf = pl.multiple_of(j * c, c)
            rb = pl.multiple_of((n - 1 - j) * c, c)
            of_ref[pl.ds(rf, c), :] = _hgrn_chunk(
                q_ref[pl.ds(rf, c), :], i_ref[pl.ds(rf, c), :], ff_ref[pl.ds(rf, c), :],
                log_lb[0:1], log_1mlb[0:1], one_m_lb[0:1], gf_ref, upf_ref, lvf_ref, stf, c - 1)
            ob_ref[pl.ds(rb, c), :] = _hgrn_chunk(
                q_ref[pl.ds(rb, c), :], i_ref[pl.ds(rb, c), :], fb_ref[pl.ds(rb, c), :],
                log_lb[1:2], log_1mlb[1:2], one_m_lb[1:2], gb_ref, upb_ref, lvb_ref, stb, 0)
            return carry
        lax.fori_loop(0, n, body, 0, unroll=4 if n % 4 == 0 else 2)

    segment(qc, ffc, fbc, ic, ofc, obc, ctx // c)
    segment(ql, ffl, fbl, il, ofl, obl, seq // c)

    nw = nw_ref[...]

    def readout(of_ref, ob_ref, g_ref, y_ref, n):
        def body(j, carry):
            r = pl.multiple_of(j * c, c)
            o = of_ref[pl.ds(r, c), :] + ob_ref[pl.ds(r, c), :]
            y = o * lax.rsqrt(jnp.mean(o * o, axis=-1, keepdims=True) + RMS_EPS) * nw
            g = g_ref[pl.ds(r, c), :]
            y_ref[pl.ds(r, c), :] = (y * (g * _sigmoid(g))).astype(y_ref.dtype)
            return carry
        lax.fori_loop(0, n, body, 0)

    readout(ofc, obc, gc, yc_ref, ctx // c)
    readout(ofl, obl, gl, yl_ref, seq // c)


def hgrn_mixer(cfg, p, logits, norm_w, layer, out_rows):
    hd = HEAD_DIM
    nh = cfg.hg_heads
    col0 = cfg.off_hg // hd
    ctx_blk0 = cfg.nl // cfg.ctx
    (gf, upf, lvf), (gb, upb, lvb) = _hgrn_consts()

    def lat_spec(part):
        return pl.BlockSpec((cfg.seq, hd), lambda b, h, part=part: (b, col0 + part * nh + h))

    def ctx_spec(part):
        return pl.BlockSpec((cfg.ctx, hd), lambda b, h, part=part: (ctx_blk0 + b, col0 + part * nh + h))

    parts = (0, 1, 2, 3, 4)

    def const(arr):
        return pl.BlockSpec(arr.shape, lambda b, h, nd=arr.ndim: (0,) * nd)

    y_lat, y_ctx = pl.pallas_call(
        functools.partial(_hgrn_kernel, layer=layer, seq=cfg.seq, ctx=cfg.ctx),
        grid=(cfg.batch, nh),
        in_specs=[lat_spec(k) for k in parts] + [ctx_spec(k) for k in parts] + [
            pl.BlockSpec((cfg.depth, 2, hd), lambda b, h: (0, 0, h)),
            pl.BlockSpec((1, hd), lambda b, h: (0, 0)),
            const(gf), const(upf), const(lvf), const(gb), const(upb), const(lvb),
        ],
        out_specs=[
            pl.BlockSpec((cfg.seq, hd), lambda b, h: (b, h)),
            pl.BlockSpec((cfg.ctx, hd), lambda b, h: (b, h)),
        ],
        out_shape=[
            jax.ShapeDtypeStruct((out_rows, cfg.hgw), BF16),
            jax.ShapeDtypeStruct((cfg.nc, cfg.hgw), BF16),
        ],
        scratch_shapes=[
            pltpu.VMEM((cfg.seq, hd), F32), pltpu.VMEM((cfg.seq, hd), F32),
            pltpu.VMEM((cfg.ctx, hd), F32), pltpu.VMEM((cfg.ctx, hd), F32),
            pltpu.VMEM((hd, hd), F32), pltpu.VMEM((hd, hd), F32),
        ],
        compiler_params=_params(("arbitrary", "arbitrary")),
        name="hgrn_mixer",
    )(*([p] * 10), logits, norm_w.reshape(1, hd), gf, upf, lvf, gb, upb, lvb)
    return y_lat, y_ctx


def _pool_kernel(prev_ref, cur_ref, next_ref, w_ref, scale_ref, o_ref, buf_ref, *, seq_len, tile, n_tiles):
    i = pl.program_id(1)
    h = POOL_HALO
    cur = cur_ref[...]
    buf_ref[0:h, :] = jnp.where(i > 0, prev_ref[...], 0.0)
    buf_ref[h:h + tile, :] = cur
    buf_ref[h + tile:2 * h + tile, :] = jnp.where(i < n_tiles - 1, next_ref[...], 0.0)
    pos = i * tile + lax.broadcasted_iota(jnp.int32, (tile, POOL_GROUP), 0)
    for gi, win in enumerate(POOL_WINDOWS):
        half = win // 2
        cols = slice(gi * POOL_GROUP, (gi + 1) * POOL_GROUP)
        acc = buf_ref[h - half:h - half + tile, cols]
        for dlt in range(-half + 1, half):
            acc = acc + buf_ref[h + dlt:h + dlt + tile, cols]
        cnt = (jnp.minimum(pos + half, seq_len) - jnp.maximum(pos - half, 0)).astype(F32)
        pooled = acc / cnt - cur[:, cols]
        mixed = _dot(pooled.astype(BF16), w_ref[gi])
        o_ref[:, cols] = (mixed * scale_ref[:, cols]).astype(o_ref.dtype)


def pool_mixer(cfg, p, w_pool, scale, *, n_seq, seq_len, row0, out_rows=None):
    out_rows = n_seq * seq_len if out_rows is None else out_rows
    tile = _pick(seq_len, (512, 256, 128))
    n_tiles = seq_len // tile
    blk0 = row0 // tile
    col = cfg.off_pool // POOL_WIDTH
    per8 = tile // POOL_HALO
    last8 = cfg.nt // POOL_HALO - 1

    def cur_map(s, i):
        return (blk0 + s * n_tiles + i, col)

    def prev_map(s, i):
        return (jnp.maximum((blk0 + s * n_tiles + i) * per8 - 1, 0), col)

    def next_map(s, i):
        return (jnp.minimum((blk0 + s * n_tiles + i + 1) * per8, last8), col)

    return pl.pallas_call(
        functools.partial(_pool_kernel, seq_len=seq_len, tile=tile, n_tiles=n_tiles),
        grid=(n_seq, n_tiles),
        in_specs=[
            pl.BlockSpec((POOL_HALO, POOL_WIDTH), prev_map),
            pl.BlockSpec((tile, POOL_WIDTH), cur_map),
            pl.BlockSpec((POOL_HALO, POOL_WIDTH), next_map),
            pl.BlockSpec((len(POOL_WINDOWS), POOL_GROUP, POOL_GROUP), lambda s, i: (0, 0, 0)),
            pl.BlockSpec((1, POOL_WIDTH), lambda s, i: (0, 0)),
        ],
        out_specs=pl.BlockSpec((tile, POOL_WIDTH), lambda s, i: (s * n_tiles + i, 0)),
        out_shape=jax.ShapeDtypeStruct((out_rows, POOL_WIDTH), BF16),
        scratch_shapes=[pltpu.VMEM((tile + 2 * POOL_HALO, POOL_WIDTH), F32)],
        compiler_params=_params(("arbitrary", "arbitrary")),
        name="pool_mixer",
    )(p, p, p, w_pool, scale.reshape(1, POOL_WIDTH))


def _rope_slab(slab, cs):
    t = slab * cs
    r = t + pltpu.roll(t, ROPE_DIM, axis=1)
    lane = lax.broadcasted_iota(jnp.int32, r.shape, 1)
    return jnp.where(lane < ROPE_DIM, r, 0.0)


def _rms_bf16(x, w):
    return (x * lax.rsqrt(jnp.mean(x * x, axis=-1, keepdims=True) + RMS_EPS) * w).astype(BF16)


def _mla_q_kernel(cq_ref, nw_ref, cs_ref, w_ref, o_ref):
    n = _rms_bf16(cq_ref[...], nw_ref[...])
    cs = cs_ref[...]
    for h in range(w_ref.shape[0]):
        y = _dot(n, w_ref[h])
        rope = _rope_slab(y[:, HEAD_DIM:], cs)
        o_ref[h] = (jnp.concatenate([y[:, :HEAD_DIM], rope], axis=1) * QK_LOG2_SCALE).astype(o_ref.dtype)


def mla_queries(cfg, p, norm_w, w_q, cs):
    tm = _pick(cfg.nc, (512, 256))
    nh = cfg.mla_heads
    col = cfg.off_cq // cfg.q_lora
    return pl.pallas_call(
        _mla_q_kernel,
        grid=(cfg.nt // tm,),
        in_specs=[
            pl.BlockSpec((tm, cfg.q_lora), lambda i: (i, col)),
            pl.BlockSpec((1, cfg.q_lora), lambda i: (0, 0)),
            pl.BlockSpec((tm, HEAD_DIM), lambda i: (i, 0)),
            pl.BlockSpec((nh, cfg.q_lora, QK_DIM), lambda i: (0, 0, 0)),
        ],
        out_specs=pl.BlockSpec((nh, tm, QK_DIM), lambda i: (0, i, 0)),
        out_shape=jax.ShapeDtypeStruct((nh, cfg.nt, QK_DIM), BF16),
        compiler_params=_params(("arbitrary",)),
        name="mla_queries",
    )(p, norm_w.reshape(1, cfg.q_lora), cs, w_q)


def _mla_kv_kernel(ckv_ref, slab_ref, nw_ref, cs_ref, w_ref, k_ref, v_ref):
    n = _rms_bf16(ckv_ref[...], nw_ref[...])
    k_rope = _rope_slab(slab_ref[...], cs_ref[...])
    lane = lax.broadcasted_iota(jnp.int32, k_rope.shape, 1)
    ones_col = jnp.where(lane == 0, 1.0, 0.0)
    for h in range(w_ref.shape[0]):
        y = _dot(n, w_ref[h])
        k_ref[h] = jnp.concatenate([y[:, :HEAD_DIM], k_rope], axis=1).astype(k_ref.dtype)
        v_ref[h] = jnp.concatenate([y[:, HEAD_DIM:], ones_col], axis=1).astype(v_ref.dtype)


def mla_keys_values(cfg, p, norm_w, w_kv, cs):
    tm = _pick(cfg.nc, (512, 256))
    nh = cfg.mla_heads
    col = cfg.off_ckv // cfg.kv_lora
    slab_col = cfg.off_slab // HEAD_DIM
    return pl.pallas_call(
        _mla_kv_kernel,
        grid=(cfg.nt // tm,),
        in_specs=[
            pl.BlockSpec((tm, cfg.kv_lora), lambda i: (i, col)),
            pl.BlockSpec((tm, HEAD_DIM), lambda i: (i, slab_col)),
            pl.BlockSpec((1, cfg.kv_lora), lambda i: (0, 0)),
            pl.BlockSpec((tm, HEAD_DIM), lambda i: (i, 0)),
            pl.BlockSpec((nh, cfg.kv_lora, 2 * HEAD_DIM), lambda i: (0, 0, 0)),
        ],
        out_specs=[
            pl.BlockSpec((nh, tm, QK_DIM), lambda i: (0, i, 0)),
            pl.BlockSpec((nh, tm, 2 * HEAD_DIM), lambda i: (0, i, 0)),
        ],
        out_shape=[
            jax.ShapeDtypeStruct((nh, cfg.nt, QK_DIM), BF16),
            jax.ShapeDtypeStruct((nh, cfg.nt, 2 * HEAD_DIM), BF16),
        ],
        compiler_params=_params(("arbitrary",)),
        name="mla_keys_values",
    )(p, p, norm_w.reshape(1, cfg.kv_lora), cs, w_kv)


ATTN_SUB = 256
ATTN_KT = 512
ATTN_SCORE_ROWS = 1024


def _softmax_pv(s_list, v_list):
    m = functools.reduce(jnp.maximum, [jnp.max(s, axis=-1, keepdims=True) for s in s_list])
    o = None
    for s, v in zip(s_list, v_list):
        part = _dot(jnp.exp2(s - m).astype(BF16), v)
        o = part if o is None else o + part
    return o[:, :HEAD_DIM] / o[:, HEAD_DIM:HEAD_DIM + 1]


def _attn_lat_kernel(q_ref, kl_ref, kc_ref, vl_ref, vc_ref, o_ref, s_ref):
    n_keys = kl_ref.shape[0]
    n_slots = s_ref.shape[0] // ATTN_SUB
    for r in range(0, q_ref.shape[0], ATTN_SUB):
        rows = slice(r, r + ATTN_SUB)
        slot = (r // ATTN_SUB) % n_slots
        srows = slice(slot * ATTN_SUB, (slot + 1) * ATTN_SUB)
        q = q_ref[rows, :]
        sc = _dot_nt(q, kc_ref[...])
        m = jnp.max(sc, axis=-1, keepdims=True)
        for c in range(0, n_keys, ATTN_KT):
            s = _dot_nt(q, kl_ref[c:c + ATTN_KT, :])
            s_ref[srows, c:c + ATTN_KT] = s
            m = jnp.maximum(m, jnp.max(s, axis=-1, keepdims=True))
        o = _dot(jnp.exp2(sc - m).astype(BF16), vc_ref[...])
        for c in range(0, n_keys, ATTN_KT):
            p = jnp.exp2((s_ref[srows, c:c + ATTN_KT] - m).astype(BF16))
            o = o + _dot(p, vl_ref[c:c + ATTN_KT, :])
        o_ref[rows, :] = (o[:, :HEAD_DIM] / o[:, HEAD_DIM:HEAD_DIM + 1]).astype(o_ref.dtype)


def _attn_ctx_kernel(q_ref, kc_ref, vc_ref, o_ref):
    o_ref[...] = _softmax_pv([_dot_nt(q_ref[...], kc_ref[...])], [vc_ref[...]]).astype(o_ref.dtype)


def mla_attention(cfg, q, k, v, *, with_ctx, out_rows):
    nh = cfg.mla_heads
    vw = 2 * HEAD_DIM
    tq = _pick(cfg.seq, (2048, 1024, 512, 256))
    nq = cfg.seq // tq
    s_rows = min(tq, ATTN_SCORE_ROWS)
    cb0 = cfg.nl // cfg.ctx
    y_lat = pl.pallas_call(
        _attn_lat_kernel,
        grid=(cfg.batch, nh, nq),
        in_specs=[
            pl.BlockSpec((None, tq, QK_DIM), lambda b, h, i: (h, b * nq + i, 0)),
            pl.BlockSpec((None, cfg.seq, QK_DIM), lambda b, h, i: (h, b, 0)),
            pl.BlockSpec((None, cfg.ctx, QK_DIM), lambda b, h, i: (h, cb0 + b, 0)),
            pl.BlockSpec((None, cfg.seq, vw), lambda b, h, i: (h, b, 0)),
            pl.BlockSpec((None, cfg.ctx, vw), lambda b, h, i: (h, cb0 + b, 0)),
        ],
        out_specs=pl.BlockSpec((tq, HEAD_DIM), lambda b, h, i: (b * nq + i, h)),
        out_shape=jax.ShapeDtypeStruct((out_rows, nh * HEAD_DIM), BF16),
        scratch_shapes=[pltpu.VMEM((s_rows, cfg.seq), F32)],
        compiler_params=_params(("arbitrary", "arbitrary", "arbitrary")),
        name="mla_attention",
    )(q, k, k, v, v)
    if not with_ctx:
        return y_lat, None
    y_ctx = pl.pallas_call(
        _attn_ctx_kernel,
        grid=(cfg.batch, nh),
        in_specs=[
            pl.BlockSpec((None, cfg.ctx, QK_DIM), lambda b, h: (h, cb0 + b, 0)),
            pl.BlockSpec((None, cfg.ctx, QK_DIM), lambda b, h: (h, cb0 + b, 0)),
            pl.BlockSpec((None, cfg.ctx, vw), lambda b, h: (h, cb0 + b, 0)),
        ],
        out_specs=pl.BlockSpec((cfg.ctx, HEAD_DIM), lambda b, h: (b, h)),
        out_shape=jax.ShapeDtypeStruct((cfg.nc, nh * HEAD_DIM), BF16),
        compiler_params=_params(("arbitrary", "arbitrary")),
        name="mla_attention_ctx",
    )(q, k, v)
    return y_lat, y_ctx


def _merge_kernel(ya_ref, yb_ref, yc_ref, ga_ref, gb_ref, gc_ref, wa_ref, wb_ref, wc_ref, o_ref):
    m = (_sigmoid(ga_ref[...]) * _dot(ya_ref[...], wa_ref[...])
         + _sigmoid(gb_ref[...]) * _dot(yb_ref[...], wb_ref[...])
         + _sigmoid(gc_ref[...]) * _dot(yc_ref[...], wc_ref[...]))
    o_ref[...] = m.astype(o_ref.dtype)


def merge_branches(cfg, ya, yb, yc, p, wa, wb, wc, *, rows):
    tm = _pick(cfg.nc, (1024, 512, 256))
    tn = _pick(cfg.d, (512, 256))
    gcols = cfg.d // tn

    def gate_spec(k):
        return pl.BlockSpec((tm, tn), lambda i, j, k=k: (i, k * gcols + j))

    return pl.pallas_call(
        _merge_kernel,
        grid=(rows // tm, cfg.d // tn),
        in_specs=[
            pl.BlockSpec((tm, ya.shape[1]), lambda i, j: (i, 0)),
            pl.BlockSpec((tm, yb.shape[1]), lambda i, j: (i, 0)),
            pl.BlockSpec((tm, yc.shape[1]), lambda i, j: (i, 0)),
            gate_spec(0), gate_spec(1), gate_spec(2),
            pl.BlockSpec((wa.shape[0], tn), lambda i, j: (0, j)),
            pl.BlockSpec((wb.shape[0], tn), lambda i, j: (0, j)),
            pl.BlockSpec((wc.shape[0], tn), lambda i, j: (0, j)),
        ],
        out_specs=pl.BlockSpec((tm, tn), lambda i, j: (i, j)),
        out_shape=jax.ShapeDtypeStruct((rows, cfg.d), BF16),
        compiler_params=_params(("arbitrary", "arbitrary")),
        name="merge_branches",
    )(ya, yb, yc, p, p, p, wa, wb, wc)


def _matmul_resid_kernel(a_ref, w_ref, x_ref, mod_ref, o_ref, *, k):
    o_ref[...] = x_ref[...] + mod_ref[k:k + 1, :] * _dot(a_ref[...], w_ref[...])


def matmul_residual(cfg, a, w, x, mod_l, *, k, rows):
    tm = _pick(cfg.nc, (1024, 512, 256))
    tn = _pick(cfg.d, (512, 256))
    row_of = _mod_row_map(cfg, tm)
    return pl.pallas_call(
        functools.partial(_matmul_resid_kernel, k=k),
        grid=(rows // tm, cfg.d // tn),
        in_specs=[
            pl.BlockSpec((tm, a.shape[1]), lambda i, j: (i, 0)),
            pl.BlockSpec((a.shape[1], tn), lambda i, j: (0, j)),
            pl.BlockSpec((tm, tn), lambda i, j: (i, j)),
            pl.BlockSpec((None, 6, tn), lambda i, j: (row_of(i), 0, j)),
        ],
        out_specs=pl.BlockSpec((tm, tn), lambda i, j: (i, j)),
        out_shape=jax.ShapeDtypeStruct((rows, cfg.d), F32),
        compiler_params=_params(("arbitrary", "arbitrary")),
        name="matmul_residual",
    )(a, w, x, mod_l)


def _top16_pass(s_ref, top_ref, rank_ref, break_ties):
    s = s_ref[...]
    n = s.shape[0]
    row = lax.broadcasted_iota(jnp.int32, s.shape, 0).astype(F32)
    rank = jnp.full(s.shape, UNRANKED, F32)
    work = s
    for r in range(PEER_TOPK):
        m = jnp.max(work, axis=0, keepdims=True)
        sel = work == m
        if break_ties:
            sel = row == jnp.min(jnp.where(sel, row, float(n)), axis=0, keepdims=True)
        rank = jnp.where(sel, float(r), rank)
        work = jnp.where(sel, -jnp.inf, work)
        top_ref[r:r + 1, :] = m
    rank_ref[...] = rank


def _peer_tables_kernel(qh_ref, keys_ref, b_ref, e2_ref, c_ref, e1_ref, cnt_ref, *scratch, tile):
    lanes = HEAD_DIM
    n_arr = 2 * (tile // lanes)
    s_refs, top_refs, rank_refs = (scratch[i * n_arr:(i + 1) * n_arr] for i in range(3))
    k1 = keys_ref[0]
    k2 = keys_ref[1]
    ranked = jnp.zeros((1, lanes), F32)
    for part in range(tile // lanes):
        qh = qh_ref[part * lanes:(part + 1) * lanes, :].astype(BF16)
        s_refs[2 * part][...] = _dot_nt(k1, qh[:, :lanes])
        s_refs[2 * part + 1][...] = _dot_nt(k2, qh[:, lanes:])
        for idx in (2 * part, 2 * part + 1):
            _top16_pass(s_refs[idx], top_refs[idx], rank_refs[idx], break_ties=False)
            n_ranked = jnp.sum(jnp.where(rank_refs[idx][...] < UNRANKED, 1.0, 0.0), axis=0, keepdims=True)
            ranked = jnp.maximum(ranked, n_ranked)

    @pl.when(jnp.max(ranked) > float(PEER_TOPK))
    def _():
        for idx in range(n_arr):
            _top16_pass(s_refs[idx], top_refs[idx], rank_refs[idx], break_ties=True)

    for part in range(tile // lanes):
        cols = slice(part * lanes, (part + 1) * lanes)
        s1 = s_refs[2 * part][...]
        s2 = s_refs[2 * part + 1][...]
        rank1 = rank_refs[2 * part][...]
        rank2 = rank_refs[2 * part + 1][...]
        t1 = top_refs[2 * part][...]
        t2 = top_refs[2 * part + 1][...]
        pieces = [t1[0:1] + t2]
        pos = [lax.broadcasted_iota(jnp.int32, (PEER_TOPK, lanes), 0).astype(F32)]
        for a in range(1, 8):
            pieces.append(t1[a:a + 1] + t2[0:8])
            pos.append(lax.broadcasted_iota(jnp.int32, (8, lanes), 0).astype(F32) + float(a * PEER_TOPK))
        pieces.append(t1[8:16] + t2[0:1])
        pos.append((lax.broadcasted_iota(jnp.int32, (8, lanes), 0).astype(F32) + 8.0) * float(PEER_TOPK))
        cand = jnp.concatenate(pieces, axis=0)
        cpos = jnp.concatenate(pos, axis=0)
        a_row = lax.broadcasted_iota(jnp.int32, (PEER_TOPK, lanes), 0).astype(F32)
        cnt = jnp.zeros((PEER_TOPK, lanes), F32)
        z = jnp.zeros((1, lanes), F32)
        best0 = None
        for r in range(PEER_TOPK):
            m = jnp.max(cand, axis=0, keepdims=True)
            first = jnp.min(jnp.where(cand == m, cpos, 1e9), axis=0, keepdims=True)
            cand = jnp.where(cpos == first, -jnp.inf, cand)
            cnt = cnt + jnp.where(a_row == jnp.floor(first * (1.0 / PEER_TOPK)), 1.0, 0.0)
            if r == 0:
                best0 = m
            z = z + jnp.exp(m - best0)
        cnt_ref[...] = cnt
        c_tab = jnp.zeros((PEER_NKEYS, lanes), F32)
        for a in range(PEER_TOPK):
            c_tab = jnp.where(rank1 == float(a), cnt_ref[a:a + 1, :], c_tab)
        b_ref[:, cols] = rank2.astype(b_ref.dtype)
        c_ref[:, cols] = c_tab
        e1_ref[:, cols] = jnp.exp(s1 - t1[0:1]) / z
        e2_ref[:, cols] = jnp.exp(s2 - t2[0:1]).astype(e2_ref.dtype)


def peer_tables(cfg, qh, keys, *, rows):
    tile = _pick(cfg.nc, (512, 256))
    n_arr = 2 * (tile // HEAD_DIM)
    nh = cfg.peer_heads
    shapes = [jax.ShapeDtypeStruct((nh, PEER_NKEYS, rows), dt) for dt in (BF16, BF16, F32, F32)]
    out_spec = pl.BlockSpec((None, PEER_NKEYS, tile), lambda i, h: (h, 0, i))
    return pl.pallas_call(
        functools.partial(_peer_tables_kernel, tile=tile),
        grid=(rows // tile, nh),
        in_specs=[
            pl.BlockSpec((tile, 2 * HEAD_DIM), lambda i, h: (i, h)),
            pl.BlockSpec((2, None, PEER_NKEYS, HEAD_DIM), lambda i, h: (0, h, 0, 0)),
        ],
        out_specs=[out_spec] * 4,
        out_shape=shapes,
        scratch_shapes=([pltpu.VMEM((PEER_TOPK, HEAD_DIM), F32)]
                        + [pltpu.VMEM((PEER_NKEYS, HEAD_DIM), F32) for _ in range(n_arr)]
                        + [pltpu.VMEM((PEER_TOPK, HEAD_DIM), F32) for _ in range(n_arr)]
                        + [pltpu.VMEM((PEER_NKEYS, HEAD_DIM), F32) for _ in range(n_arr)]),
        compiler_params=_params(("arbitrary", "arbitrary")),
        name="peer_tables",
    )(qh, keys)


PEER_ROWS = 8
PEER_EB = PEER_ROWS * PEER_NKEYS
PEER_DOT_ROWS = 1


def _peer_dense_kernel(ht_ref, u_ref, vt_ref, b_ref, e2_ref, c_ref, e1_ref, x_ref, mod_ref, o_ref,
                       acc_ref, w_ref, *, heads):
    e = pl.program_id(1)

    @pl.when(e == 0)
    def _():
        acc_ref[...] = jnp.zeros_like(acc_ref)

    zero = jnp.zeros((), BF16)
    ht = ht_ref[...]
    for r0 in range(0, PEER_ROWS, PEER_DOT_ROWS):
        blk = slice(r0 * PEER_NKEYS, (r0 + PEER_DOT_ROWS) * PEER_NKEYS)
        act = _dot(u_ref[blk, :], ht)
        gelu = (0.5 * act * (1.0 + lax.erf(act * float(math.sqrt(0.5))))).astype(BF16)
        for q in range(PEER_DOT_ROWS):
            r = r0 + q
            g = jnp.zeros((PEER_NKEYS, act.shape[1]), BF16)
            for h in range(heads):
                crow = jnp.broadcast_to(c_ref[h, r:r + 1, :], g.shape).astype(BF16)
                erow = jnp.broadcast_to(e1_ref[h, r:r + 1, :], g.shape).astype(BF16)
                g = g + jnp.where(b_ref[h] < crow, e2_ref[h] * erow, zero)
            w_ref[r * PEER_NKEYS:(r + 1) * PEER_NKEYS, :] = g * gelu[q * PEER_NKEYS:(q + 1) * PEER_NKEYS]
    acc_ref[...] += _dot(vt_ref[...], w_ref[...])

    @pl.when(e == pl.num_programs(1) - 1)
    def _():
        o_ref[...] = x_ref[...] + mod_ref[5:6, :] * acc_ref[...].T


def peer_dense(cfg, ht, u, vt, tabs, x, mod_l, *, layer, rows):
    tile = _pick(cfg.nc, (512, 256))
    eb = PEER_EB
    n_exp = u.shape[1]
    nh = cfg.peer_heads
    row_of = _mod_row_map(cfg, tile)
    tab_spec = pl.BlockSpec((nh, PEER_NKEYS, tile), lambda i, e: (0, 0, i))
    row_spec = pl.BlockSpec((nh, PEER_ROWS, tile), lambda i, e: (0, e, i))
    return pl.pallas_call(
        functools.partial(_peer_dense_kernel, heads=nh),
        grid=(rows // tile, n_exp // eb),
        in_specs=[
            pl.BlockSpec((cfg.d, tile), lambda i, e: (0, i)),
            pl.BlockSpec((None, eb, cfg.d), lambda i, e: (layer, e, 0)),
            pl.BlockSpec((None, cfg.d, eb), lambda i, e: (layer, 0, e)),
            tab_spec, tab_spec, row_spec, row_spec,
            pl.BlockSpec((tile, cfg.d), lambda i, e: (i, 0)),
            pl.BlockSpec((None, 6, cfg.d), lambda i, e: (row_of(i), 0, 0)),
        ],
        out_specs=pl.BlockSpec((tile, cfg.d), lambda i, e: (i, 0)),
        out_shape=jax.ShapeDtypeStruct((rows, cfg.d), F32),
        scratch_shapes=[pltpu.VMEM((cfg.d, tile), F32), pltpu.VMEM((eb, tile), BF16)],
        compiler_params=_params(("arbitrary", "arbitrary")),
        name="peer_dense",
    )(ht, u, vt, *tabs, x, mod_l)


def _final_norm_kernel(x_ref, w_ref, o_ref):
    x = x_ref[...]
    o_ref[...] = x * lax.rsqrt(jnp.mean(x * x, axis=-1, keepdims=True) + RMS_EPS) * w_ref[...]


def final_norm(cfg, x, w):
    tm = _pick(cfg.nl, (1024, 512, 256))
    return pl.pallas_call(
        _final_norm_kernel,
        grid=(cfg.nl // tm,),
        in_specs=[pl.BlockSpec((tm, cfg.d), lambda i: (i, 0)), pl.BlockSpec((1, cfg.d), lambda i: (0, 0))],
        out_specs=pl.BlockSpec((tm, cfg.d), lambda i: (i, 0)),
        out_shape=jax.ShapeDtypeStruct((cfg.nl, cfg.d), F32),
        compiler_params=_params(("arbitrary",)),
        name="final_norm",
    )(x, w.reshape(1, cfg.d))


def _rot_cols(w):
    q = ROPE_DIM // 4
    return jnp.concatenate([-w[..., q:2 * q], w[..., 0:q], -w[..., 3 * q:4 * q], w[..., 2 * q:3 * q]], axis=-1)


def _in_proj_weight(cfg, w_in, n_cols):
    d = cfg.d
    hg_end = 5 * cfg.hgw
    pool_end = hg_end + POOL_WIDTH
    cq_end = pool_end + cfg.q_lora
    ckv_end = cq_end + cfg.kv_lora
    rope_end = ckv_end + ROPE_DIM
    w_in = w_in.astype(BF16)
    k_rope = w_in[..., ckv_end:rope_end]
    parts = [w_in[..., rope_end:rope_end + 3 * d], w_in[..., :ckv_end], k_rope, _rot_cols(k_rope)]
    parts.append(jnp.zeros(w_in.shape[:-1] + (n_cols - cfg.in_cols,), BF16))
    return jnp.concatenate(parts, axis=-1)


def _mla_q_weight(cfg, w_uq):
    w = w_uq.reshape(cfg.q_lora, cfg.mla_heads, HEAD_DIM + ROPE_DIM)
    rope = w[..., HEAD_DIM:]
    w = jnp.concatenate([w[..., :HEAD_DIM], rope, _rot_cols(rope)], axis=-1)
    return jnp.transpose(w, (1, 0, 2)).astype(BF16)


def _mla_kv_weight(cfg, w_ukv):
    w = w_ukv.reshape(cfg.kv_lora, cfg.mla_heads, 2 * HEAD_DIM)
    return jnp.transpose(w, (1, 0, 2)).astype(BF16)


def _rope_table(cfg):
    rows = cfg.seq // cfg.grid_w
    r, col = jnp.meshgrid(jnp.arange(rows), jnp.arange(cfg.grid_w), indexing="ij")
    n_freq = ROPE_DIM // 4
    freqs = ROPE_THETA ** (-jnp.arange(n_freq, dtype=F32) / n_freq)
    ang_r = r.reshape(-1)[:, None] * freqs
    ang_c = col.reshape(-1)[:, None] * freqs
    cos = jnp.concatenate([jnp.cos(ang_r)] * 2 + [jnp.cos(ang_c)] * 2, axis=1)
    sin = jnp.concatenate([jnp.sin(ang_r)] * 2 + [jnp.sin(ang_c)] * 2, axis=1)
    lat = jnp.tile(jnp.concatenate([cos, sin], axis=1).astype(F32), (cfg.batch, 1))
    ctx = jnp.concatenate([jnp.ones((cfg.nc, ROPE_DIM), F32), jnp.zeros((cfg.nc, ROPE_DIM), F32)], axis=1)
    return jnp.concatenate([lat, ctx], axis=0)


def _forward(cfg, x, c, ctx, c_ctx, w_mod, b_mod, norm_mix, norm_ffn, w_in, hg_lb_logits, hg_norm,
             pool_w, pool_scale, mla_q_norm, mla_w_uq, mla_kv_norm, mla_w_ukv,
             w_branch_a, w_branch_b, w_branch_c, w_out, peer_wq, peer_keys, peer_u, peer_v, final_w):
    d = cfg.d
    assert cfg.seq % HG_CHUNK == 0 and cfg.ctx % HG_CHUNK == 0 and cfg.nl % cfg.ctx == 0
    assert cfg.off_pool % POOL_WIDTH == 0 and cfg.off_cq % cfg.q_lora == 0
    assert cfg.off_ckv % cfg.kv_lora == 0 and cfg.batch < MOD_ROWS
    tn_in = 1536 if d % 256 == 0 and cfg.in_cols > 8192 else 256
    n_cols = -(-cfg.in_cols // tn_in) * tn_in
    tm = _pick(cfg.nc, (1024, 512, 256))

    xs = jnp.concatenate([x.reshape(cfg.nl, d), ctx.reshape(cfg.nc, d)], axis=0)
    c_all = jnp.concatenate([c, c_ctx[None], jnp.zeros((MOD_ROWS - cfg.batch - 1, d), F32)], axis=0)
    mod = adaln_tables(cfg, c_all, w_mod, b_mod)
    cs = _rope_table(cfg)
    w_in_all = _in_proj_weight(cfg, w_in, n_cols)
    wq_all = peer_wq.astype(BF16)
    u_all = peer_u.astype(BF16)
    vt_all = jnp.swapaxes(peer_v.astype(BF16), 1, 2)

    for l in range(cfg.depth):
        last = l == cfg.depth - 1
        rows = cfg.nl if last else cfg.nt
        mod_l = mod[l]
        p = norm_matmul(cfg, xs, norm_mix[l], mod_l, w_in_all, layer=l,
                        k0=0, rows=cfg.nt, tm=tm, tn=tn_in, emit_h=False)
        ya, hg_ctx = hgrn_mixer(cfg, p, hg_lb_logits, hg_norm[l], l, rows)
        pw = pool_w[l].astype(BF16)
        yb = pool_mixer(cfg, p, pw, pool_scale[l], n_seq=cfg.batch, seq_len=cfg.seq, row0=0, out_rows=rows)
        q = mla_queries(cfg, p, mla_q_norm[l], _mla_q_weight(cfg, mla_w_uq[l]), cs)
        k, v = mla_keys_values(cfg, p, mla_kv_norm[l], _mla_kv_weight(cfg, mla_w_ukv[l]), cs)
        yc, att_ctx = mla_attention(cfg, q, k, v, with_ctx=not last, out_rows=rows)
        if not last:
            pool_ctx = pool_mixer(cfg, p, pw, pool_scale[l], n_seq=cfg.batch, seq_len=cfg.ctx, row0=cfg.nl)
            ya = lax.dynamic_update_slice(ya, hg_ctx, (cfg.nl, 0))
            yb = lax.dynamic_update_slice(yb, pool_ctx, (cfg.nl, 0))
            yc = lax.dynamic_update_slice(yc, att_ctx, (cfg.nl, 0))
        m = merge_branches(cfg, ya, yb, yc, p, w_branch_a[l].astype(BF16), w_branch_b[l].astype(BF16),
                           w_branch_c[l].astype(BF16), rows=rows)
        xs = matmul_residual(cfg, m, w_out[l].astype(BF16), xs, mod_l, k=2, rows=rows)
        qh, h2t = norm_matmul(cfg, xs, norm_ffn[l], mod_l, wq_all, layer=l,
                             k0=3, rows=rows, tm=tm, tn=_pick(peer_wq.shape[2], (512, 256)), emit_h=True)
        tabs = peer_tables(cfg, qh, peer_keys[l].astype(BF16), rows=rows)
        xs = peer_dense(cfg, h2t, u_all, vt_all, tabs, xs, mod_l, layer=l, rows=rows)

    return final_norm(cfg, xs, final_w).reshape(cfg.batch, cfg.seq, d)


def kernel(x, c, ctx, c_ctx, w_mod, b_mod, norm_mix, norm_ffn, w_in, hg_lb_logits, hg_norm, pool_w, pool_scale,
           mla_q_norm, mla_w_uq, mla_kv_norm, mla_w_ukv, w_branch_a, w_branch_b, w_branch_c, w_out,
           peer_wq, peer_keys, peer_u, peer_v, final_norm):
    batch, seq, d = x.shape
    cfg = Cfg(d=d, batch=batch, seq=seq, ctx=ctx.shape[1], grid_w=64, depth=w_mod.shape[0],
              hg_heads=hg_lb_logits.shape[2] // HEAD_DIM,
              mla_heads=mla_w_ukv.shape[2] // (2 * HEAD_DIM), q_lora=mla_q_norm.shape[1],
              kv_lora=mla_kv_norm.shape[1], peer_heads=peer_keys.shape[2])
    return _forward(cfg, x, c, ctx, c_ctx, w_mod, b_mod, norm_mix, norm_ffn, w_in, hg_lb_logits, hg_norm,
                    pool_w, pool_scale, mla_q_norm, mla_w_uq, mla_kv_norm, mla_w_ukv,
                    w_branch_a, w_branch_b, w_branch_c, w_out, peer_wq, peer_keys, peer_u, peer_v, final_norm)
```

```python
import functools
import math
from typing import NamedTuple

import numpy as np
import jax
import jax.numpy as jnp
from jax import lax
from jax.experimental import pallas as pl
from jax.experimental.pallas import tpu as pltpu

F32 = jnp.float32
BF16 = jnp.bfloat16

RMS_EPS = 1e-6
ROPE_THETA = 10000.0
HEAD_DIM = 128
ROPE_DIM = 64
QK_DIM = 256
LOG2_E = float(math.log2(math.e))
QK_LOG2_SCALE = float((HEAD_DIM + ROPE_DIM) ** -0.5) * LOG2_E
POOL_WINDOWS = (2, 4, 8, 16)
POOL_GROUP = 256
POOL_WIDTH = POOL_GROUP * len(POOL_WINDOWS)
POOL_HALO = 8
PEER_NKEYS = 128
PEER_TOPK = 16
HG_CHUNK = 128
HG_LEVELS = 7
UNRANKED = 99.0
MOD_ROWS = 8
VMEM_LIMIT = 56 * 1024 * 1024


class Cfg(NamedTuple):
    d: int
    batch: int
    seq: int
    ctx: int
    grid_w: int
    depth: int
    hg_heads: int
    mla_heads: int
    q_lora: int
    kv_lora: int
    peer_heads: int

    @property
    def nl(self):
        return self.batch * self.seq

    @property
    def nc(self):
        return self.batch * self.ctx

    @property
    def nt(self):
        return self.nl + self.nc

    @property
    def hgw(self):
        return self.hg_heads * HEAD_DIM

    @property
    def off_gate(self):
        return 0

    @property
    def off_hg(self):
        return 3 * self.d

    @property
    def off_pool(self):
        return self.off_hg + 5 * self.hgw

    @property
    def off_cq(self):
        return self.off_pool + POOL_WIDTH

    @property
    def off_ckv(self):
        return self.off_cq + self.q_lora

    @property
    def off_slab(self):
        return self.off_ckv + self.kv_lora

    @property
    def in_cols(self):
        return self.off_slab + HEAD_DIM


def _params(sem):
    return pltpu.CompilerParams(dimension_semantics=sem, vmem_limit_bytes=VMEM_LIMIT)


def _dot(a, b):
    return jnp.dot(a, b, preferred_element_type=F32)


def _dot_nt(a, b):
    return lax.dot_general(a, b, (((1,), (1,)), ((), ())), preferred_element_type=F32)


def _sigmoid(x):
    return 1.0 / (1.0 + jnp.exp(-x))


def _pick(n, prefs):
    for p in prefs:
        if n % p == 0:
            return p
    raise ValueError(f"no tile for {n} in {prefs}")


def _mod_row_map(cfg, tm):
    n_lat = cfg.nl // tm
    per_batch = cfg.seq // tm
    return lambda i: jnp.where(i < n_lat, i // per_batch, cfg.batch)


def _mod_kernel(c_ref, w_ref, b_ref, o_ref):
    @pl.when(pl.program_id(1) == 0)
    def _():
        o_ref[...] = jnp.broadcast_to(b_ref[...], o_ref.shape)

    c = c_ref[...]
    s = (c * _sigmoid(c)).astype(BF16)
    o_ref[...] += _dot(s, w_ref[...].astype(BF16))


def adaln_tables(cfg, c_all, w_mod, b_mod):
    d6 = 6 * cfg.d
    tk = _pick(cfg.d, (256, 128))
    out = pl.pallas_call(
        _mod_kernel,
        grid=(cfg.depth, cfg.d // tk),
        in_specs=[
            pl.BlockSpec((MOD_ROWS, tk), lambda l, k: (0, k)),
            pl.BlockSpec((None, tk, d6), lambda l, k: (l, k, 0)),
            pl.BlockSpec((None, 1, d6), lambda l, k: (l, 0, 0)),
        ],
        out_specs=pl.BlockSpec((None, MOD_ROWS, d6), lambda l, k: (l, 0, 0)),
        out_shape=jax.ShapeDtypeStruct((cfg.depth, MOD_ROWS, d6), F32),
        compiler_params=_params(("arbitrary", "arbitrary")),
        name="adaln_tables",
    )(c_all, w_mod, b_mod.reshape(cfg.depth, 1, d6))
    return out.reshape(cfg.depth, MOD_ROWS, 6, cfg.d)


def _norm_matmul_kernel(x_ref, gain_ref, mod_ref, w_ref, *rest, k0, emit_h):
    if emit_h:
        o_ref, hout_ref, h_ref = rest
    else:
        o_ref, h_ref = rest

    @pl.when(pl.program_id(1) == 0)
    def _():
        x = x_ref[...]
        y = x * lax.rsqrt(jnp.mean(x * x, axis=-1, keepdims=True) + RMS_EPS) * gain_ref[...]
        h = (y * (1.0 + mod_ref[k0 + 1:k0 + 2, :]) + mod_ref[k0:k0 + 1, :]).astype(BF16)
        h_ref[...] = h
        if emit_h:
            hout_ref[...] = h.T

    o_ref[...] = _dot(h_ref[...], w_ref[...])


def norm_matmul(cfg, x, gain, mod_l, w, *, layer, k0, rows, tm, tn, emit_h):
    n = w.shape[2]
    row_of = _mod_row_map(cfg, tm)
    out_shape = [jax.ShapeDtypeStruct((rows, n), F32)]
    out_specs = [pl.BlockSpec((tm, tn), lambda i, j: (i, j))]
    if emit_h:
        out_shape.append(jax.ShapeDtypeStruct((cfg.d, rows), BF16))
        out_specs.append(pl.BlockSpec((cfg.d, tm), lambda i, j: (0, i)))
    res = pl.pallas_call(
        functools.partial(_norm_matmul_kernel, k0=k0, emit_h=emit_h),
        grid=(rows // tm, n // tn),
        in_specs=[
            pl.BlockSpec((tm, cfg.d), lambda i, j: (i, 0)),
            pl.BlockSpec((1, cfg.d), lambda i, j: (0, 0)),
            pl.BlockSpec((None, 6, cfg.d), lambda i, j: (row_of(i), 0, 0)),
            pl.BlockSpec((None, cfg.d, tn), lambda i, j: (layer, 0, j)),
        ],
        out_specs=out_specs,
        out_shape=out_shape,
        scratch_shapes=[pltpu.VMEM((tm, cfg.d), BF16)],
        compiler_params=_params(("arbitrary", "arbitrary")),
        name="norm_matmul_h" if emit_h else "norm_matmul",
    )(x, gain.reshape(1, cfg.d), mod_l, w)
    return res if emit_h else res[0]


def _hgrn_consts():
    c = HG_CHUNK
    out = []
    for rev in (False, True):
        p = np.arange(c) if not rev else c - 1 - np.arange(c)
        pt, pu = p[:, None], p[None, :]
        g = np.zeros((HG_LEVELS + 1, c, c), np.float32)
        up = np.zeros((HG_LEVELS, c, c), np.float32)
        g[0] = pu <= pt
        for l in range(HG_LEVELS):
            m = 1 << l
            blk = p >> (l + 1)
            upper = ((p >> l) & 1) == 1
            mid = (blk * 2 * m + m)[:, None]
            same = blk[:, None] == blk[None, :]
            g_up = same & (pu >= mid) & (pu <= pt)
            g_lo = same & (pu > pt) & (pu < mid)
            g[1 + l] = np.where(upper[:, None], g_up, g_lo)
            up[l] = np.broadcast_to(upper[:, None], (c, c))
        x = pt ^ pu
        lv = np.where(pu < pt, np.floor(np.log2(np.maximum(x, 1))), -1.0).astype(np.float32)
        out.append((jnp.asarray(g.reshape(-1, c), BF16), jnp.asarray(up, F32), jnp.asarray(lv, F32)))
    return out


def _hgrn_chunk(q_raw, v, f_raw, log_lb, log_1mlb, one_m_lb, g_ref, up_ref, lv_ref, st_ref, end_row):
    c = HG_CHUNK
    q = q_raw * _sigmoid(q_raw)
    e = jnp.exp(-jnp.abs(f_raw))
    one_pe = 1.0 + e
    log_sig = jnp.minimum(f_raw, 0.0) - jnp.log(one_pe)
    t = log_1mlb + log_sig
    log_f = jnp.maximum(log_lb, t) + jnp.log(1.0 + jnp.exp(-jnp.abs(log_lb - t)))
    k = one_m_lb * jnp.where(f_raw >= 0.0, e, 1.0) / one_pe
    hi = log_f.astype(BF16)
    lo = (log_f - hi.astype(F32)).astype(BF16)
    a2 = _dot(g_ref[...], jnp.concatenate([hi, lo], axis=1))
    a = a2[:, :HEAD_DIM] + a2[:, HEAD_DIM:]
    b = a[0:c]
    lv = lv_ref[...]
    scores = jnp.zeros((c, c), F32)
    for l in range(HG_LEVELS):
        e_l = jnp.exp(a[(1 + l) * c:(2 + l) * c])
        x = (jnp.where(up_ref[l] > 0.5, q, k) * e_l).astype(BF16)
        scores = jnp.where(lv == float(l), _dot_nt(x, x), scores)
    b_end = b[end_row:end_row + 1, :]
    qb = (q * jnp.exp(b)).astype(BF16)
    kd = (k * jnp.exp(b_end - b)).astype(BF16)
    st = st_ref[...]
    vb = v.astype(BF16)
    o = (_dot(scores.astype(BF16), vb) + _dot_nt(qb, st.astype(BF16))
         + jnp.sum(q * k, axis=-1, keepdims=True) * v)
    st_ref[...] = st * jnp.exp(b_end) + _dot(v.T.astype(BF16), kd)
    return o


def _hgrn_kernel(ql, ffl, fbl, il, gl, qc, ffc, fbc, ic, gc, logit_ref, nw_ref,
                 gf_ref, upf_ref, lvf_ref, gb_ref, upb_ref, lvb_ref,
                 yl_ref, yc_ref, ofl, obl, ofc, obc, stf, stb, *, layer, seq, ctx):
    c = HG_CHUNK
    depth = logit_ref.shape[0]
    lg = [logit_ref[dd] for dd in range(depth)]
    mx = functools.reduce(jnp.maximum, lg)
    ex = [jnp.exp(v - mx) for v in lg]
    tot = functools.reduce(jnp.add, ex)
    cum = [ex[0] / tot]
    for dd in range(1, layer + 1):
        cum.append(cum[-1] + ex[dd] / tot)
    lb = cum[layer] - cum[0]
    log_lb = jnp.log(lb)
    log_1mlb = jnp.log1p(-lb)
    one_m_lb = 1.0 - lb

    stf[...] = jnp.zeros_like(stf)
    stb[...] = jnp.zeros_like(stb)

    def segment(q_ref, ff_ref, fb_ref, i_ref, of_ref, ob_ref, n):
        def body(j, carry):
            rf = pl.multiple_of(j * c, c)
            rb = pl.multiple_of((n - 1 - j) * c, c)
            of_ref[pl.ds(rf, c), :] = _hgrn_chunk(
                q_ref[pl.ds(rf, c), :], i_ref[pl.ds(rf, c), :], ff_ref[pl.ds(rf, c), :],
                log_lb[0:1], log_1mlb[0:1], one_m_lb[0:1], gf_ref, upf_ref, lvf_ref, stf, c - 1)
            ob_ref[pl.ds(rb, c), :] = _hgrn_chunk(
                q_ref[pl.ds(rb, c), :], i_ref[pl.ds(rb, c), :], fb_ref[pl.ds(rb, c), :],
                log_lb[1:2], log_1mlb[1:2], one_m_lb[1:2], gb_ref, upb_ref, lvb_ref, stb, 0)
            return carry
        lax.fori_loop(0, n, body, 0, unroll=4 if n % 4 == 0 else 2)

    segment(qc, ffc, fbc, ic, ofc, obc, ctx // c)
    segment(ql, ffl, fbl, il, ofl, obl, seq // c)

    nw = nw_ref[...]

    def readout(of_ref, ob_ref, g_ref, y_ref, n):
        def body(j, carry):
            r = pl.multiple_of(j * c, c)
            o = of_ref[pl.ds(r, c), :] + ob_ref[pl.ds(r, c), :]
            y = o * lax.rsqrt(jnp.mean(o * o, axis=-1, keepdims=True) + RMS_EPS) * nw
            g = g_ref[pl.ds(r, c), :]
            y_ref[pl.ds(r, c), :] = (y * (g * _sigmoid(g))).astype(y_ref.dtype)
            return carry
        lax.fori_loop(0, n, body, 0)

    readout(ofc, obc, gc, yc_ref, ctx // c)
    readout(ofl, obl, gl, yl_ref, seq // c)


def hgrn_mixer(cfg, p, logits, norm_w, layer, out_rows):
    hd = HEAD_DIM
    nh = cfg.hg_heads
    col0 = cfg.off_hg // hd
    ctx_blk0 = cfg.nl // cfg.ctx
    (gf, upf, lvf), (gb, upb, lvb) = _hgrn_consts()

    def lat_spec(part):
        return pl.BlockSpec((cfg.seq, hd), lambda b, h, part=part: (b, col0 + part * nh + h))

    def ctx_spec(part):
        return pl.BlockSpec((cfg.ctx, hd), lambda b, h, part=part: (ctx_blk0 + b, col0 + part * nh + h))

    parts = (0, 1, 2, 3, 4)

    def const(arr):
        return pl.BlockSpec(arr.shape, lambda b, h, nd=arr.ndim: (0,) * nd)

    y_lat, y_ctx = pl.pallas_call(
        functools.partial(_hgrn_kernel, layer=layer, seq=cfg.seq, ctx=cfg.ctx),
        grid=(cfg.batch, nh),
        in_specs=[lat_spec(k) for k in parts] + [ctx_spec(k) for k in parts] + [
            pl.BlockSpec((cfg.depth, 2, hd), lambda b, h: (0, 0, h)),
            pl.BlockSpec((1, hd), lambda b, h: (0, 0)),
            const(gf), const(upf), const(lvf), const(gb), const(upb), const(lvb),
        ],
        out_specs=[
            pl.BlockSpec((cfg.seq, hd), lambda b, h: (b, h)),
            pl.BlockSpec((cfg.ctx, hd), lambda b, h: (b, h)),
        ],
        out_shape=[
            jax.ShapeDtypeStruct((out_rows, cfg.hgw), BF16),
            jax.ShapeDtypeStruct((cfg.nc, cfg.hgw), BF16),
        ],
        scratch_shapes=[
            pltpu.VMEM((cfg.seq, hd), F32), pltpu.VMEM((cfg.seq, hd), F32),
            pltpu.VMEM((cfg.ctx, hd), F32), pltpu.VMEM((cfg.ctx, hd), F32),
            pltpu.VMEM((hd, hd), F32), pltpu.VMEM((hd, hd), F32),
        ],
        compiler_params=_params(("arbitrary", "arbitrary")),
        name="hgrn_mixer",
    )(*([p] * 10), logits, norm_w.reshape(1, hd), gf, upf, lvf, gb, upb, lvb)
    return y_lat, y_ctx


def _pool_kernel(prev_ref, cur_ref, next_ref, w_ref, scale_ref, o_ref, buf_ref, *, seq_len, tile, n_tiles):
    i = pl.program_id(1)
    h = POOL_HALO
    cur = cur_ref[...]
    buf_ref[0:h, :] = jnp.where(i > 0, prev_ref[...], 0.0)
    buf_ref[h:h + tile, :] = cur
    buf_ref[h + tile:2 * h + tile, :] = jnp.where(i < n_tiles - 1, next_ref[...], 0.0)
    pos = i * tile + lax.broadcasted_iota(jnp.int32, (tile, POOL_GROUP), 0)
    for gi, win in enumerate(POOL_WINDOWS):
        half = win // 2
        cols = slice(gi * POOL_GROUP, (gi + 1) * POOL_GROUP)
        acc = buf_ref[h - half:h - half + tile, cols]
        for dlt in range(-half + 1, half):
            acc = acc + buf_ref[h + dlt:h + dlt + tile, cols]
        cnt = (jnp.minimum(pos + half, seq_len) - jnp.maximum(pos - half, 0)).astype(F32)
        pooled = acc / cnt - cur[:, cols]
        mixed = _dot(pooled.astype(BF16), w_ref[gi])
        o_ref[:, cols] = (mixed * scale_ref[:, cols]).astype(o_ref.dtype)


def pool_mixer(cfg, p, w_pool, scale, *, n_seq, seq_len, row0, out_rows=None):
    out_rows = n_seq * seq_len if out_rows is None else out_rows
    tile = _pick(seq_len, (512, 256, 128))
    n_tiles = seq_len // tile
    blk0 = row0 // tile
    col = cfg.off_pool // POOL_WIDTH
    per8 = tile // POOL_HALO
    last8 = cfg.nt // POOL_HALO - 1

    def cur_map(s, i):
        return (blk0 + s * n_tiles + i, col)

    def prev_map(s, i):
        return (jnp.maximum((blk0 + s * n_tiles + i) * per8 - 1, 0), col)

    def next_map(s, i):
        return (jnp.minimum((blk0 + s * n_tiles + i + 1) * per8, last8), col)

    return pl.pallas_call(
        functools.partial(_pool_kernel, seq_len=seq_len, tile=tile, n_tiles=n_tiles),
        grid=(n_seq, n_tiles),
        in_specs=[
            pl.BlockSpec((POOL_HALO, POOL_WIDTH), prev_map),
            pl.BlockSpec((tile, POOL_WIDTH), cur_map),
            pl.BlockSpec((POOL_HALO, POOL_WIDTH), next_map),
            pl.BlockSpec((len(POOL_WINDOWS), POOL_GROUP, POOL_GROUP), lambda s, i: (0, 0, 0)),
            pl.BlockSpec((1, POOL_WIDTH), lambda s, i: (0, 0)),
        ],
        out_specs=pl.BlockSpec((tile, POOL_WIDTH), lambda s, i: (s * n_tiles + i, 0)),
        out_shape=jax.ShapeDtypeStruct((out_rows, POOL_WIDTH), BF16),
        scratch_shapes=[pltpu.VMEM((tile + 2 * POOL_HALO, POOL_WIDTH), F32)],
        compiler_params=_params(("arbitrary", "arbitrary")),
        name="pool_mixer",
    )(p, p, p, w_pool, scale.reshape(1, POOL_WIDTH))


def _rope_slab(slab, cs):
    t = slab * cs
    r = t + pltpu.roll(t, ROPE_DIM, axis=1)
    lane = lax.broadcasted_iota(jnp.int32, r.shape, 1)
    return jnp.where(lane < ROPE_DIM, r, 0.0)


def _rms_bf16(x, w):
    return (x * lax.rsqrt(jnp.mean(x * x, axis=-1, keepdims=True) + RMS_EPS) * w).astype(BF16)


def _mla_q_kernel(cq_ref, nw_ref, cs_ref, w_ref, o_ref):
    n = _rms_bf16(cq_ref[...], nw_ref[...])
    cs = cs_ref[...]
    for h in range(w_ref.shape[0]):
        y = _dot(n, w_ref[h])
        rope = _rope_slab(y[:, HEAD_DIM:], cs)
        o_ref[h] = (jnp.concatenate([y[:, :HEAD_DIM], rope], axis=1) * QK_LOG2_SCALE).astype(o_ref.dtype)


def mla_queries(cfg, p, norm_w, w_q, cs):
    tm = _pick(cfg.nc, (512, 256))
    nh = cfg.mla_heads
    col = cfg.off_cq // cfg.q_lora
    return pl.pallas_call(
        _mla_q_kernel,
        grid=(cfg.nt // tm,),
        in_specs=[
            pl.BlockSpec((tm, cfg.q_lora), lambda i: (i, col)),
            pl.BlockSpec((1, cfg.q_lora), lambda i: (0, 0)),
            pl.BlockSpec((tm, HEAD_DIM), lambda i: (i, 0)),
            pl.BlockSpec((nh, cfg.q_lora, QK_DIM), lambda i: (0, 0, 0)),
        ],
        out_specs=pl.BlockSpec((nh, tm, QK_DIM), lambda i: (0, i, 0)),
        out_shape=jax.ShapeDtypeStruct((nh, cfg.nt, QK_DIM), BF16),
        compiler_params=_params(("arbitrary",)),
        name="mla_queries",
    )(p, norm_w.reshape(1, cfg.q_lora), cs, w_q)


def _mla_kv_kernel(ckv_ref, slab_ref, nw_ref, cs_ref, w_ref, k_ref, v_ref):
    n = _rms_bf16(ckv_ref[...], nw_ref[...])
    k_rope = _rope_slab(slab_ref[...], cs_ref[...])
    lane = lax.broadcasted_iota(jnp.int32, k_rope.shape, 1)
    ones_col = jnp.where(lane == 0, 1.0, 0.0)
    for h in range(w_ref.shape[0]):
        y = _dot(n, w_ref[h])
        k_ref[h] = jnp.concatenate([y[:, :HEAD_DIM], k_rope], axis=1).astype(k_ref.dtype)
        v_ref[h] = jnp.concatenate([y[:, HEAD_DIM:], ones_col], axis=1).astype(v_ref.dtype)


def mla_keys_values(cfg, p, norm_w, w_kv, cs):
    tm = _pick(cfg.nc, (512, 256))
    nh = cfg.mla_heads
    col = cfg.off_ckv // cfg.kv_lora
    slab_col = cfg.off_slab // HEAD_DIM
    return pl.pallas_call(
        _mla_kv_kernel,
        grid=(cfg.nt // tm,),
        in_specs=[
            pl.BlockSpec((tm, cfg.kv_lora), lambda i: (i, col)),
            pl.BlockSpec((tm, HEAD_DIM), lambda i: (i, slab_col)),
            pl.BlockSpec((1, cfg.kv_lora), lambda i: (0, 0)),
            pl.BlockSpec((tm, HEAD_DIM), lambda i: (i, 0)),
            pl.BlockSpec((nh, cfg.kv_lora, 2 * HEAD_DIM), lambda i: (0, 0, 0)),
        ],
        out_specs=[
            pl.BlockSpec((nh, tm, QK_DIM), lambda i: (0, i, 0)),
            pl.BlockSpec((nh, tm, 2 * HEAD_DIM), lambda i: (0, i, 0)),
        ],
        out_shape=[
            jax.ShapeDtypeStruct((nh, cfg.nt, QK_DIM), BF16),
            jax.ShapeDtypeStruct((nh, cfg.nt, 2 * HEAD_DIM), BF16),
        ],
        compiler_params=_params(("arbitrary",)),
        name="mla_keys_values",
    )(p, p, norm_w.reshape(1, cfg.kv_lora), cs, w_kv)


ATTN_SUB = 256
ATTN_KT = 512
ATTN_SCORE_ROWS = 1024


def _softmax_pv(s_list, v_list):
    m = functools.reduce(jnp.maximum, [jnp.max(s, axis=-1, keepdims=True) for s in s_list])
    o = None
    for s, v in zip(s_list, v_list):
        part = _dot(jnp.exp2(s - m).astype(BF16), v)
        o = part if o is None else o + part
    return o[:, :HEAD_DIM] / o[:, HEAD_DIM:HEAD_DIM + 1]


def _attn_lat_kernel(q_ref, kl_ref, kc_ref, vl_ref, vc_ref, o_ref, s_ref):
    n_keys = kl_ref.shape[0]
    n_slots = s_ref.shape[0] // ATTN_SUB
    for r in range(0, q_ref.shape[0], ATTN_SUB):
        rows = slice(r, r + ATTN_SUB)
        slot = (r // ATTN_SUB) % n_slots
        srows = slice(slot * ATTN_SUB, (slot + 1) * ATTN_SUB)
        q = q_ref[rows, :]
        sc = _dot_nt(q, kc_ref[...])
        m = jnp.max(sc, axis=-1, keepdims=True)
        for c in range(0, n_keys, ATTN_KT):
            s = _dot_nt(q, kl_ref[c:c + ATTN_KT, :])
            s_ref[srows, c:c + ATTN_KT] = s
            m = jnp.maximum(m, jnp.max(s, axis=-1, keepdims=True))
        o = _dot(jnp.exp2(sc - m).astype(BF16), vc_ref[...])
        for c in range(0, n_keys, ATTN_KT):
            p = jnp.exp2((s_ref[srows, c:c + ATTN_KT] - m).astype(BF16))
            o = o + _dot(p, vl_ref[c:c + ATTN_KT, :])
        o_ref[rows, :] = (o[:, :HEAD_DIM] / o[:, HEAD_DIM:HEAD_DIM + 1]).astype(o_ref.dtype)


def _attn_ctx_kernel(q_ref, kc_ref, vc_ref, o_ref):
    o_ref[...] = _softmax_pv([_dot_nt(q_ref[...], kc_ref[...])], [vc_ref[...]]).astype(o_ref.dtype)


def mla_attention(cfg, q, k, v, *, with_ctx, out_rows):
    nh = cfg.mla_heads
    vw = 2 * HEAD_DIM
    tq = _pick(cfg.seq, (2048, 1024, 512, 256))
    nq = cfg.seq // tq
    s_rows = min(tq, ATTN_SCORE_ROWS)
    cb0 = cfg.nl // cfg.ctx
    y_lat = pl.pallas_call(
        _attn_lat_kernel,
        grid=(cfg.batch, nh, nq),
        in_specs=[
            pl.BlockSpec((None, tq, QK_DIM), lambda b, h, i: (h, b * nq + i, 0)),
            pl.BlockSpec((None, cfg.seq, QK_DIM), lambda b, h, i: (h, b, 0)),
            pl.BlockSpec((None, cfg.ctx, QK_DIM), lambda b, h, i: (h, cb0 + b, 0)),
            pl.BlockSpec((None, cfg.seq, vw), lambda b, h, i: (h, b, 0)),
            pl.BlockSpec((None, cfg.ctx, vw), lambda b, h, i: (h, cb0 + b, 0)),
        ],
        out_specs=pl.BlockSpec((tq, HEAD_DIM), lambda b, h, i: (b * nq + i, h)),
        out_shape=jax.ShapeDtypeStruct((out_rows, nh * HEAD_DIM), BF16),
        scratch_shapes=[pltpu.VMEM((s_rows, cfg.seq), F32)],
        compiler_params=_params(("arbitrary", "arbitrary", "arbitrary")),
        name="mla_attention",
    )(q, k, k, v, v)
    if not with_ctx:
        return y_lat, None
    y_ctx = pl.pallas_call(
        _attn_ctx_kernel,
        grid=(cfg.batch, nh),
        in_specs=[
            pl.BlockSpec((None, cfg.ctx, QK_DIM), lambda b, h: (h, cb0 + b, 0)),
            pl.BlockSpec((None, cfg.ctx, QK_DIM), lambda b, h: (h, cb0 + b, 0)),
            pl.BlockSpec((None, cfg.ctx, vw), lambda b, h: (h, cb0 + b, 0)),
        ],
        out_specs=pl.BlockSpec((cfg.ctx, HEAD_DIM), lambda b, h: (b, h)),
        out_shape=jax.ShapeDtypeStruct((cfg.nc, nh * HEAD_DIM), BF16),
        compiler_params=_params(("arbitrary", "arbitrary")),
        name="mla_attention_ctx",
    )(q, k, v)
    return y_lat, y_ctx


def _merge_kernel(ya_ref, yb_ref, yc_ref, ga_ref, gb_ref, gc_ref, wa_ref, wb_ref, wc_ref, o_ref):
    m = (_sigmoid(ga_ref[...]) * _dot(ya_ref[...], wa_ref[...])
         + _sigmoid(gb_ref[...]) * _dot(yb_ref[...], wb_ref[...])
         + _sigmoid(gc_ref[...]) * _dot(yc_ref[...], wc_ref[...]))
    o_ref[...] = m.astype(o_ref.dtype)


def merge_branches(cfg, ya, yb, yc, p, wa, wb, wc, *, rows):
    tm = _pick(cfg.nc, (1024, 512, 256))
    tn = _pick(cfg.d, (512, 256))
    gcols = cfg.d // tn

    def gate_spec(k):
        return pl.BlockSpec((tm, tn), lambda i, j, k=k: (i, k * gcols + j))

    return pl.pallas_call(
        _merge_kernel,
        grid=(rows // tm, cfg.d // tn),
        in_specs=[
            pl.BlockSpec((tm, ya.shape[1]), lambda i, j: (i, 0)),
            pl.BlockSpec((tm, yb.shape[1]), lambda i, j: (i, 0)),
            pl.BlockSpec((tm, yc.shape[1]), lambda i, j: (i, 0)),
            gate_spec(0), gate_spec(1), gate_spec(2),
            pl.BlockSpec((wa.shape[0], tn), lambda i, j: (0, j)),
            pl.BlockSpec((wb.shape[0], tn), lambda i, j: (0, j)),
            pl.BlockSpec((wc.shape[0], tn), lambda i, j: (0, j)),
        ],
        out_specs=pl.BlockSpec((tm, tn), lambda i, j: (i, j)),
        out_shape=jax.ShapeDtypeStruct((rows, cfg.d), BF16),
        compiler_params=_params(("arbitrary", "arbitrary")),
        name="merge_branches",
    )(ya, yb, yc, p, p, p, wa, wb, wc)


def _matmul_resid_kernel(a_ref, w_ref, x_ref, mod_ref, o_ref, *, k):
    o_ref[...] = x_ref[...] + mod_ref[k:k + 1, :] * _dot(a_ref[...], w_ref[...])


def matmul_residual(cfg, a, w, x, mod_l, *, k, rows):
    tm = _pick(cfg.nc, (1024, 512, 256))
    tn = _pick(cfg.d, (1024, 512, 256))
    row_of = _mod_row_map(cfg, tm)
    return pl.pallas_call(
        functools.partial(_matmul_resid_kernel, k=k),
        grid=(rows // tm, cfg.d // tn),
        in_specs=[
            pl.BlockSpec((tm, a.shape[1]), lambda i, j: (i, 0)),
            pl.BlockSpec((a.shape[1], tn), lambda i, j: (0, j)),
            pl.BlockSpec((tm, tn), lambda i, j: (i, j)),
            pl.BlockSpec((None, 6, tn), lambda i, j: (row_of(i), 0, j)),
        ],
        out_specs=pl.BlockSpec((tm, tn), lambda i, j: (i, j)),
        out_shape=jax.ShapeDtypeStruct((rows, cfg.d), F32),
        compiler_params=_params(("arbitrary", "arbitrary")),
        name="matmul_residual",
    )(a, w, x, mod_l)


def _top16_pass(s_ref, top_ref, rank_ref, break_ties):
    s = s_ref[...]
    n = s.shape[0]
    row = lax.broadcasted_iota(jnp.int32, s.shape, 0).astype(F32)
    rank = jnp.full(s.shape, UNRANKED, F32)
    work = s
    for r in range(PEER_TOPK):
        m = jnp.max(work, axis=0, keepdims=True)
        sel = work == m
        if break_ties:
            sel = row == jnp.min(jnp.where(sel, row, float(n)), axis=0, keepdims=True)
        rank = jnp.where(sel, float(r), rank)
        work = jnp.where(sel, -jnp.inf, work)
        top_ref[r:r + 1, :] = m
    rank_ref[...] = rank


def _peer_tables_kernel(qh_ref, keys_ref, b_ref, e2_ref, c_ref, e1_ref, cnt_ref, *scratch, tile):
    lanes = HEAD_DIM
    n_arr = 2 * (tile // lanes)
    s_refs, top_refs, rank_refs = (scratch[i * n_arr:(i + 1) * n_arr] for i in range(3))
    k1 = keys_ref[0]
    k2 = keys_ref[1]
    ranked = jnp.zeros((1, lanes), F32)
    for part in range(tile // lanes):
        qh = qh_ref[part * lanes:(part + 1) * lanes, :].astype(BF16)
        s_refs[2 * part][...] = _dot_nt(k1, qh[:, :lanes])
        s_refs[2 * part + 1][...] = _dot_nt(k2, qh[:, lanes:])
        for idx in (2 * part, 2 * part + 1):
            _top16_pass(s_refs[idx], top_refs[idx], rank_refs[idx], break_ties=False)
            n_ranked = jnp.sum(jnp.where(rank_refs[idx][...] < UNRANKED, 1.0, 0.0), axis=0, keepdims=True)
            ranked = jnp.maximum(ranked, n_ranked)

    @pl.when(jnp.max(ranked) > float(PEER_TOPK))
    def _():
        for idx in range(n_arr):
            _top16_pass(s_refs[idx], top_refs[idx], rank_refs[idx], break_ties=True)

    for part in range(tile // lanes):
        cols = slice(part * lanes, (part + 1) * lanes)
        s1 = s_refs[2 * part][...]
        s2 = s_refs[2 * part + 1][...]
        rank1 = rank_refs[2 * part][...]
        rank2 = rank_refs[2 * part + 1][...]
        t1 = top_refs[2 * part][...]
        t2 = top_refs[2 * part + 1][...]
        pieces = [t1[0:1] + t2]
        pos = [lax.broadcasted_iota(jnp.int32, (PEER_TOPK, lanes), 0).astype(F32)]
        for a in range(1, 8):
            pieces.append(t1[a:a + 1] + t2[0:8])
            pos.append(lax.broadcasted_iota(jnp.int32, (8, lanes), 0).astype(F32) + float(a * PEER_TOPK))
        pieces.append(t1[8:16] + t2[0:1])
        pos.append((lax.broadcasted_iota(jnp.int32, (8, lanes), 0).astype(F32) + 8.0) * float(PEER_TOPK))
        cand = jnp.concatenate(pieces, axis=0)
        cpos = jnp.concatenate(pos, axis=0)
        a_row = lax.broadcasted_iota(jnp.int32, (PEER_TOPK, lanes), 0).astype(F32)
        cnt = jnp.zeros((PEER_TOPK, lanes), F32)
        z = jnp.zeros((1, lanes), F32)
        best0 = None
        for r in range(PEER_TOPK):
            m = jnp.max(cand, axis=0, keepdims=True)
            first = jnp.min(jnp.where(cand == m, cpos, 1e9), axis=0, keepdims=True)
            cand = jnp.where(cpos == first, -jnp.inf, cand)
            cnt = cnt + jnp.where(a_row == jnp.floor(first * (1.0 / PEER_TOPK)), 1.0, 0.0)
            if r == 0:
                best0 = m
            z = z + jnp.exp(m - best0)
        cnt_ref[...] = cnt
        c_tab = jnp.zeros((PEER_NKEYS, lanes), F32)
        for a in range(PEER_TOPK):
            c_tab = jnp.where(rank1 == float(a), cnt_ref[a:a + 1, :], c_tab)
        b_ref[:, cols] = rank2.astype(b_ref.dtype)
        c_ref[:, cols] = c_tab
        e1_ref[:, cols] = jnp.exp(s1 - t1[0:1]) / z
        e2_ref[:, cols] = jnp.exp(s2 - t2[0:1]).astype(e2_ref.dtype)


def peer_tables(cfg, qh, keys, *, rows):
    tile = _pick(cfg.nc, (512, 256))
    n_arr = 2 * (tile // HEAD_DIM)
    nh = cfg.peer_heads
    shapes = [jax.ShapeDtypeStruct((nh, PEER_NKEYS, rows), dt) for dt in (BF16, BF16, F32, F32)]
    out_spec = pl.BlockSpec((None, PEER_NKEYS, tile), lambda i, h: (h, 0, i))
    return pl.pallas_call(
        functools.partial(_peer_tables_kernel, tile=tile),
        grid=(rows // tile, nh),
        in_specs=[
            pl.BlockSpec((tile, 2 * HEAD_DIM), lambda i, h: (i, h)),
            pl.BlockSpec((2, None, PEER_NKEYS, HEAD_DIM), lambda i, h: (0, h, 0, 0)),
        ],
        out_specs=[out_spec] * 4,
        out_shape=shapes,
        scratch_shapes=([pltpu.VMEM((PEER_TOPK, HEAD_DIM), F32)]
                        + [pltpu.VMEM((PEER_NKEYS, HEAD_DIM), F32) for _ in range(n_arr)]
                        + [pltpu.VMEM((PEER_TOPK, HEAD_DIM), F32) for _ in range(n_arr)]
                        + [pltpu.VMEM((PEER_NKEYS, HEAD_DIM), F32) for _ in range(n_arr)]),
        compiler_params=_params(("arbitrary", "arbitrary")),
        name="peer_tables",
    )(qh, keys)


PEER_ROWS = 8
PEER_EB = PEER_ROWS * PEER_NKEYS
PEER_DOT_ROWS = 1


def _peer_dense_kernel(ht_ref, u_ref, vt_ref, b_ref, e2_ref, c_ref, e1_ref, x_ref, mod_ref, fw_ref, o_ref,
                       acc_ref, w_ref, *, heads, final):
    e = pl.program_id(1)

    @pl.when(e == 0)
    def _():
        acc_ref[...] = jnp.zeros_like(acc_ref)

    zero = jnp.zeros((), BF16)
    ht = ht_ref[...]
    for r0 in range(0, PEER_ROWS, PEER_DOT_ROWS):
        blk = slice(r0 * PEER_NKEYS, (r0 + PEER_DOT_ROWS) * PEER_NKEYS)
        act = _dot(u_ref[blk, :], ht)
        gelu = (0.5 * act * (1.0 + lax.erf(act * float(math.sqrt(0.5))))).astype(BF16)
        for q in range(PEER_DOT_ROWS):
            r = r0 + q
            g = jnp.zeros((PEER_NKEYS, act.shape[1]), BF16)
            for h in range(heads):
                crow = jnp.broadcast_to(c_ref[h, r:r + 1, :], g.shape).astype(BF16)
                erow = jnp.broadcast_to(e1_ref[h, r:r + 1, :], g.shape).astype(BF16)
                g = g + jnp.where(b_ref[h] < crow, e2_ref[h] * erow, zero)
            w_ref[r * PEER_NKEYS:(r + 1) * PEER_NKEYS, :] = g * gelu[q * PEER_NKEYS:(q + 1) * PEER_NKEYS]
    acc_ref[...] += _dot(vt_ref[...], w_ref[...])

    @pl.when(e == pl.num_programs(1) - 1)
    def _():
        y = x_ref[...] + mod_ref[5:6, :] * acc_ref[...].T
        if final:
            y = y * lax.rsqrt(jnp.mean(y * y, axis=-1, keepdims=True) + RMS_EPS) * fw_ref[...]
        o_ref[...] = y


def peer_dense(cfg, ht, u, vt, tabs, x, mod_l, final_w, *, layer, rows, final):
    tile = _pick(cfg.nc, (512, 256))
    eb = PEER_EB
    n_exp = u.shape[1]
    nh = cfg.peer_heads
    row_of = _mod_row_map(cfg, tile)
    tab_spec = pl.BlockSpec((nh, PEER_NKEYS, tile), lambda i, e: (0, 0, i))
    row_spec = pl.BlockSpec((nh, PEER_ROWS, tile), lambda i, e: (0, e, i))
    return pl.pallas_call(
        functools.partial(_peer_dense_kernel, heads=nh, final=final),
        grid=(rows // tile, n_exp // eb),
        in_specs=[
            pl.BlockSpec((cfg.d, tile), lambda i, e: (0, i)),
            pl.BlockSpec((None, eb, cfg.d), lambda i, e: (layer, e, 0)),
            pl.BlockSpec((None, cfg.d, eb), lambda i, e: (layer, 0, e)),
            tab_spec, tab_spec, row_spec, row_spec,
            pl.BlockSpec((tile, cfg.d), lambda i, e: (i, 0)),
            pl.BlockSpec((None, 6, cfg.d), lambda i, e: (row_of(i), 0, 0)),
            pl.BlockSpec((1, cfg.d), lambda i, e: (0, 0)),
        ],
        out_specs=pl.BlockSpec((tile, cfg.d), lambda i, e: (i, 0)),
        out_shape=jax.ShapeDtypeStruct((rows, cfg.d), F32),
        scratch_shapes=[pltpu.VMEM((cfg.d, tile), F32), pltpu.VMEM((eb, tile), BF16)],
        compiler_params=_params(("arbitrary", "arbitrary")),
        name="peer_dense",
    )(ht, u, vt, *tabs, x, mod_l, final_w.reshape(1, cfg.d))


def _rot_cols(w):
    q = ROPE_DIM // 4
    return jnp.concatenate([-w[..., q:2 * q], w[..., 0:q], -w[..., 3 * q:4 * q], w[..., 2 * q:3 * q]], axis=-1)


def _in_proj_weight(cfg, w_in, n_cols):
    d = cfg.d
    hg_end = 5 * cfg.hgw
    pool_end = hg_end + POOL_WIDTH
    cq_end = pool_end + cfg.q_lora
    ckv_end = cq_end + cfg.kv_lora
    rope_end = ckv_end + ROPE_DIM
    w_in = w_in.astype(BF16)
    k_rope = w_in[..., ckv_end:rope_end]
    parts = [w_in[..., rope_end:rope_end + 3 * d], w_in[..., :ckv_end], k_rope, _rot_cols(k_rope)]
    parts.append(jnp.zeros(w_in.shape[:-1] + (n_cols - cfg.in_cols,), BF16))
    return jnp.concatenate(parts, axis=-1)


def _mla_q_weight(cfg, w_uq):
    w = w_uq.reshape(cfg.q_lora, cfg.mla_heads, HEAD_DIM + ROPE_DIM)
    rope = w[..., HEAD_DIM:]
    w = jnp.concatenate([w[..., :HEAD_DIM], rope, _rot_cols(rope)], axis=-1)
    return jnp.transpose(w, (1, 0, 2)).astype(BF16)


def _mla_kv_weight(cfg, w_ukv):
    w = w_ukv.reshape(cfg.kv_lora, cfg.mla_heads, 2 * HEAD_DIM)
    return jnp.transpose(w, (1, 0, 2)).astype(BF16)


def _rope_table(cfg):
    rows = cfg.seq // cfg.grid_w
    r, col = jnp.meshgrid(jnp.arange(rows), jnp.arange(cfg.grid_w), indexing="ij")
    n_freq = ROPE_DIM // 4
    freqs = ROPE_THETA ** (-jnp.arange(n_freq, dtype=F32) / n_freq)
    ang_r = r.reshape(-1)[:, None] * freqs
    ang_c = col.reshape(-1)[:, None] * freqs
    cos = jnp.concatenate([jnp.cos(ang_r)] * 2 + [jnp.cos(ang_c)] * 2, axis=1)
    sin = jnp.concatenate([jnp.sin(ang_r)] * 2 + [jnp.sin(ang_c)] * 2, axis=1)
    lat = jnp.tile(jnp.concatenate([cos, sin], axis=1).astype(F32), (cfg.batch, 1))
    ctx = jnp.concatenate([jnp.ones((cfg.nc, ROPE_DIM), F32), jnp.zeros((cfg.nc, ROPE_DIM), F32)], axis=1)
    return jnp.concatenate([lat, ctx], axis=0)


def _forward(cfg, x, c, ctx, c_ctx, w_mod, b_mod, norm_mix, norm_ffn, w_in, hg_lb_logits, hg_norm,
             pool_w, pool_scale, mla_q_norm, mla_w_uq, mla_kv_norm, mla_w_ukv,
             w_branch_a, w_branch_b, w_branch_c, w_out, peer_wq, peer_keys, peer_u, peer_v, final_w):
    d = cfg.d
    assert cfg.seq % HG_CHUNK == 0 and cfg.ctx % HG_CHUNK == 0 and cfg.nl % cfg.ctx == 0
    assert cfg.off_pool % POOL_WIDTH == 0 and cfg.off_cq % cfg.q_lora == 0
    assert cfg.off_ckv % cfg.kv_lora == 0 and cfg.batch < MOD_ROWS
    tn_in = 1536 if d % 256 == 0 and cfg.in_cols > 8192 else 256
    n_cols = -(-cfg.in_cols // tn_in) * tn_in
    tm = _pick(cfg.nc, (1024, 512, 256))

    xs = jnp.concatenate([x.reshape(cfg.nl, d), ctx.reshape(cfg.nc, d)], axis=0)
    c_all = jnp.concatenate([c, c_ctx[None], jnp.zeros((MOD_ROWS - cfg.batch - 1, d), F32)], axis=0)
    mod = adaln_tables(cfg, c_all, w_mod, b_mod)
    cs = _rope_table(cfg)
    w_in_all = _in_proj_weight(cfg, w_in, n_cols)
    wq_all = peer_wq.astype(BF16)
    u_all = peer_u.astype(BF16)
    vt_all = jnp.swapaxes(peer_v.astype(BF16), 1, 2)

    for l in range(cfg.depth):
        last = l == cfg.depth - 1
        rows = cfg.nl if last else cfg.nt
        mod_l = mod[l]
        p = norm_matmul(cfg, xs, norm_mix[l], mod_l, w_in_all, layer=l,
                        k0=0, rows=cfg.nt, tm=tm, tn=tn_in, emit_h=False)
        ya, hg_ctx = hgrn_mixer(cfg, p, hg_lb_logits, hg_norm[l], l, rows)
        pw = pool_w[l].astype(BF16)
        yb = pool_mixer(cfg, p, pw, pool_scale[l], n_seq=cfg.batch, seq_len=cfg.seq, row0=0, out_rows=rows)
        q = mla_queries(cfg, p, mla_q_norm[l], _mla_q_weight(cfg, mla_w_uq[l]), cs)
        k, v = mla_keys_values(cfg, p, mla_kv_norm[l], _mla_kv_weight(cfg, mla_w_ukv[l]), cs)
        yc, att_ctx = mla_attention(cfg, q, k, v, with_ctx=not last, out_rows=rows)
        if not last:
            pool_ctx = pool_mixer(cfg, p, pw, pool_scale[l], n_seq=cfg.batch, seq_len=cfg.ctx, row0=cfg.nl)
            ya = lax.dynamic_update_slice(ya, hg_ctx, (cfg.nl, 0))
            yb = lax.dynamic_update_slice(yb, pool_ctx, (cfg.nl, 0))
            yc = lax.dynamic_update_slice(yc, att_ctx, (cfg.nl, 0))
        m = merge_branches(cfg, ya, yb, yc, p, w_branch_a[l].astype(BF16), w_branch_b[l].astype(BF16),
                           w_branch_c[l].astype(BF16), rows=rows)
        xs = matmul_residual(cfg, m, w_out[l].astype(BF16), xs, mod_l, k=2, rows=rows)
        qh, h2t = norm_matmul(cfg, xs, norm_ffn[l], mod_l, wq_all, layer=l,
                             k0=3, rows=rows, tm=tm, tn=_pick(peer_wq.shape[2], (1024, 512, 256)), emit_h=True)
        tabs = peer_tables(cfg, qh, peer_keys[l].astype(BF16), rows=rows)
        xs = peer_dense(cfg, h2t, u_all, vt_all, tabs, xs, mod_l, final_w, layer=l, rows=rows, final=last)

    return xs.reshape(cfg.batch, cfg.seq, d)


def kernel(x, c, ctx, c_ctx, w_mod, b_mod, norm_mix, norm_ffn, w_in, hg_lb_logits, hg_norm, pool_w, pool_scale,
           mla_q_norm, mla_w_uq, mla_kv_norm, mla_w_ukv, w_branch_a, w_branch_b, w_branch_c, w_out,
           peer_wq, peer_keys, peer_u, peer_v, final_norm):
    batch, seq, d = x.shape
    cfg = Cfg(d=d, batch=batch, seq=seq, ctx=ctx.shape[1], grid_w=64, depth=w_mod.shape[0],
              hg_heads=hg_lb_logits.shape[2] // HEAD_DIM,
              mla_heads=mla_w_ukv.shape[2] // (2 * HEAD_DIM), q_lora=mla_q_norm.shape[1],
              kv_lora=mla_kv_norm.shape[1], peer_heads=peer_keys.shape[2])
    return _forward(cfg, x, c, ctx, c_ctx, w_mod, b_mod, norm_mix, norm_ffn, w_in, hg_lb_logits, hg_norm,
                    pool_w, pool_scale, mla_q_norm, mla_w_uq, mla_kv_norm, mla_w_ukv,
                    w_branch_a, w_branch_b, w_branch_c, w_out, peer_wq, peer_keys, peer_u, peer_v, final_norm)
```
